```python
import jax, jax.numpy as jnp
from jax import lax
import numpy as np

D_MODEL = 1024
BATCH = 8
SEQ = 2048
DEPTH = 2
DEC_BATCH = 32
DEC_SEQ = 1
PAST_LEN = 8192
PAGE_SIZE = 128

HEAD_DIM = 64
DSWA_GROUPS = ((128, 1), (512, 4), (2048, 16))
HEADS_PER_GROUP = 2
A_HEADS = HEADS_PER_GROUP * len(DSWA_GROUPS)
A_WIDTH = A_HEADS * HEAD_DIM
A_OUT = HEADS_PER_GROUP * HEAD_DIM
CONV_WIDTH = 384
CONV_K = 3
C_HEADS = 4
C_DK = 64
C_DV = 64
C_WIDTH = C_HEADS * C_DK
CHUNK = 64
N_BRANCH = 3
D_FF = 4 * D_MODEL
ROPE_THETA = 10000.0
EPS = 1e-6
NEG_BIG = -1e30
LB_FLOOR = 1e-30
IN_COLS = 3 * A_WIDTH + 3 * CONV_WIDTH + 4 * C_WIDTH + N_BRANCH * D_MODEL

kernel_name = 'hybrid_dilated_conv_hgrn2_step'


def rmsnorm(x, g):
    xf = x.astype(jnp.float32)
    y = xf * lax.rsqrt(jnp.mean(xf * xf, axis=-1, keepdims=True) + EPS)
    return (y * g.astype(jnp.float32)).astype(x.dtype)


def split_in(z):
    sizes = [A_WIDTH] * 3 + [CONV_WIDTH] * 3 + [C_WIDTH] * 4 + [N_BRANCH * D_MODEL]
    idx, acc = [], 0
    for s in sizes[:-1]:
        acc += s
        idx.append(acc)
    return jnp.split(z, idx, axis=-1)


def rope(x, pos):
    half = HEAD_DIM // 2
    inv = ROPE_THETA ** (-jnp.arange(half, dtype=jnp.float32) / half)
    ang = pos.astype(jnp.float32)[:, None] * inv[None, :]
    cos = jnp.cos(ang)[None, :, None, :]
    sin = jnp.sin(ang)[None, :, None, :]
    xf = x.astype(jnp.float32)
    x1, x2 = xf[..., :half], xf[..., half:]
    return jnp.concatenate([x1 * cos - x2 * sin, x2 * cos + x1 * sin], axis=-1).astype(x.dtype)


def masked_softmax(s, valid):
    s = jnp.where(valid, s.astype(jnp.float32) * (HEAD_DIM ** -0.5), NEG_BIG)
    lse = jax.nn.logsumexp(s, axis=-1, keepdims=True)
    return jnp.exp(s - lse), lse[..., 0]


def dilated_band_attn(q, k, v, window, dil):
    bsz, T, H, dh = q.shape
    band = window // dil
    n = T // dil
    nb = -(-n // band)
    n_pad = nb * band

    def by_residue(t, front):
        t = t.reshape(bsz, n, dil, H, dh).transpose(0, 2, 1, 3, 4)
        return jnp.pad(t, ((0, 0), (0, 0), (front, n_pad - n), (0, 0), (0, 0)))

    qb = by_residue(q, 0).reshape(bsz, dil, nb, band, H, dh)
    kb = by_residue(k, band).reshape(bsz, dil, nb + 1, band, H, dh)
    vb = by_residue(v, band).reshape(bsz, dil, nb + 1, band, H, dh)
    kk = jnp.concatenate([kb[:, :, :-1], kb[:, :, 1:]], axis=3)
    vv = jnp.concatenate([vb[:, :, :-1], vb[:, :, 1:]], axis=3)
    s = jnp.einsum('brnqhd,brnkhd->brnhqk', qb, kk)
    qi = jnp.arange(band)[:, None]
    kj = jnp.arange(2 * band)[None, :]
    dist = qi + band - kj
    key_idx = jnp.arange(nb)[:, None, None] * band + kj[None] - band
    valid = (dist >= 0) & (dist <= band) & (key_idx >= 0)
    p, lse = masked_softmax(s, valid[None, None, :, None])
    o = jnp.einsum('brnhqk,brnkhd->brnqhd', p.astype(v.dtype), vv)
    o = o.reshape(bsz, dil, n_pad, H, dh)[:, :, :n].transpose(0, 2, 1, 3, 4).reshape(bsz, T, H, dh)
    lse = lse.transpose(0, 1, 2, 4, 3).reshape(bsz, dil, n_pad, H)[:, :, :n]
    lse = lse.transpose(0, 2, 1, 3).reshape(bsz, T, H)
    return o, lse


def dilated_gather_attn(q, kv_ext, window, dil, pos):
    T = q.shape[1]
    band = window // dil
    steps = jnp.arange(band + 1)
    idx = window + jnp.arange(T)[:, None] - dil * steps[None, :]
    key_pos = pos[:, None] - dil * steps[None, :]
    kv = kv_ext[:, idx]
    s = jnp.einsum('bthd,btkhd->bhtk', q, kv[:, :, :, 0])
    p, lse = masked_softmax(s, (key_pos >= 0)[None, None])
    o = jnp.einsum('bhtk,btkhd->bthd', p.astype(kv.dtype), kv[:, :, :, 1])
    return o, lse.transpose(0, 2, 1)


def hgrn2_chunked(q, log_f, k, v, s0):
    bsz, T, H, dk = q.shape
    dv = v.shape[-1]
    nc = -(-T // CHUNK)
    tp = nc * CHUNK

    def chunks(t):
        t = jnp.pad(t.astype(jnp.float32), ((0, 0), (0, tp - T), (0, 0), (0, 0)))
        return t.reshape(bsz, nc, CHUNK, H, t.shape[-1]).transpose(1, 0, 3, 2, 4)

    qc, bc, kc, vc = chunks(q), jnp.cumsum(chunks(log_f), axis=3), chunks(k), chunks(v)
    causal = jnp.tril(jnp.ones((CHUNK, CHUNK), dtype=bool))

    def step(S, inp):
        qi, bi, ki, vi = inp
        diff = bi[:, :, :, None, :] - bi[:, :, None, :, :]
        decay = jnp.exp(jnp.where(causal[:, :, None], diff, NEG_BIG))
        att = jnp.einsum('bhtc,bhsc,bhtsc->bhts', qi, ki, decay)
        o = jnp.einsum('bhts,bhsv->bhtv', att, vi) + jnp.einsum('bhtc,bhcv->bhtv', qi * jnp.exp(bi), S)
        bl = bi[:, :, -1:, :]
        S = jnp.exp(bl)[:, :, 0, :, None] * S + jnp.einsum('bhsc,bhsv->bhcv', ki * jnp.exp(bl - bi), vi)
        return S, o

    s_fin, o = lax.scan(step, s0.astype(jnp.float32), (qc, bc, kc, vc))
    o = o.transpose(1, 0, 3, 2, 4).reshape(bsz, tp, H, dv)[:, :T]
    return o, s_fin


def trunk_layer(x, pos, kv_prev, conv_prev, hgrn_prev, gather,
                w_in, b_gate, norm1, conv_w, lower, hgrn_norm, w_pa, w_pb, w_pc, w_o, norm2, w_up, w_down):
    bsz, T, _ = x.shape
    h = rmsnorm(x, norm1)
    qa, ka, va, hb, gate_b, gate_c, qc, fc, ic, og, zg = split_in(h @ w_in)

    qa = rope(qa.reshape(bsz, T, A_HEADS, HEAD_DIM), pos)
    ka = rope(ka.reshape(bsz, T, A_HEADS, HEAD_DIM), pos)
    va = va.reshape(bsz, T, A_HEADS, HEAD_DIM)
    outs, lses, new_kv = [], [], []
    for gi, (win, dil) in enumerate(DSWA_GROUPS):
        sl = slice(gi * HEADS_PER_GROUP, (gi + 1) * HEADS_PER_GROUP)
        kv_new = jnp.stack([ka[:, :, sl], va[:, :, sl]], axis=2)
        kv_ext = jnp.concatenate([kv_prev[gi], kv_new], axis=1)
        if gather:
            o_g, lse_g = dilated_gather_attn(qa[:, :, sl], kv_ext, win, dil, pos)
        else:
            o_g, lse_g = dilated_band_attn(qa[:, :, sl], ka[:, :, sl], va[:, :, sl], win, dil)
        outs.append(o_g.astype(jnp.float32))
        lses.append(lse_g)
        new_kv.append(kv_ext[:, -win:])
    wts = jax.nn.softmax(jnp.stack(lses, axis=0), axis=0)
    ya = jnp.sum(wts[..., None] * jnp.stack(outs, axis=0), axis=0)
    ya = ya.reshape(bsz, T, A_OUT).astype(x.dtype)

    u = gate_c * hb
    u_ext = jnp.concatenate([conv_prev, u], axis=1)
    conv = sum(conv_w[j] * u_ext[:, j:j + T] for j in range(CONV_K))
    yb = gate_b * conv
    new_conv = u_ext[:, -(CONV_K - 1):]

    zf = fc.astype(jnp.float32).reshape(bsz, T, C_HEADS, C_DK)
    lb = lower.reshape(C_HEADS, C_DK)
    log_f = jnp.logaddexp(jnp.log(jnp.maximum(lb, LB_FLOOR)), jnp.log1p(-lb) + jax.nn.log_sigmoid(zf))
    kgate = (1.0 - lb) * jax.nn.sigmoid(-zf)
    oc, s_fin = hgrn2_chunked(qc.reshape(bsz, T, C_HEADS, C_DK), log_f, kgate,
                              ic.reshape(bsz, T, C_HEADS, C_DV), hgrn_prev)
    oc = oc * lax.rsqrt(jnp.mean(oc * oc, axis=-1, keepdims=True) + EPS)
    oc = oc * hgrn_norm.astype(jnp.float32).reshape(C_HEADS, C_DV)
    oc = oc * jax.nn.silu(og.astype(jnp.float32).reshape(bsz, T, C_HEADS, C_DV))
    yc = oc.reshape(bsz, T, C_WIDTH).astype(x.dtype)

    g = jax.nn.sigmoid((zg + b_gate).astype(jnp.float32)).reshape(bsz, T, N_BRANCH, D_MODEL)
    mix = g[:, :, 0] * (ya @ w_pa) + g[:, :, 1] * (yb @ w_pb) + g[:, :, 2] * (yc @ w_pc)
    x = x + mix.astype(x.dtype) @ w_o

    h2 = rmsnorm(x, norm2)
    x = x + jnp.square(jax.nn.relu(h2 @ w_up)) @ w_down
    return x, (new_kv[0], new_kv[1], new_kv[2], new_conv, s_fin.astype(x.dtype))


def setup_inputs(seed: int = 0) -> dict:
    key = jax.random.key(seed)
    ks = jax.random.split(key, 24)
    d = D_MODEL

    def nrm(k, shape, scale=1.0):
        return jax.random.normal(k, shape, jnp.float32) * scale

    def kv_shape(win):
        return (DEPTH, DEC_BATCH, win, 2, HEADS_PER_GROUP, HEAD_DIM)

    return {
        'x_prompt': nrm(ks[0], (BATCH, SEQ, d)),
        'x_sample': nrm(ks[1], (DEC_BATCH, DEC_SEQ, d)),
        'cache_kv_w128': nrm(ks[2], kv_shape(DSWA_GROUPS[0][0])),
        'cache_kv_w512': nrm(ks[3], kv_shape(DSWA_GROUPS[1][0])),
        'cache_kv_w2048': nrm(ks[4], kv_shape(DSWA_GROUPS[2][0])),
        'state_conv': nrm(ks[5], (DEPTH, DEC_BATCH, CONV_K - 1, CONV_WIDTH)),
        'state_hgrn': nrm(ks[6], (DEPTH, DEC_BATCH, C_HEADS, C_DK, C_DV), 0.3),
        'w_in': nrm(ks[7], (DEPTH, d, IN_COLS), d ** -0.5),
        'b_gate': nrm(ks[8], (DEPTH, N_BRANCH * d), 0.1),
        'norm1': 1.0 + nrm(ks[9], (DEPTH, d), 0.05),
        'conv_w': nrm(ks[10], (DEPTH, CONV_K, CONV_WIDTH), CONV_K ** -0.5),
        'hgrn_lb': nrm(ks[11], (DEPTH, C_WIDTH), 0.5),
        'hgrn_norm': 1.0 + nrm(ks[12], (DEPTH, C_WIDTH), 0.05),
        'w_pa': nrm(ks[13], (DEPTH, A_OUT, d), A_OUT ** -0.5),
        'w_pb': nrm(ks[14], (DEPTH, CONV_WIDTH, d), CONV_WIDTH ** -0.5),
        'w_pc': nrm(ks[15], (DEPTH, C_WIDTH, d), C_WIDTH ** -0.5),
        'w_o': nrm(ks[16], (DEPTH, d, d), d ** -0.5),
        'norm2': 1.0 + nrm(ks[17], (DEPTH, d), 0.05),
        'w_up': nrm(ks[18], (DEPTH, d, D_FF), d ** -0.5),
        'w_down': nrm(ks[19], (DEPTH, D_FF, d), D_FF ** -0.5),
        'norm_f': 1.0 + nrm(ks[20], (d,), 0.05),
    }


def reference(x_prompt, x_sample, cache_kv_w128, cache_kv_w512, cache_kv_w2048, state_conv, state_hgrn,
              w_in, b_gate, norm1, conv_w, hgrn_lb, hgrn_norm, w_pa, w_pb, w_pc, w_o, norm2, w_up, w_down,
              norm_f):
    sm = jax.nn.softmax(hgrn_lb.astype(jnp.float32), axis=0)
    lower = jnp.cumsum(sm, axis=0) - sm[0:1]
    bp, tp = x_prompt.shape[:2]
    bs, ts = x_sample.shape[:2]
    pos_p = jnp.arange(tp, dtype=jnp.int32)
    pos_s = PAST_LEN + jnp.arange(ts, dtype=jnp.int32)
    caches = (cache_kv_w128, cache_kv_w512, cache_kv_w2048)
    xp, xs = x_prompt, x_sample
    new_p = [[] for _ in range(5)]
    new_s = [[] for _ in range(5)]
    for l in range(DEPTH):
        params = (w_in[l], b_gate[l], norm1[l], conv_w[l], lower[l], hgrn_norm[l], w_pa[l], w_pb[l],
                  w_pc[l], w_o[l], norm2[l], w_up[l], w_down[l])
        zero_kv = [jnp.zeros((bp, win, 2, HEADS_PER_GROUP, HEAD_DIM), xp.dtype) for win, _ in DSWA_GROUPS]
        xp, st_p = trunk_layer(xp, pos_p, zero_kv, jnp.zeros((bp, CONV_K - 1, CONV_WIDTH), xp.dtype),
                               jnp.zeros((bp, C_HEADS, C_DK, C_DV), xp.dtype), False, *params)
        xs, st_s = trunk_layer(xs, pos_s, [c[l] for c in caches], state_conv[l], state_hgrn[l], True, *params)
        for i in range(5):
            new_p[i].append(st_p[i])
            new_s[i].append(st_s[i])
    y_prompt = rmsnorm(xp, norm_f)
    y_sample = rmsnorm(xs, norm_f)
    kvp128, kvp512, kvp2048, conv_p, hgrn_p = [jnp.stack(s, axis=0) for s in new_p]
    kvs128, kvs512, kvs2048, conv_s, hgrn_s = [jnp.stack(s, axis=0) for s in new_s]
    return (y_prompt, y_sample, kvp128, kvp512, kvp2048, conv_p, hgrn_p, kvs128, kvs512, kvs2048, conv_s, hgrn_s)
```

```python
import functools

import jax
import jax.numpy as jnp
from jax import lax
from jax.experimental import pallas as pl
from jax.experimental.pallas import tpu as pltpu

F32 = jnp.float32
BF16 = jnp.bfloat16

D_MODEL = 1024
HEAD_DIM = 64
DSWA_GROUPS = ((128, 1), (512, 4), (2048, 16))
BAND = 128
GROUP_W = 2 * HEAD_DIM
A_WIDTH = 3 * GROUP_W
CONV_WIDTH = 384
CONV_K = 3
C_HEADS = 4
C_WIDTH = C_HEADS * 64
N_BRANCH = 3
D_FF = 4 * D_MODEL
ROPE_THETA = 10000.0
EPS = 1e-6
NEG_BIG = -1e30
LB_FLOOR = 1e-30
PAST_LEN = 8192
IN_COLS = 3 * A_WIDTH + 3 * CONV_WIDTH + 4 * C_WIDTH + N_BRANCH * D_MODEL
COL_B = 3 * A_WIDTH
COL_C = COL_B + 3 * CONV_WIDTH
COL_G = COL_C + 4 * C_WIDTH
CHUNK = 64
SUB = 8
VMEM_LIMIT = 56 * 1024 * 1024


def _params(n_grid):
    return pltpu.CompilerParams(dimension_semantics=("arbitrary",) * n_grid,
                                vmem_limit_bytes=VMEM_LIMIT)


def _const_spec(shape):
    nd = len(shape)
    return pl.BlockSpec(shape, lambda *_: (0,) * nd, pipeline_mode=pl.Buffered(1))


def _dot(a, b):
    return jnp.dot(a.astype(BF16), b.astype(BF16), preferred_element_type=F32)


def _dot_nt(a, b):
    return lax.dot_general(a.astype(BF16), b.astype(BF16), (((1,), (1,)), ((), ())),
                           preferred_element_type=F32)


def _dot_tn(a, b):
    return lax.dot_general(a.astype(BF16), b.astype(BF16), (((0,), (0,)), ((), ())),
                           preferred_element_type=F32)


def _rmsnorm(x, g):
    return x * lax.rsqrt(jnp.mean(x * x, axis=-1, keepdims=True) + EPS) * g


def _inproj_body(x_ref, g_ref, w_ref, cos_ref, sin_ref, q_ref, k_ref, v_ref, zb_ref, zc_ref, zg_ref):
    h = _rmsnorm(x_ref[...], g_ref[...]).astype(BF16)
    cos = cos_ref[...]
    sin = sin_ref[...]
    lane = lax.broadcasted_iota(jnp.int32, cos.shape, 1)
    first_half = (lane % HEAD_DIM) < HEAD_DIM // 2

    def proj(c0, c1):
        return jnp.dot(h, w_ref[:, c0:c1], preferred_element_type=F32)

    def rope_store(out_ref, c0):
        for g in range(3):
            z = proj(c0 + g * GROUP_W, c0 + (g + 1) * GROUP_W)
            partner = jnp.where(first_half, pltpu.roll(z, GROUP_W - HEAD_DIM // 2, 1),
                                pltpu.roll(z, HEAD_DIM // 2, 1))
            out_ref[g] = z * cos + partner * sin

    rope_store(q_ref, 0)
    rope_store(k_ref, A_WIDTH)
    for g in range(3):
        v_ref[g] = proj(2 * A_WIDTH + g * GROUP_W, 2 * A_WIDTH + (g + 1) * GROUP_W)
    zb_ref[...] = proj(COL_B, COL_C)
    zc_ref[...] = proj(COL_C, COL_G)
    zg_ref[...] = proj(COL_G, IN_COLS)


def _inproj(x2d, norm, w_bf, cos, sin, tm):
    m = x2d.shape[0]
    n_pos = cos.shape[0] // tm
    row = lambda w: pl.BlockSpec((tm, w), lambda i: (i, 0))
    tab = pl.BlockSpec((tm, GROUP_W), lambda i: (i % n_pos, 0))
    widths = (3 * CONV_WIDTH, 4 * C_WIDTH, N_BRANCH * D_MODEL)
    grp = pl.BlockSpec((3, tm, GROUP_W), lambda i: (0, i, 0))
    return pl.pallas_call(
        _inproj_body,
        grid=(m // tm,),
        in_specs=[row(D_MODEL), _const_spec((1, D_MODEL)), _const_spec((D_MODEL, IN_COLS)), tab, tab],
        out_specs=[grp] * 3 + [row(w) for w in widths],
        out_shape=[jax.ShapeDtypeStruct((3, m, GROUP_W), F32)] * 3
        + [jax.ShapeDtypeStruct((m, w), F32) for w in widths],
        compiler_params=_params(1),
        name="inproj",
    )(x2d, norm.reshape(1, D_MODEL), w_bf, cos, sin)


def _rope_tables(pos):
    half = HEAD_DIM // 2
    inv = ROPE_THETA ** (-jnp.arange(half, dtype=F32) / half)
    ang = pos.astype(F32)[:, None] * inv[None, :]
    cos, sin = jnp.cos(ang), jnp.sin(ang)
    return jnp.tile(cos, (1, 4)), jnp.tile(jnp.concatenate([-sin, sin], axis=1), (1, 2))


def _softmax_tile(q, blocks):
    lane = lax.broadcasted_iota(jnp.int32, (BAND, GROUP_W), 1)
    head0 = lane < HEAD_DIM
    outs, lses = [], []
    for h in range(2):
        qh = jnp.where(head0 if h == 0 else ~head0, q, 0.0)
        s = [jnp.where(valid, _dot_nt(qh, k) * (HEAD_DIM ** -0.5), NEG_BIG) for k, _, valid in blocks]
        m = functools.reduce(jnp.maximum, [jnp.max(x, axis=-1, keepdims=True) for x in s])
        p = [jnp.exp(x - m) for x in s]
        l = functools.reduce(jnp.add, [jnp.sum(x, axis=-1, keepdims=True) for x in p])
        o = functools.reduce(jnp.add, [_dot(x, v) for x, (_, v, _) in zip(p, blocks)])
        outs.append(o / l)
        lses.append(m + jnp.log(l))
    return jnp.where(head0, outs[0], outs[1]), jnp.where(head0, lses[0], lses[1])


def _attn_prompt_body(q_ref, k_ref, v_ref, ya_ref, c0_ref, c1_ref, c2_ref, o_scr, lse_scr, *, seq):
    qi = lax.broadcasted_iota(jnp.int32, (BAND, BAND), 0)
    kj = lax.broadcasted_iota(jnp.int32, (BAND, BAND), 1)
    cur_valid = kj <= qi
    prev_valid = kj >= qi

    def rows(start, dil):
        return pl.ds(start, BAND, stride=dil) if dil > 1 else pl.ds(start, BAND)

    def tile(g, dil, start, prev_start, has_prev):
        r = rows(start, dil)
        blocks = [(k_ref[g, r, :], v_ref[g, r, :], cur_valid)]
        if prev_start is not None:
            rp = rows(prev_start, dil)
            blocks.append((k_ref[g, rp, :], v_ref[g, rp, :], prev_valid & has_prev))
        o, lse = _softmax_tile(q_ref[g, r, :], blocks)
        o_scr[g, r, :] = o
        lse_scr[g, r, :] = lse

    def dense_block(i, carry):
        start = pl.multiple_of(i * BAND, BAND)
        prev_start = pl.multiple_of(jnp.maximum(i - 1, 0) * BAND, BAND)
        tile(0, 1, start, prev_start, i > 0)
        return carry

    lax.fori_loop(0, seq // BAND, dense_block, 0)

    for g, (_, dil) in enumerate(DSWA_GROUPS):
        if dil == 1:
            continue
        n_blocks = seq // dil // BAND
        for r in range(dil):
            for i in range(n_blocks):
                start = r + i * BAND * dil
                tile(g, dil, start, start - BAND * dil if i > 0 else None, True)

    step = 256

    def merge(i, carry):
        sl = pl.ds(pl.multiple_of(i * step, step), step)
        lse = [lse_scr[g, sl, :] for g in range(3)]
        m = jnp.maximum(jnp.maximum(lse[0], lse[1]), lse[2])
        e = [jnp.exp(x - m) for x in lse]
        acc = e[0] * o_scr[0, sl, :] + e[1] * o_scr[1, sl, :] + e[2] * o_scr[2, sl, :]
        ya_ref[0, sl, :] = acc / (e[0] + e[1] + e[2])
        return carry

    lax.fori_loop(0, seq // step, merge, 0)

    for g, (c_ref, (win, _)) in enumerate(zip((c0_ref, c1_ref, c2_ref), DSWA_GROUPS)):
        c_ref[:, 0:GROUP_W] = k_ref[g, seq - win:seq, :]
        c_ref[:, GROUP_W:2 * GROUP_W] = v_ref[g, seq - win:seq, :]


def _attn_prompt(q, k, v, layer, caches):
    _, bsz, seq, _ = q.shape
    depth = 2
    assert seq % (BAND * 16) == 0 and seq >= 2048
    qkv_spec = pl.BlockSpec((3, None, seq, GROUP_W), lambda b: (0, b, 0, 0))
    cache_specs = [pl.BlockSpec((None, None, win, 2 * GROUP_W), lambda b: (layer, b, 0, 0))
                   for win, _ in DSWA_GROUPS]
    cache_shapes = [jax.ShapeDtypeStruct((depth, bsz, win, 2 * GROUP_W), F32) for win, _ in DSWA_GROUPS]
    in_specs = [qkv_spec] * 3
    args = [q, k, v]
    aliases = {}
    if caches is not None:
        in_specs = in_specs + [pl.BlockSpec(memory_space=pl.ANY)] * 3
        args = args + list(caches)
        aliases = {3: 1, 4: 2, 5: 3}

    def body(*refs):
        ins, rest = refs[:3], refs[3 + (3 if caches is not None else 0):]
        _attn_prompt_body(*ins, *rest, seq=seq)

    return pl.pallas_call(
        body,
        grid=(bsz,),
        in_specs=in_specs,
        out_specs=[pl.BlockSpec((1, seq, GROUP_W), lambda b: (b, 0, 0))] + cache_specs,
        out_shape=[jax.ShapeDtypeStruct((bsz, seq, GROUP_W), F32)] + cache_shapes,
        scratch_shapes=[pltpu.VMEM((3, seq, GROUP_W), F32), pltpu.VMEM((3, seq, GROUP_W), F32)],
        input_output_aliases=aliases,
        compiler_params=_params(1),
        name="attn_prompt",
    )(*args)


def _log_forget_and_kgate(zf, lb):
    log_sig = -(jnp.maximum(-zf, 0.0) + jnp.log1p(jnp.exp(-jnp.abs(zf))))
    a = jnp.log(jnp.maximum(lb, LB_FLOOR))
    c = jnp.log1p(-lb) + log_sig
    log_f = jnp.maximum(a, c) + jnp.log1p(jnp.exp(-jnp.abs(a - c)))
    kgate = (1.0 - lb) * jax.nn.sigmoid(-zf)
    return log_f, kgate


def _silu(x):
    return x * jax.nn.sigmoid(x)


def _split3(x):
    hi = x.astype(BF16)
    r1 = x - hi.astype(F32)
    mid = r1.astype(BF16)
    lo = (r1 - mid.astype(F32)).astype(BF16)
    return hi, mid, lo


def _hgrn_chunk(q, zf, v, og, st, lb, gnorm, kbv_scr):
    n = CHUNK
    row = lax.broadcasted_iota(jnp.int32, (n, C_WIDTH), 0)
    lane_head = lax.broadcasted_iota(jnp.int32, (n, C_WIDTH), 1) // 64
    log_f, k = _log_forget_and_kgate(zf, lb)

    tri = (lax.broadcasted_iota(jnp.int32, (n, n), 1) <= lax.broadcasted_iota(jnp.int32, (n, n), 0))
    tri = jnp.where(tri, 1.0, 0.0).astype(BF16)
    b = functools.reduce(jnp.add, [jnp.dot(tri, piece, preferred_element_type=F32) for piece in _split3(log_f)])

    hr = lax.broadcasted_iota(jnp.int32, (C_WIDTH, C_WIDTH), 0) // 64
    hc = lax.broadcasted_iota(jnp.int32, (C_WIDTH, C_WIDTH), 1) // 64
    same_head = hr == hc
    ones_bd = jnp.where(same_head, 1.0, 0.0).astype(BF16)

    kbv_scr[0, SUB:SUB + n, :] = k
    kbv_scr[1, SUB:SUB + n, :] = b
    kbv_scr[2, SUB:SUB + n, :] = v
    o = jnp.dot((q * k).astype(BF16), ones_bd, preferred_element_type=F32) * v
    for d in range(1, SUB):
        sl = pl.ds(SUB - d, n)
        k_d, b_d, v_d = kbv_scr[0, sl, :], kbv_scr[1, sl, :], kbv_scr[2, sl, :]
        in_tile = (row % SUB) >= d
        y = jnp.where(in_tile, q * k_d * jnp.exp(jnp.minimum(b - b_d, 0.0)), 0.0)
        o = o + jnp.dot(y.astype(BF16), ones_bd, preferred_element_type=F32) * v_d

    srow = lax.broadcasted_iota(jnp.int32, (C_HEADS * n, n), 0) % n
    scol = lax.broadcasted_iota(jnp.int32, (C_HEADS * n, n), 1)
    att = jnp.zeros((C_HEADS * n, n), F32)
    m = n // 2
    while m >= SUB:
        ref_q, ref_k = [], []
        for j in range(n // m):
            own = b[j * m:(j + 1) * m, :]
            if j % 2 == 1:
                ref_q.append(jnp.broadcast_to(b[j * m - 1:j * m, :], (m, C_WIDTH)))
                ref_k.append(own)
            else:
                ref_q.append(own)
                ref_k.append(jnp.broadcast_to(b[(j + 1) * m - 1:(j + 1) * m, :], (m, C_WIDTH)))
        upper = ((row // m) % 2) == 1
        qt = jnp.where(upper, q * jnp.exp(jnp.minimum(b - jnp.concatenate(ref_q, axis=0), 0.0)), 0.0)
        kt = jnp.where(upper, 0.0, k * jnp.exp(jnp.minimum(jnp.concatenate(ref_k, axis=0) - b, 0.0)))
        q_stack = jnp.concatenate([jnp.where(lane_head == h, qt, 0.0) for h in range(C_HEADS)], axis=0)
        a = _dot_nt(q_stack, kt)
        att = att + jnp.where((srow // (2 * m)) == (scol // (2 * m)), a, 0.0)
        m //= 2
    r = _dot(att, v)
    for h in range(C_HEADS):
        o = o + jnp.where(lane_head == h, r[h * n:(h + 1) * n, :], 0.0)

    o = o + _dot_nt(q * jnp.exp(b), st)
    b_last = b[n - 1:n, :]
    k_hat = k * jnp.exp(b_last - b)
    st = st * jnp.exp(b_last) + jnp.where(same_head, _dot_tn(v, k_hat), 0.0)

    ms = jnp.zeros_like(o)
    for h in range(C_HEADS):
        sel = lane_head == h
        ms = ms + jnp.where(sel, jnp.sum(jnp.where(sel, o * o, 0.0), axis=-1, keepdims=True), 0.0)
    o = o * lax.rsqrt(ms * (1.0 / 64) + EPS) * gnorm * _silu(og)
    return o, st


def _hgrn_prompt_body(zc_ref, lb_ref, gn_ref, yc_ref, st_ref, st_scr, kbv_scr, *, rows):
    t = pl.program_id(1)
    kbv_scr[:, 0:SUB, :] = jnp.zeros((3, SUB, C_WIDTH), F32)

    @pl.when(t == 0)
    def _():
        st_scr[...] = jnp.zeros_like(st_scr)

    lb = lb_ref[...]
    gnorm = gn_ref[...]

    def chunk(c, carry):
        sl = pl.ds(pl.multiple_of(c * CHUNK, CHUNK), CHUNK)
        o, st = _hgrn_chunk(zc_ref[0, sl, 0:C_WIDTH], zc_ref[0, sl, C_WIDTH:2 * C_WIDTH],
                            zc_ref[0, sl, 2 * C_WIDTH:3 * C_WIDTH], zc_ref[0, sl, 3 * C_WIDTH:4 * C_WIDTH],
                            st_scr[...], lb, gnorm, kbv_scr)
        st_scr[...] = st
        yc_ref[0, sl, :] = o
        return carry

    lax.fori_loop(0, rows // CHUNK, chunk, 0)

    @pl.when(t == pl.num_programs(1) - 1)
    def _():
        for h in range(C_HEADS):
            st_ref[0, h] = st_scr[h * 64:(h + 1) * 64, h * 64:(h + 1) * 64]


def _hgrn_prompt(zc, lb, gnorm, rows=512):
    bsz, seq, _ = zc.shape
    yc, st = pl.pallas_call(
        functools.partial(_hgrn_prompt_body, rows=rows),
        grid=(bsz, seq // rows),
        in_specs=[pl.BlockSpec((1, rows, 4 * C_WIDTH), lambda b, t: (b, t, 0)),
                  _const_spec((1, C_WIDTH)), _const_spec((1, C_WIDTH))],
        out_specs=[pl.BlockSpec((1, rows, C_WIDTH), lambda b, t: (b, t, 0)),
                   pl.BlockSpec((1, C_HEADS, 64, 64), lambda b, t: (b, 0, 0, 0))],
        out_shape=[jax.ShapeDtypeStruct((bsz, seq, C_WIDTH), F32),
                   jax.ShapeDtypeStruct((bsz, C_HEADS, 64, 64), F32)],
        scratch_shapes=[pltpu.VMEM((C_WIDTH, C_WIDTH), F32), pltpu.VMEM((3, SUB + CHUNK, C_WIDTH), F32)],
        compiler_params=_params(2),
        name="hgrn_prompt",
    )(zc, lb.reshape(1, C_WIDTH), gnorm.reshape(1, C_WIDTH))
    return yc, jnp.swapaxes(st, -1, -2)


def _gated_merge(x, ya, yb, yc, zg, bg, wpa, wpb, wpc, wo):
    g = jax.nn.sigmoid(zg + bg)
    mix = (g[:, 0:D_MODEL] * _dot(ya, wpa) + g[:, D_MODEL:2 * D_MODEL] * _dot(yb, wpb)
           + g[:, 2 * D_MODEL:3 * D_MODEL] * _dot(yc, wpc))
    return x + _dot(mix, wo)


def _merge_prompt_body(x_ref, ya_ref, zb_ref, yc_ref, zg_ref, bg_ref, cw_ref, wpa_ref, wpb_ref, wpc_ref,
                       wo_ref, x1_ref, nc_ref, u_scr, *, rows):
    @pl.when(pl.program_id(1) == 0)
    def _():
        u_scr[0:8, :] = jnp.zeros((8, CONV_WIDTH), F32)

    zb = zb_ref[0]
    u = zb[:, 2 * CONV_WIDTH:3 * CONV_WIDTH] * zb[:, 0:CONV_WIDTH]
    u_scr[8:8 + rows, :] = u
    conv = (cw_ref[0:1, :] * u_scr[pl.ds(6, rows), :] + cw_ref[1:2, :] * u_scr[pl.ds(7, rows), :]
            + cw_ref[2:3, :] * u)
    yb = zb[:, CONV_WIDTH:2 * CONV_WIDTH] * conv
    u_scr[0:8, :] = u_scr[rows:rows + 8, :]
    nc_ref[0] = u[rows - (CONV_K - 1):rows, :]
    x1_ref[0] = _gated_merge(x_ref[0], ya_ref[0], yb, yc_ref[0], zg_ref[0], bg_ref[...],
                             wpa_ref[...], wpb_ref[...], wpc_ref[...], wo_ref[...])


def _merge_prompt(x, ya, zb, yc, zg, b_gate, conv_w, wpa, wpb, wpc, wo, rows=256):
    bsz, seq, _ = x.shape
    blk = lambda w: pl.BlockSpec((1, rows, w), lambda b, t: (b, t, 0))
    return pl.pallas_call(
        functools.partial(_merge_prompt_body, rows=rows),
        grid=(bsz, seq // rows),
        in_specs=[blk(D_MODEL), blk(GROUP_W), blk(3 * CONV_WIDTH), blk(C_WIDTH), blk(N_BRANCH * D_MODEL),
                  _const_spec((1, N_BRANCH * D_MODEL)), _const_spec((CONV_K, CONV_WIDTH)),
                  _const_spec((GROUP_W, D_MODEL)), _const_spec((CONV_WIDTH, D_MODEL)),
                  _const_spec((C_WIDTH, D_MODEL)), _const_spec((D_MODEL, D_MODEL))],
        out_specs=[blk(D_MODEL), pl.BlockSpec((1, CONV_K - 1, CONV_WIDTH), lambda b, t: (b, 0, 0))],
        out_shape=[jax.ShapeDtypeStruct((bsz, seq, D_MODEL), F32),
                   jax.ShapeDtypeStruct((bsz, CONV_K - 1, CONV_WIDTH), F32)],
        scratch_shapes=[pltpu.VMEM((rows + 8, CONV_WIDTH), F32)],
        compiler_params=_params(2),
        name="merge_prompt",
    )(x, ya, zb, yc, zg, b_gate.reshape(1, -1), conv_w, wpa, wpb, wpc, wo)


def _merge_sample_body(x_ref, ya_ref, zb_ref, yc_ref, zg_ref, p0_ref, p1_ref, bg_ref, cw_ref, wpa_ref,
                       wpb_ref, wpc_ref, wo_ref, x1_ref, u_ref):
    zb = zb_ref[...]
    u = zb[:, 2 * CONV_WIDTH:3 * CONV_WIDTH] * zb[:, 0:CONV_WIDTH]
    conv = cw_ref[0:1, :] * p0_ref[...] + cw_ref[1:2, :] * p1_ref[...] + cw_ref[2:3, :] * u
    yb = zb[:, CONV_WIDTH:2 * CONV_WIDTH] * conv
    u_ref[...] = u
    x1_ref[...] = _gated_merge(x_ref[...], ya_ref[...], yb, yc_ref[...], zg_ref[...], bg_ref[...],
                               wpa_ref[...], wpb_ref[...], wpc_ref[...], wo_ref[...])


def _merge_sample(x, ya, zb, yc, zg, prev0, prev1, b_gate, conv_w, wpa, wpb, wpc, wo):
    m = x.shape[0]
    args = (x, ya, zb, yc, zg, prev0, prev1, b_gate.reshape(1, -1), conv_w, wpa, wpb, wpc, wo)
    return pl.pallas_call(
        _merge_sample_body,
        grid=(1,),
        in_specs=[_const_spec(a.shape) for a in args],
        out_specs=[_const_spec((m, D_MODEL)), _const_spec((m, CONV_WIDTH))],
        out_shape=[jax.ShapeDtypeStruct((m, D_MODEL), F32), jax.ShapeDtypeStruct((m, CONV_WIDTH), F32)],
        compiler_params=_params(1),
        name="merge_sample",
    )(*args)


def _mlp_body(x_ref, g2_ref, wup_ref, wdn_ref, gf_ref, out_ref, *, final):
    x = x_ref[...]
    h = _rmsnorm(x, g2_ref[...]).astype(BF16)
    acc = x
    for c in range(D_FF // D_MODEL):
        cols = slice(c * D_MODEL, (c + 1) * D_MODEL)
        a = jnp.maximum(jnp.dot(h, wup_ref[:, cols], preferred_element_type=F32), 0.0)
        acc = acc + jnp.dot((a * a).astype(BF16), wdn_ref[cols, :], preferred_element_type=F32)
    out_ref[...] = _rmsnorm(acc, gf_ref[...]) if final else acc


def _mlp(x2d, norm2, wup, wdn, norm_f, final, tm):
    m = x2d.shape[0]
    row = pl.BlockSpec((tm, D_MODEL), lambda i: (i, 0))
    return pl.pallas_call(
        functools.partial(_mlp_body, final=final),
        grid=(m // tm,),
        in_specs=[row, _const_spec((1, D_MODEL)), _const_spec((D_MODEL, D_FF)), _const_spec((D_FF, D_MODEL)),
                  _const_spec((1, D_MODEL))],
        out_specs=row,
        out_shape=jax.ShapeDtypeStruct((m, D_MODEL), F32),
        compiler_params=_params(1),
        name="mlp",
    )(x2d, norm2.reshape(1, D_MODEL), wup, wdn, norm_f.reshape(1, D_MODEL))


def _sample_mix_body(q_ref, k_ref, v_ref, ck0_ref, cv0_ref, ck1_ref, cv1_ref, ck2_ref, cv2_ref, zrow_ref, zcol_ref,
                     s_ref, lbc_ref, gn_ref, ya_ref, yc_ref, s_out_ref, n0_ref, n1_ref, n2_ref):
    lane = lax.broadcasted_iota(jnp.int32, (8, GROUP_W), 1)
    rowi = lax.broadcasted_iota(jnp.int32, (8, GROUP_W), 0)
    head_of_row = (lane // HEAD_DIM) == rowi
    head0 = lax.broadcasted_iota(jnp.int32, (1, GROUP_W), 1) < HEAD_DIM

    outs, lses = [], []
    cache_refs = ((ck0_ref, cv0_ref), (ck1_ref, cv1_ref), (ck2_ref, cv2_ref))
    for g, ((ck_ref, cv_ref), n_ref, (win, dil)) in enumerate(zip(cache_refs, (n0_ref, n1_ref, n2_ref),
                                                                  DSWA_GROUPS)):
        q, k_new, v_new = q_ref[g], k_ref[g], v_ref[g]
        r = pl.ds(0, BAND, stride=dil) if dil > 1 else pl.ds(0, BAND)
        k_c, v_c = ck_ref[r, :], cv_ref[r, :]
        q2 = jnp.where(head_of_row, jnp.broadcast_to(q, (8, GROUP_W)), 0.0)
        scale = HEAD_DIM ** -0.5
        s = _dot_nt(q2, k_c) * scale
        s_self = jnp.sum(q2 * k_new, axis=-1, keepdims=True) * scale
        m = jnp.maximum(jnp.max(s, axis=-1, keepdims=True), s_self)
        p, p_self = jnp.exp(s - m), jnp.exp(s_self - m)
        l = jnp.sum(p, axis=-1, keepdims=True) + p_self
        o = (_dot(p, v_c) + p_self * v_new) / l
        lse = m + jnp.log(l)
        outs.append(jnp.where(head0, o[0:1], o[1:2]))
        lses.append(jnp.where(head0, jnp.broadcast_to(lse[0:1], (1, GROUP_W)),
                              jnp.broadcast_to(lse[1:2], (1, GROUP_W))))
        n_ref[0:win - 1, 0:GROUP_W] = ck_ref[1:win, :]
        n_ref[0:win - 1, GROUP_W:2 * GROUP_W] = cv_ref[1:win, :]
        n_ref[win - 1:win, 0:GROUP_W] = k_new
        n_ref[win - 1:win, GROUP_W:2 * GROUP_W] = v_new
    m = jnp.maximum(jnp.maximum(lses[0], lses[1]), lses[2])
    e = [jnp.exp(x - m) for x in lses]
    ya_ref[0] = (e[0] * outs[0] + e[1] * outs[1] + e[2] * outs[2]) / (e[0] + e[1] + e[2])

    zrow = zrow_ref[0]
    q_col = zcol_ref[0, 0:C_WIDTH, :]
    log_f, k_col = _log_forget_and_kgate(zcol_ref[0, C_WIDTH:2 * C_WIDTH, :], lbc_ref[...])
    v_row = zrow[:, 2 * C_WIDTH:3 * C_WIDTH]
    og_row = zrow[:, 3 * C_WIDTH:4 * C_WIDTH]
    v_sel = jnp.concatenate([jnp.broadcast_to(v_row[:, h * 64:(h + 1) * 64], (64, 64)) for h in range(C_HEADS)],
                            axis=0)
    s_new = jnp.exp(log_f) * s_ref[...] + k_col * v_sel
    s_out_ref[...] = s_new
    qs = q_col * s_new
    o4 = jnp.concatenate([jnp.sum(qs[h * 64:(h + 1) * 64, :], axis=0, keepdims=True) for h in range(C_HEADS)],
                         axis=0)
    og4 = jnp.concatenate([og_row[:, h * 64:(h + 1) * 64] for h in range(C_HEADS)], axis=0)
    o4 = o4 * lax.rsqrt(jnp.mean(o4 * o4, axis=-1, keepdims=True) + EPS) * gn_ref[...] * _silu(og4)
    yc_ref[0] = o4


def _sample_mix(q, k, v, caches, zc, state, lb, gnorm, layer, new_caches):
    n = q.shape[1]
    depth = caches[0].shape[0]
    row3 = lambda w: pl.BlockSpec((1, 1, w), lambda b: (b, 0, 0))
    grp = pl.BlockSpec((3, None, 1, GROUP_W), lambda b: (0, b, 0, 0))
    cache_specs = [pl.BlockSpec((None, None, win, 2 * GROUP_W), lambda b: (layer, b, 0, 0))
                   for win, _ in DSWA_GROUPS]
    half_specs = [pl.BlockSpec((None, None, win, GROUP_W), functools.partial(lambda b, h: (layer, b, 0, h), h=h))
                  for win, _ in DSWA_GROUPS for h in range(2)]
    in_specs = ([grp] * 3 + half_specs
                + [row3(4 * C_WIDTH), pl.BlockSpec((1, 4 * C_WIDTH, 1), lambda b: (b, 0, 0)),
                   pl.BlockSpec((None, None, C_WIDTH, 64), lambda b: (layer, b, 0, 0)),
                   _const_spec((C_WIDTH, 1)), _const_spec((C_HEADS, 64))])
    args = [q.reshape(3, n, 1, GROUP_W), k.reshape(3, n, 1, GROUP_W), v.reshape(3, n, 1, GROUP_W),
            *[c for c in caches for _ in range(2)],
            zc.reshape(n, 1, 4 * C_WIDTH), zc.reshape(n, 4 * C_WIDTH, 1), state,
            lb.reshape(C_WIDTH, 1), gnorm.reshape(C_HEADS, 64)]
    aliases = {}
    n_extra = 0
    if new_caches is not None:
        n_extra = 3
        aliases = {len(args) + i: 3 + i for i in range(3)}
        in_specs = in_specs + [pl.BlockSpec(memory_space=pl.ANY)] * 3
        args = args + list(new_caches)

    def body(*refs):
        n_in = len(args) - n_extra
        _sample_mix_body(*refs[:n_in], *refs[n_in + n_extra:])

    return pl.pallas_call(
        body,
        grid=(n,),
        in_specs=in_specs,
        out_specs=[pl.BlockSpec((1, 1, GROUP_W), lambda b: (b, 0, 0)),
                   pl.BlockSpec((1, C_HEADS, 64), lambda b: (b, 0, 0)),
                   pl.BlockSpec((None, C_WIDTH, 64), lambda b: (b, 0, 0))] + cache_specs,
        out_shape=[jax.ShapeDtypeStruct((n, 1, GROUP_W), F32), jax.ShapeDtypeStruct((n, C_HEADS, 64), F32),
                   jax.ShapeDtypeStruct((n, C_WIDTH, 64), F32)]
        + [jax.ShapeDtypeStruct((depth, n, win, 2 * GROUP_W), F32) for win, _ in DSWA_GROUPS],
        input_output_aliases=aliases,
        compiler_params=_params(1),
        name="sample_mix",
    )(*args)


def kernel(x_prompt, x_sample, cache_kv_w128, cache_kv_w512, cache_kv_w2048, state_conv, state_hgrn, w_in, b_gate,
           norm1, conv_w, hgrn_lb, hgrn_norm, w_pa, w_pb, w_pc, w_o, norm2, w_up, w_down, norm_f):
    bp, tp, d = x_prompt.shape
    bs, ts, _ = x_sample.shape
    depth = w_in.shape[0]
    assert d == D_MODEL and ts == 1 and depth == 2
    assert PAST_LEN >= max(win for win, _ in DSWA_GROUPS)

    sm = jax.nn.softmax(hgrn_lb.astype(F32), axis=0)
    lower = jnp.cumsum(sm, axis=0) - sm[0:1]

    cos_p, sin_p = _rope_tables(jnp.arange(tp, dtype=jnp.int32))
    cos_s, sin_s = _rope_tables(jnp.full((bs,), PAST_LEN, dtype=jnp.int32))

    caches = [c.reshape(depth, bs, win, 2 * GROUP_W)
              for c, (win, _) in zip((cache_kv_w128, cache_kv_w512, cache_kv_w2048), DSWA_GROUPS)]
    state = state_hgrn.reshape(depth, bs, C_WIDTH, 64)

    xp = x_prompt.reshape(bp * tp, d)
    xs = x_sample.reshape(bs, d)
    kv_p, kv_s = None, None
    conv_p, hgrn_p, conv_s, hgrn_s = [], [], [], []
    for l in range(depth):
        w_in_bf = w_in[l].astype(BF16)
        wpa, wpb, wpc, wo = (w[l].astype(BF16) for w in (w_pa, w_pb, w_pc, w_o))
        wup, wdn = w_up[l].astype(BF16), w_down[l].astype(BF16)
        final = l == depth - 1

        q, k, v, zb, zc, zg = _inproj(xp, norm1[l], w_in_bf, cos_p, sin_p, tm=256)
        ya, *kv_p = _attn_prompt(q.reshape(3, bp, tp, -1), k.reshape(3, bp, tp, -1), v.reshape(3, bp, tp, -1),
                                 l, kv_p)
        yc, st = _hgrn_prompt(zc.reshape(bp, tp, -1), lower[l], hgrn_norm[l])
        x1, nc = _merge_prompt(xp.reshape(bp, tp, d), ya, zb.reshape(bp, tp, -1), yc, zg.reshape(bp, tp, -1),
                               b_gate[l], conv_w[l], wpa, wpb, wpc, wo)
        xp = _mlp(x1.reshape(bp * tp, d), norm2[l], wup, wdn, norm_f, final, tm=512)
        conv_p.append(nc)
        hgrn_p.append(st)

        q, k, v, zb, zc, zg = _inproj(xs, norm1[l], w_in_bf, cos_s, sin_s, tm=bs)
        ya, yc, s_new, *kv_s = _sample_mix(q, k, v, caches, zc, state, lower[l], hgrn_norm[l], l, kv_s)
        x1, u = _merge_sample(xs, ya.reshape(bs, GROUP_W), zb, yc.reshape(bs, C_WIDTH), zg,
                              state_conv[l, :, 0], state_conv[l, :, 1], b_gate[l], conv_w[l], wpa, wpb, wpc, wo)
        xs = _mlp(x1, norm2[l], wup, wdn, norm_f, final, tm=bs)
        conv_s.append(jnp.stack([state_conv[l, :, 1], u], axis=1))
        hgrn_s.append(s_new.reshape(bs, C_HEADS, 64, 64))

    def kv6(c, n):
        return c.reshape(depth, n, c.shape[2], 2, 2, HEAD_DIM)

    return (xp.reshape(bp, tp, d), xs.reshape(bs, ts, d),
            kv6(kv_p[0], bp), kv6(kv_p[1], bp), kv6(kv_p[2], bp), jnp.stack(conv_p), jnp.stack(hgrn_p),
            kv6(kv_s[0], bs), kv6(kv_s[1], bs), kv6(kv_s[2], bs), jnp.stack(conv_s), jnp.stack(hgrn_s))
```

```python
import functools

import jax
import jax.numpy as jnp
from jax import lax
from jax.experimental import pallas as pl
from jax.experimental.pallas import tpu as pltpu

F32 = jnp.float32
BF16 = jnp.bfloat16

D_MODEL = 1024
HEAD_DIM = 64
DSWA_GROUPS = ((128, 1), (512, 4), (2048, 16))
BAND = 128
GROUP_W = 2 * HEAD_DIM
A_WIDTH = 3 * GROUP_W
CONV_WIDTH = 384
CONV_K = 3
C_HEADS = 4
C_WIDTH = C_HEADS * 64
N_BRANCH = 3
D_FF = 4 * D_MODEL
ROPE_THETA = 10000.0
EPS = 1e-6
NEG_BIG = -1e30
LB_FLOOR = 1e-30
PAST_LEN = 8192
IN_COLS = 3 * A_WIDTH + 3 * CONV_WIDTH + 4 * C_WIDTH + N_BRANCH * D_MODEL
COL_B = 3 * A_WIDTH
COL_C = COL_B + 3 * CONV_WIDTH
COL_G = COL_C + 4 * C_WIDTH
CHUNK = 64
SUB = 8
VMEM_LIMIT = 56 * 1024 * 1024


def _params(n_grid):
    return pltpu.CompilerParams(dimension_semantics=("arbitrary",) * n_grid,
                                vmem_limit_bytes=VMEM_LIMIT)


def _const_spec(shape):
    nd = len(shape)
    return pl.BlockSpec(shape, lambda *_: (0,) * nd, pipeline_mode=pl.Buffered(1))


def _dot(a, b):
    return jnp.dot(a.astype(BF16), b.astype(BF16), preferred_element_type=F32)


def _dot_nt(a, b):
    return lax.dot_general(a.astype(BF16), b.astype(BF16), (((1,), (1,)), ((), ())),
                           preferred_element_type=F32)


def _dot_tn(a, b):
    return lax.dot_general(a.astype(BF16), b.astype(BF16), (((0,), (0,)), ((), ())),
                           preferred_element_type=F32)


def _rmsnorm(x, g):
    return x * lax.rsqrt(jnp.mean(x * x, axis=-1, keepdims=True) + EPS) * g


def _inproj_body(x_ref, g_ref, wa_ref, wc_ref, cos_ref, sin_ref, lb_ref, gn_ref, q_ref, k_ref, v_ref, zc_ref,
                 *, chunked):
    h = _rmsnorm(x_ref[...], g_ref[...]).astype(BF16)
    cos = cos_ref[...]
    sin = sin_ref[...]
    lane = lax.broadcasted_iota(jnp.int32, cos.shape, 1)
    first_half = (lane % HEAD_DIM) < HEAD_DIM // 2

    def proj(c0, c1):
        return jnp.dot(h, wa_ref[:, c0:c1], preferred_element_type=F32)

    def rope_store(out_ref, c0):
        for g in range(3):
            z = proj(c0 + g * GROUP_W, c0 + (g + 1) * GROUP_W)
            partner = jnp.where(first_half, pltpu.roll(z, GROUP_W - HEAD_DIM // 2, 1),
                                pltpu.roll(z, HEAD_DIM // 2, 1))
            out_ref[g] = z * cos + partner * sin

    rope_store(q_ref, 0)
    rope_store(k_ref, A_WIDTH)
    for g in range(3):
        v_ref[g] = proj(2 * A_WIDTH + g * GROUP_W, 2 * A_WIDTH + (g + 1) * GROUP_W)
    parts = _hgrn_inputs(jnp.dot(h, wc_ref[...], preferred_element_type=F32), lb_ref[...], gn_ref[...], chunked)
    for i, part in enumerate(parts):
        zc_ref[:, i * C_WIDTH:(i + 1) * C_WIDTH] = part


def _inproj(x2d, norm, w_a, w_c, cos, sin, lb, gnorm, tm, chunked):
    m = x2d.shape[0]
    assert not chunked or tm % CHUNK == 0
    n_pos = cos.shape[0] // tm
    row = lambda w: pl.BlockSpec((tm, w), lambda i: (i, 0))
    tab = pl.BlockSpec((tm, GROUP_W), lambda i: (i % n_pos, 0))
    grp = pl.BlockSpec((3, tm, GROUP_W), lambda i: (0, i, 0))
    return pl.pallas_call(
        functools.partial(_inproj_body, chunked=chunked),
        grid=(m // tm,),
        in_specs=[row(D_MODEL), _const_spec((1, D_MODEL)), _const_spec(w_a.shape), _const_spec(w_c.shape),
                  tab, tab, _const_spec((1, C_WIDTH)), _const_spec((1, C_WIDTH))],
        out_specs=[grp] * 3 + [row(5 * C_WIDTH)],
        out_shape=[jax.ShapeDtypeStruct((3, m, GROUP_W), F32)] * 3
        + [jax.ShapeDtypeStruct((m, 5 * C_WIDTH), F32)],
        compiler_params=_params(1),
        name="inproj",
    )(x2d, norm.reshape(1, D_MODEL), w_a, w_c, cos, sin, lb.reshape(1, C_WIDTH), gnorm.reshape(1, C_WIDTH))


def _rope_tables(pos):
    half = HEAD_DIM // 2
    inv = ROPE_THETA ** (-jnp.arange(half, dtype=F32) / half)
    ang = pos.astype(F32)[:, None] * inv[None, :]
    cos, sin = jnp.cos(ang), jnp.sin(ang)
    return jnp.tile(cos, (1, 4)), jnp.tile(jnp.concatenate([-sin, sin], axis=1), (1, 2))


def _attend(tiles):
    head0 = lax.broadcasted_iota(jnp.int32, (BAND, GROUP_W), 1) < HEAD_DIM
    scale = HEAD_DIM ** -0.5
    q2 = [jnp.concatenate([jnp.where(head0, q, 0.0), jnp.where(head0, 0.0, q)], axis=0).astype(BF16)
          for q, _ in tiles]
    s = [[jnp.where(valid, _dot_nt(qq, k) * scale, NEG_BIG) for k, _, valid in blocks]
         for qq, (_, blocks) in zip(q2, tiles)]
    m = [jnp.max(functools.reduce(jnp.maximum, x), axis=-1, keepdims=True) for x in s]
    p = [[jnp.exp(x - mm).astype(BF16) for x in ss] for ss, mm in zip(s, m)]
    ext = [functools.reduce(jnp.add, [jnp.dot(pp, v1, preferred_element_type=F32)
                                      for pp, (_, v1, _) in zip(pt, blocks)])
           for pt, (_, blocks) in zip(p, tiles)]
    results = []
    for e, mm in zip(ext, m):
        den = e[:, GROUP_W:]
        o2 = e[:, 0:GROUP_W] / den
        lse2 = mm + jnp.log(den)
        results.append((jnp.where(head0, o2[0:BAND], o2[BAND:]), jnp.where(head0, lse2[0:BAND], lse2[BAND:])))
    return results


def _attn_prompt_body(q_ref, k_ref, v_ref, ya_ref, c0_ref, c1_ref, c2_ref, o_scr, lse_scr, *, seq):
    qi = lax.broadcasted_iota(jnp.int32, (2 * BAND, BAND), 0) % BAND
    kj = lax.broadcasted_iota(jnp.int32, (2 * BAND, BAND), 1)
    cur_valid = kj <= qi
    prev_valid = kj >= qi
    ones = jnp.ones((BAND, GROUP_W), BF16)

    def rows(start, dil):
        return pl.ds(start, BAND, stride=dil) if dil > 1 else pl.ds(start, BAND)

    def key_block(g, r):
        return k_ref[g, r, :].astype(BF16), jnp.concatenate([v_ref[g, r, :].astype(BF16), ones], axis=1)

    def run(g, dil, chains):
        tiles, slices = [], []
        for starts, first_prev in chains:
            chain = [rows(s, dil) for s in starts]
            keys = [key_block(g, r) for r in chain]
            for j, r in enumerate(chain):
                blocks = [(*keys[j], cur_valid)]
                if j > 0:
                    blocks.append((*keys[j - 1], prev_valid))
                elif first_prev is not None:
                    blocks.append((*key_block(g, rows(first_prev[0], dil)), prev_valid & first_prev[1]))
                tiles.append((q_ref[g, r, :], blocks))
            slices += chain
        for r, (o, lse) in zip(slices, _attend(tiles)):
            o_scr[g, r, :] = o
            lse_scr[g, r, :] = lse

    for g, (_, dil) in enumerate(DSWA_GROUPS):
        n_blocks = seq // dil // BAND
        if dil == 1:
            def dense(i, carry):
                s0 = pl.multiple_of(4 * i * BAND, BAND)
                prev = pl.multiple_of(jnp.maximum(4 * i - 1, 0) * BAND, BAND)
                run(g, dil, [([s0 + u * BAND for u in range(4)], (prev, i > 0))])
                return carry

            lax.fori_loop(0, n_blocks // 4, dense, 0)
        elif n_blocks > 1:
            def residue(r, carry, g=g, dil=dil, n_blocks=n_blocks):
                run(g, dil, [([r + i * BAND * dil for i in range(n_blocks)], None)])
                return carry

            lax.fori_loop(0, dil, residue, 0)
        else:
            def residues(j, carry, g=g, dil=dil):
                run(g, dil, [([4 * j + u], None) for u in range(4)])
                return carry

            lax.fori_loop(0, dil // 4, residues, 0)

    step = 256

    def merge(i, carry):
        sl = pl.ds(pl.multiple_of(i * step, step), step)
        lse = [lse_scr[g, sl, :] for g in range(3)]
        m = jnp.maximum(jnp.maximum(lse[0], lse[1]), lse[2])
        e = [jnp.exp(x - m) for x in lse]
        acc = e[0] * o_scr[0, sl, :] + e[1] * o_scr[1, sl, :] + e[2] * o_scr[2, sl, :]
        ya_ref[0, sl, :] = acc / (e[0] + e[1] + e[2])
        return carry

    lax.fori_loop(0, seq // step, merge, 0)

    for g, (c_ref, (win, _)) in enumerate(zip((c0_ref, c1_ref, c2_ref), DSWA_GROUPS)):
        c_ref[:, 0:GROUP_W] = k_ref[g, seq - win:seq, :]
        c_ref[:, GROUP_W:2 * GROUP_W] = v_ref[g, seq - win:seq, :]


def _attn_prompt(q, k, v, layer, caches):
    _, bsz, seq, _ = q.shape
    depth = 2
    assert seq % (BAND * 16) == 0 and seq >= 2048
    qkv_spec = pl.BlockSpec((3, None, seq, GROUP_W), lambda b: (0, b, 0, 0))
    cache_specs = [pl.BlockSpec((None, None, win, 2 * GROUP_W), lambda b: (layer, b, 0, 0))
                   for win, _ in DSWA_GROUPS]
    cache_shapes = [jax.ShapeDtypeStruct((depth, bsz, win, 2 * GROUP_W), F32) for win, _ in DSWA_GROUPS]
    in_specs = [qkv_spec] * 3
    args = [q, k, v]
    aliases = {}
    if caches is not None:
        in_specs = in_specs + [pl.BlockSpec(memory_space=pl.ANY)] * 3
        args = args + list(caches)
        aliases = {3: 1, 4: 2, 5: 3}

    def body(*refs):
        ins, rest = refs[:3], refs[3 + (3 if caches is not None else 0):]
        _attn_prompt_body(*ins, *rest, seq=seq)

    return pl.pallas_call(
        body,
        grid=(bsz,),
        in_specs=in_specs,
        out_specs=[pl.BlockSpec((1, seq, GROUP_W), lambda b: (b, 0, 0))] + cache_specs,
        out_shape=[jax.ShapeDtypeStruct((bsz, seq, GROUP_W), F32)] + cache_shapes,
        scratch_shapes=[pltpu.VMEM((3, seq, GROUP_W), F32), pltpu.VMEM((3, seq, GROUP_W), F32)],
        input_output_aliases=aliases,
        compiler_params=_params(1),
        name="attn_prompt",
    )(*args)


def _log_forget_and_kgate(zf, lb):
    log_sig = -(jnp.maximum(-zf, 0.0) + jnp.log1p(jnp.exp(-jnp.abs(zf))))
    a = jnp.log(jnp.maximum(lb, LB_FLOOR))
    c = jnp.log1p(-lb) + log_sig
    log_f = jnp.maximum(a, c) + jnp.log1p(jnp.exp(-jnp.abs(a - c)))
    kgate = (1.0 - lb) * jax.nn.sigmoid(-zf)
    return log_f, kgate


def _silu(x):
    return x * jax.nn.sigmoid(x)


def _split3(x):
    hi = x.astype(BF16)
    r1 = x - hi.astype(F32)
    mid = r1.astype(BF16)
    lo = (r1 - mid.astype(F32)).astype(BF16)
    return hi, mid, lo


def _hgrn_inputs(zc, lb, gnorm, chunked):
    q, zf, v, og = (zc[:, i * C_WIDTH:(i + 1) * C_WIDTH] for i in range(4))
    log_f, k = _log_forget_and_kgate(zf, lb)
    if chunked:
        n = zc.shape[0]
        r = lax.broadcasted_iota(jnp.int32, (n, n), 0)
        c = lax.broadcasted_iota(jnp.int32, (n, n), 1)
        tri = jnp.where((c <= r) & ((r // CHUNK) == (c // CHUNK)), 1.0, 0.0).astype(BF16)
        b = functools.reduce(jnp.add, [jnp.dot(tri, p, preferred_element_type=F32) for p in _split3(log_f)])
    else:
        b = log_f
    return q, b, k, v, gnorm * _silu(og)


def _hgrn_chunk(q, b, k, v, gate, st, kbv_scr):
    n = CHUNK
    row = lax.broadcasted_iota(jnp.int32, (n, C_WIDTH), 0)
    lane_head = lax.broadcasted_iota(jnp.int32, (n, C_WIDTH), 1) // 64

    hr = lax.broadcasted_iota(jnp.int32, (C_WIDTH, C_WIDTH), 0) // 64
    hc = lax.broadcasted_iota(jnp.int32, (C_WIDTH, C_WIDTH), 1) // 64
    same_head = hr == hc
    ones_bd = jnp.where(same_head, 1.0, 0.0).astype(BF16)

    kbv_scr[0, SUB:SUB + n, :] = k
    kbv_scr[1, SUB:SUB + n, :] = b
    kbv_scr[2, SUB:SUB + n, :] = v
    o = jnp.dot((q * k).astype(BF16), ones_bd, preferred_element_type=F32) * v
    for d in range(1, SUB):
        sl = pl.ds(SUB - d, n)
        k_d, b_d, v_d = kbv_scr[0, sl, :], kbv_scr[1, sl, :], kbv_scr[2, sl, :]
        in_tile = (row % SUB) >= d
        y = jnp.where(in_tile, q * k_d * jnp.exp(jnp.minimum(b - b_d, 0.0)), 0.0)
        o = o + jnp.dot(y.astype(BF16), ones_bd, preferred_element_type=F32) * v_d

    srow = lax.broadcasted_iota(jnp.int32, (C_HEADS * n, n), 0) % n
    scol = lax.broadcasted_iota(jnp.int32, (C_HEADS * n, n), 1)
    att = jnp.zeros((C_HEADS * n, n), F32)
    m = n // 2
    while m >= SUB:
        ref_q, ref_k = [], []
        for j in range(n // m):
            own = b[j * m:(j + 1) * m, :]
            if j % 2 == 1:
                ref_q.append(jnp.broadcast_to(b[j * m - 1:j * m, :], (m, C_WIDTH)))
                ref_k.append(own)
            else:
                ref_q.append(own)
                ref_k.append(jnp.broadcast_to(b[(j + 1) * m - 1:(j + 1) * m, :], (m, C_WIDTH)))
        upper = ((row // m) % 2) == 1
        qt = jnp.where(upper, q * jnp.exp(jnp.minimum(b - jnp.concatenate(ref_q, axis=0), 0.0)), 0.0)
        kt = jnp.where(upper, 0.0, k * jnp.exp(jnp.minimum(jnp.concatenate(ref_k, axis=0) - b, 0.0)))
        q_stack = jnp.concatenate([jnp.where(lane_head == h, qt, 0.0) for h in range(C_HEADS)], axis=0)
        a = _dot_nt(q_stack, kt)
        att = att + jnp.where((srow // (2 * m)) == (scol // (2 * m)), a, 0.0)
        m //= 2
    r = _dot(att, v)
    for h in range(C_HEADS):
        o = o + jnp.where(lane_head == h, r[h * n:(h + 1) * n, :], 0.0)

    o = o + _dot_nt(q * jnp.exp(b), st)
    b_last = b[n - 1:n, :]
    k_hat = k * jnp.exp(b_last - b)
    st = st * jnp.exp(b_last) + jnp.where(same_head, _dot_tn(v, k_hat), 0.0)

    ms = jnp.zeros_like(o)
    for h in range(C_HEADS):
        sel = lane_head == h
        ms = ms + jnp.where(sel, jnp.sum(jnp.where(sel, o * o, 0.0), axis=-1, keepdims=True), 0.0)
    o = o * lax.rsqrt(ms * (1.0 / 64) + EPS) * gate
    return o, st


def _hgrn_prompt_body(zc_ref, yc_ref, st_ref, st_scr, kbv_scr, *, rows):
    t = pl.program_id(1)
    kbv_scr[:, 0:SUB, :] = jnp.zeros((3, SUB, C_WIDTH), F32)

    @pl.when(t == 0)
    def _():
        st_scr[...] = jnp.zeros_like(st_scr)

    def chunk(c, carry):
        sl = pl.ds(pl.multiple_of(c * CHUNK, CHUNK), CHUNK)
        o, st = _hgrn_chunk(*(zc_ref[0, sl, i * C_WIDTH:(i + 1) * C_WIDTH] for i in range(5)),
                            st_scr[...], kbv_scr)
        st_scr[...] = st
        yc_ref[0, sl, :] = o
        return carry

    lax.fori_loop(0, rows // CHUNK, chunk, 0)

    @pl.when(t == pl.num_programs(1) - 1)
    def _():
        for h in range(C_HEADS):
            st_ref[0, h] = st_scr[h * 64:(h + 1) * 64, h * 64:(h + 1) * 64]


def _hgrn_prompt(zc, rows=512):
    bsz, seq, _ = zc.shape
    yc, st = pl.pallas_call(
        functools.partial(_hgrn_prompt_body, rows=rows),
        grid=(bsz, seq // rows),
        in_specs=[pl.BlockSpec((1, rows, 5 * C_WIDTH), lambda b, t: (b, t, 0))],
        out_specs=[pl.BlockSpec((1, rows, C_WIDTH), lambda b, t: (b, t, 0)),
                   pl.BlockSpec((1, C_HEADS, 64, 64), lambda b, t: (b, 0, 0, 0))],
        out_shape=[jax.ShapeDtypeStruct((bsz, seq, C_WIDTH), F32),
                   jax.ShapeDtypeStruct((bsz, C_HEADS, 64, 64), F32)],
        scratch_shapes=[pltpu.VMEM((C_WIDTH, C_WIDTH), F32), pltpu.VMEM((3, SUB + CHUNK, C_WIDTH), F32)],
        compiler_params=_params(2),
        name="hgrn_prompt",
    )(zc)
    return yc, jnp.swapaxes(st, -1, -2)


def _gated_merge(x, ya, yb, yc, zg, bg, wpa, wpb, wpc, wo):
    g = jax.nn.sigmoid(zg + bg)
    mix = (g[:, 0:D_MODEL] * _dot(ya, wpa) + g[:, D_MODEL:2 * D_MODEL] * _dot(yb, wpb)
           + g[:, 2 * D_MODEL:3 * D_MODEL] * _dot(yc, wpc))
    return x + _dot(mix, wo)


def _project_conv_and_gates(x, g1, wb, wg):
    h = _rmsnorm(x, g1).astype(BF16)
    return jnp.dot(h, wb, preferred_element_type=F32), jnp.dot(h, wg, preferred_element_type=F32)


def _merge_prompt_body(x_ref, ya_ref, yc_ref, g1_ref, wb_ref, wg_ref, bg_ref, cw_ref, wpa_ref, wpb_ref, wpc_ref,
                       wo_ref, x1_ref, nc_ref, u_scr, *, rows):
    @pl.when(pl.program_id(1) == 0)
    def _():
        u_scr[0:8, :] = jnp.zeros((8, CONV_WIDTH), F32)

    zb, zg = _project_conv_and_gates(x_ref[0], g1_ref[...], wb_ref[...], wg_ref[...])
    u = zb[:, 2 * CONV_WIDTH:3 * CONV_WIDTH] * zb[:, 0:CONV_WIDTH]
    u_scr[8:8 + rows, :] = u
    conv = (cw_ref[0:1, :] * u_scr[pl.ds(6, rows), :] + cw_ref[1:2, :] * u_scr[pl.ds(7, rows), :]
            + cw_ref[2:3, :] * u)
    yb = zb[:, CONV_WIDTH:2 * CONV_WIDTH] * conv
    u_scr[0:8, :] = u_scr[rows:rows + 8, :]
    nc_ref[0] = u[rows - (CONV_K - 1):rows, :]
    x1_ref[0] = _gated_merge(x_ref[0], ya_ref[0], yb, yc_ref[0], zg, bg_ref[...],
                             wpa_ref[...], wpb_ref[...], wpc_ref[...], wo_ref[...])


def _merge_prompt(x, ya, yc, norm1, w_b, w_g, b_gate, conv_w, wpa, wpb, wpc, wo, rows=256):
    bsz, seq, _ = x.shape
    blk = lambda w: pl.BlockSpec((1, rows, w), lambda b, t: (b, t, 0))
    return pl.pallas_call(
        functools.partial(_merge_prompt_body, rows=rows),
        grid=(bsz, seq // rows),
        in_specs=[blk(D_MODEL), blk(GROUP_W), blk(C_WIDTH),
                  _const_spec((1, D_MODEL)), _const_spec(w_b.shape), _const_spec(w_g.shape),
                  _const_spec((1, N_BRANCH * D_MODEL)), _const_spec((CONV_K, CONV_WIDTH)),
                  _const_spec((GROUP_W, D_MODEL)), _const_spec((CONV_WIDTH, D_MODEL)),
                  _const_spec((C_WIDTH, D_MODEL)), _const_spec((D_MODEL, D_MODEL))],
        out_specs=[blk(D_MODEL), pl.BlockSpec((1, CONV_K - 1, CONV_WIDTH), lambda b, t: (b, 0, 0))],
        out_shape=[jax.ShapeDtypeStruct((bsz, seq, D_MODEL), F32),
                   jax.ShapeDtypeStruct((bsz, CONV_K - 1, CONV_WIDTH), F32)],
        scratch_shapes=[pltpu.VMEM((rows + 8, CONV_WIDTH), F32)],
        compiler_params=_params(2),
        name="merge_prompt",
    )(x, ya, yc, norm1.reshape(1, D_MODEL), w_b, w_g, b_gate.reshape(1, -1), conv_w, wpa, wpb, wpc, wo)


def _merge_sample_body(x_ref, ya_ref, yc_ref, g1_ref, wb_ref, wg_ref, p0_ref, p1_ref, bg_ref, cw_ref, wpa_ref,
                       wpb_ref, wpc_ref, wo_ref, x1_ref, u_ref):
    zb, zg = _project_conv_and_gates(x_ref[...], g1_ref[...], wb_ref[...], wg_ref[...])
    u = zb[:, 2 * CONV_WIDTH:3 * CONV_WIDTH] * zb[:, 0:CONV_WIDTH]
    conv = cw_ref[0:1, :] * p0_ref[...] + cw_ref[1:2, :] * p1_ref[...] + cw_ref[2:3, :] * u
    yb = zb[:, CONV_WIDTH:2 * CONV_WIDTH] * conv
    u_ref[...] = u
    x1_ref[...] = _gated_merge(x_ref[...], ya_ref[...], yb, yc_ref[...], zg, bg_ref[...],
                               wpa_ref[...], wpb_ref[...], wpc_ref[...], wo_ref[...])


def _merge_sample(x, ya, yc, norm1, w_b, w_g, prev0, prev1, b_gate, conv_w, wpa, wpb, wpc, wo):
    m = x.shape[0]
    args = (x, ya, yc, norm1.reshape(1, D_MODEL), w_b, w_g, prev0, prev1, b_gate.reshape(1, -1), conv_w,
            wpa, wpb, wpc, wo)
    return pl.pallas_call(
        _merge_sample_body,
        grid=(1,),
        in_specs=[_const_spec(a.shape) for a in args],
        out_specs=[_const_spec((m, D_MODEL)), _const_spec((m, CONV_WIDTH))],
        out_shape=[jax.ShapeDtypeStruct((m, D_MODEL), F32), jax.ShapeDtypeStruct((m, CONV_WIDTH), F32)],
        compiler_params=_params(1),
        name="merge_sample",
    )(*args)


def _mlp_body(x_ref, g2_ref, wup_ref, wdn_ref, gf_ref, out_ref, *, final):
    x = x_ref[...]
    h = _rmsnorm(x, g2_ref[...]).astype(BF16)
    acc = x
    for c in range(D_FF // D_MODEL):
        cols = slice(c * D_MODEL, (c + 1) * D_MODEL)
        a = jnp.maximum(jnp.dot(h, wup_ref[:, cols], preferred_element_type=F32), 0.0)
        acc = acc + jnp.dot((a * a).astype(BF16), wdn_ref[cols, :], preferred_element_type=F32)
    out_ref[...] = _rmsnorm(acc, gf_ref[...]) if final else acc


def _mlp(x2d, norm2, wup, wdn, norm_f, final, tm):
    m = x2d.shape[0]
    row = pl.BlockSpec((tm, D_MODEL), lambda i: (i, 0))
    return pl.pallas_call(
        functools.partial(_mlp_body, final=final),
        grid=(m // tm,),
        in_specs=[row, _const_spec((1, D_MODEL)), _const_spec((D_MODEL, D_FF)), _const_spec((D_FF, D_MODEL)),
                  _const_spec((1, D_MODEL))],
        out_specs=row,
        out_shape=jax.ShapeDtypeStruct((m, D_MODEL), F32),
        compiler_params=_params(1),
        name="mlp",
    )(x2d, norm2.reshape(1, D_MODEL), wup, wdn, norm_f.reshape(1, D_MODEL))


def _sample_mix_body(q_ref, k_ref, v_ref, ck0_ref, cv0_ref, ck1_ref, cv1_ref, ck2_ref, cv2_ref, zrow_ref, zcol_ref,
                     s_ref, ya_ref, yc_ref, s_out_ref, n0_ref, n1_ref, n2_ref):
    lane = lax.broadcasted_iota(jnp.int32, (8, GROUP_W), 1)
    rowi = lax.broadcasted_iota(jnp.int32, (8, GROUP_W), 0)
    head_of_row = (lane // HEAD_DIM) == rowi
    head0 = lax.broadcasted_iota(jnp.int32, (1, GROUP_W), 1) < HEAD_DIM

    outs, lses = [], []
    cache_refs = ((ck0_ref, cv0_ref), (ck1_ref, cv1_ref), (ck2_ref, cv2_ref))
    for g, ((ck_ref, cv_ref), n_ref, (win, dil)) in enumerate(zip(cache_refs, (n0_ref, n1_ref, n2_ref),
                                                                  DSWA_GROUPS)):
        q, k_new, v_new = q_ref[g], k_ref[g], v_ref[g]
        r = pl.ds(0, BAND, stride=dil) if dil > 1 else pl.ds(0, BAND)
        k_c, v_c = ck_ref[r, :], cv_ref[r, :]
        q2 = jnp.where(head_of_row, jnp.broadcast_to(q, (8, GROUP_W)), 0.0)
        scale = HEAD_DIM ** -0.5
        s = _dot_nt(q2, k_c) * scale
        s_self = jnp.sum(q2 * k_new, axis=-1, keepdims=True) * scale
        m = jnp.maximum(jnp.max(s, axis=-1, keepdims=True), s_self)
        p, p_self = jnp.exp(s - m), jnp.exp(s_self - m)
        l = jnp.sum(p, axis=-1, keepdims=True) + p_self
        o = (_dot(p, v_c) + p_self * v_new) / l
        lse = m + jnp.log(l)
        outs.append(jnp.where(head0, o[0:1], o[1:2]))
        lses.append(jnp.where(head0, jnp.broadcast_to(lse[0:1], (1, GROUP_W)),
                              jnp.broadcast_to(lse[1:2], (1, GROUP_W))))
        n_ref[0:win - 1, 0:GROUP_W] = ck_ref[1:win, :]
        n_ref[0:win - 1, GROUP_W:2 * GROUP_W] = cv_ref[1:win, :]
        n_ref[win - 1:win, 0:GROUP_W] = k_new
        n_ref[win - 1:win, GROUP_W:2 * GROUP_W] = v_new
    m = jnp.maximum(jnp.maximum(lses[0], lses[1]), lses[2])
    e = [jnp.exp(x - m) for x in lses]
    ya_ref[0] = (e[0] * outs[0] + e[1] * outs[1] + e[2] * outs[2]) / (e[0] + e[1] + e[2])

    zrow = zrow_ref[0]
    q_col = zcol_ref[0, 0:C_WIDTH, :]
    log_f = zcol_ref[0, C_WIDTH:2 * C_WIDTH, :]
    k_col = zcol_ref[0, 2 * C_WIDTH:3 * C_WIDTH, :]
    v_row = zrow[:, 3 * C_WIDTH:4 * C_WIDTH]
    gate_row = zrow[:, 4 * C_WIDTH:5 * C_WIDTH]
    v_sel = jnp.concatenate([jnp.broadcast_to(v_row[:, h * 64:(h + 1) * 64], (64, 64)) for h in range(C_HEADS)],
                            axis=0)
    s_new = jnp.exp(log_f) * s_ref[...] + k_col * v_sel
    s_out_ref[...] = s_new
    qs = q_col * s_new
    o4 = jnp.concatenate([jnp.sum(qs[h * 64:(h + 1) * 64, :], axis=0, keepdims=True) for h in range(C_HEADS)],
                         axis=0)
    gate4 = jnp.concatenate([gate_row[:, h * 64:(h + 1) * 64] for h in range(C_HEADS)], axis=0)
    o4 = o4 * lax.rsqrt(jnp.mean(o4 * o4, axis=-1, keepdims=True) + EPS) * gate4
    yc_ref[0] = o4


def _sample_mix(q, k, v, caches, zc, state, layer, new_caches):
    n = q.shape[1]
    depth = caches[0].shape[0]
    row3 = lambda w: pl.BlockSpec((1, 1, w), lambda b: (b, 0, 0))
    grp = pl.BlockSpec((3, None, 1, GROUP_W), lambda b: (0, b, 0, 0))
    cache_specs = [pl.BlockSpec((None, None, win, 2 * GROUP_W), lambda b: (layer, b, 0, 0))
                   for win, _ in DSWA_GROUPS]
    half_specs = [pl.BlockSpec((None, None, win, GROUP_W), functools.partial(lambda b, h: (layer, b, 0, h), h=h))
                  for win, _ in DSWA_GROUPS for h in range(2)]
    in_specs = ([grp] * 3 + half_specs
                + [row3(5 * C_WIDTH), pl.BlockSpec((1, 5 * C_WIDTH, 1), lambda b: (b, 0, 0)),
                   pl.BlockSpec((None, None, C_WIDTH, 64), lambda b: (layer, b, 0, 0))])
    args = [q.reshape(3, n, 1, GROUP_W), k.reshape(3, n, 1, GROUP_W), v.reshape(3, n, 1, GROUP_W),
            *[c for c in caches for _ in range(2)],
            zc.reshape(n, 1, 5 * C_WIDTH), zc.reshape(n, 5 * C_WIDTH, 1), state]
    aliases = {}
    n_extra = 0
    if new_caches is not None:
        n_extra = 3
        aliases = {len(args) + i: 3 + i for i in range(3)}
        in_specs = in_specs + [pl.BlockSpec(memory_space=pl.ANY)] * 3
        args = args + list(new_caches)

    def body(*refs):
        n_in = len(args) - n_extra
        _sample_mix_body(*refs[:n_in], *refs[n_in + n_extra:])

    return pl.pallas_call(
        body,
        grid=(n,),
        in_specs=in_specs,
        out_specs=[pl.BlockSpec((1, 1, GROUP_W), lambda b: (b, 0, 0)),
                   pl.BlockSpec((1, C_HEADS, 64), lambda b: (b, 0, 0)),
                   pl.BlockSpec((None, C_WIDTH, 64), lambda b: (b, 0, 0))] + cache_specs,
        out_shape=[jax.ShapeDtypeStruct((n, 1, GROUP_W), F32), jax.ShapeDtypeStruct((n, C_HEADS, 64), F32),
                   jax.ShapeDtypeStruct((n, C_WIDTH, 64), F32)]
        + [jax.ShapeDtypeStruct((depth, n, win, 2 * GROUP_W), F32) for win, _ in DSWA_GROUPS],
        input_output_aliases=aliases,
        compiler_params=_params(1),
        name="sample_mix",
    )(*args)


def kernel(x_prompt, x_sample, cache_kv_w128, cache_kv_w512, cache_kv_w2048, state_conv, state_hgrn, w_in, b_gate,
           norm1, conv_w, hgrn_lb, hgrn_norm, w_pa, w_pb, w_pc, w_o, norm2, w_up, w_down, norm_f):
    bp, tp, d = x_prompt.shape
    bs, ts, _ = x_sample.shape
    depth = w_in.shape[0]
    assert d == D_MODEL and ts == 1 and depth == 2
    assert PAST_LEN >= max(win for win, _ in DSWA_GROUPS)

    sm = jax.nn.softmax(hgrn_lb.astype(F32), axis=0)
    lower = jnp.cumsum(sm, axis=0) - sm[0:1]

    cos_p, sin_p = _rope_tables(jnp.arange(tp, dtype=jnp.int32))
    cos_s, sin_s = _rope_tables(jnp.full((bs,), PAST_LEN, dtype=jnp.int32))

    caches = [c.reshape(depth, bs, win, 2 * GROUP_W)
              for c, (win, _) in zip((cache_kv_w128, cache_kv_w512, cache_kv_w2048), DSWA_GROUPS)]
    state = state_hgrn.reshape(depth, bs, C_WIDTH, 64)

    xp = x_prompt.reshape(bp * tp, d)
    xs = x_sample.reshape(bs, d)
    kv_p, kv_s = None, None
    conv_p, hgrn_p, conv_s, hgrn_s = [], [], [], []
    for l in range(depth):
        w_a, w_b, w_c, w_g = (w_in[l, :, c0:c1].astype(BF16) for c0, c1 in
                              ((0, COL_B), (COL_B, COL_C), (COL_C, COL_G), (COL_G, IN_COLS)))
        wpa, wpb, wpc, wo = (w[l].astype(BF16) for w in (w_pa, w_pb, w_pc, w_o))
        wup, wdn = w_up[l].astype(BF16), w_down[l].astype(BF16)
        final = l == depth - 1

        q, k, v, zc = _inproj(xp, norm1[l], w_a, w_c, cos_p, sin_p, lower[l], hgrn_norm[l], tm=256, chunked=True)
        ya, *kv_p = _attn_prompt(q.reshape(3, bp, tp, -1), k.reshape(3, bp, tp, -1), v.reshape(3, bp, tp, -1),
                                 l, kv_p)
        yc, st = _hgrn_prompt(zc.reshape(bp, tp, -1))
        x1, nc = _merge_prompt(xp.reshape(bp, tp, d), ya, yc, norm1[l], w_b, w_g,
                               b_gate[l], conv_w[l], wpa, wpb, wpc, wo)
        xp = _mlp(x1.reshape(bp * tp, d), norm2[l], wup, wdn, norm_f, final, tm=512)
        conv_p.append(nc)
        hgrn_p.append(st)

        q, k, v, zc = _inproj(xs, norm1[l], w_a, w_c, cos_s, sin_s, lower[l], hgrn_norm[l], tm=bs, chunked=False)
        ya, yc, s_new, *kv_s = _sample_mix(q, k, v, caches, zc, state, l, kv_s)
        x1, u = _merge_sample(xs, ya.reshape(bs, GROUP_W), yc.reshape(bs, C_WIDTH), norm1[l], w_b, w_g,
                              state_conv[l, :, 0], state_conv[l, :, 1], b_gate[l], conv_w[l], wpa, wpb, wpc, wo)
        xs = _mlp(x1, norm2[l], wup, wdn, norm_f, final, tm=bs)
        conv_s.append(jnp.stack([state_conv[l, :, 1], u], axis=1))
        hgrn_s.append(s_new.reshape(bs, C_HEADS, 64, 64))

    def kv6(c, n):
        return c.reshape(depth, n, c.shape[2], 2, 2, HEAD_DIM)

    return (xp.reshape(bp, tp, d), xs.reshape(bs, ts, d),
            kv6(kv_p[0], bp), kv6(kv_p[1], bp), kv6(kv_p[2], bp), jnp.stack(conv_p), jnp.stack(hgrn_p),
            kv6(kv_s[0], bs), kv6(kv_s[1], bs), kv6(kv_s[2], bs), jnp.stack(conv_s), jnp.stack(hgrn_s))
```

```python
import functools

import jax
import jax.numpy as jnp
from jax import lax
from jax.experimental import pallas as pl
from jax.experimental.pallas import tpu as pltpu

F32 = jnp.float32
BF16 = jnp.bfloat16

D_MODEL = 1024
HEAD_DIM = 64
DSWA_GROUPS = ((128, 1), (512, 4), (2048, 16))
BAND = 128
GROUP_W = 2 * HEAD_DIM
A_WIDTH = 3 * GROUP_W
CONV_WIDTH = 384
CONV_K = 3
C_HEADS = 4
C_WIDTH = C_HEADS * 64
N_BRANCH = 3
D_FF = 4 * D_MODEL
ROPE_THETA = 10000.0
EPS = 1e-6
NEG_BIG = -1e30
LB_FLOOR = 1e-30
PAST_LEN = 8192
IN_COLS = 3 * A_WIDTH + 3 * CONV_WIDTH + 4 * C_WIDTH + N_BRANCH * D_MODEL
COL_B = 3 * A_WIDTH
COL_C = COL_B + 3 * CONV_WIDTH
COL_G = COL_C + 4 * C_WIDTH
CHUNK = 64
SUB = 8
VMEM_LIMIT = 56 * 1024 * 1024
ROW_TILE = 512
SUB_ROWS = 256


def _params(n_grid):
    return pltpu.CompilerParams(dimension_semantics=("arbitrary",) * n_grid,
                                vmem_limit_bytes=VMEM_LIMIT)


def _const_spec(shape):
    nd = len(shape)
    return pl.BlockSpec(shape, lambda *_: (0,) * nd, pipeline_mode=pl.Buffered(1))


def _dot(a, b):
    return jnp.dot(a.astype(BF16), b.astype(BF16), preferred_element_type=F32)


def _dot_nt(a, b):
    return lax.dot_general(a.astype(BF16), b.astype(BF16), (((1,), (1,)), ((), ())),
                           preferred_element_type=F32)


def _dot_tn(a, b):
    return lax.dot_general(a.astype(BF16), b.astype(BF16), (((0,), (0,)), ((), ())),
                           preferred_element_type=F32)


def _rmsnorm(x, g):
    return x * lax.rsqrt(jnp.mean(x * x, axis=-1, keepdims=True) + EPS) * g


def _inproj_body(x_ref, g_ref, wa_ref, wc_ref, cos_ref, sin_ref, lb_ref, gn_ref, q_ref, k_ref, v_ref, zc_ref,
                 *, chunked):
    tm = x_ref.shape[0]
    sub = min(tm, SUB_ROWS)
    lane = lax.broadcasted_iota(jnp.int32, (sub, GROUP_W), 1)
    first_half = (lane % HEAD_DIM) < HEAD_DIM // 2

    for r0 in range(0, tm, sub):
        rows = slice(r0, r0 + sub)
        h = _rmsnorm(x_ref[rows, :], g_ref[...]).astype(BF16)
        cos = cos_ref[rows, :]
        sin = sin_ref[rows, :]
        za = jnp.dot(h, wa_ref[...], preferred_element_type=F32)
        for out_ref, c0 in ((q_ref, 0), (k_ref, A_WIDTH)):
            for g in range(3):
                z = za[:, c0 + g * GROUP_W:c0 + (g + 1) * GROUP_W]
                partner = jnp.where(first_half, pltpu.roll(z, GROUP_W - HEAD_DIM // 2, 1),
                                    pltpu.roll(z, HEAD_DIM // 2, 1))
                out_ref[g, rows, :] = z * cos + partner * sin
        for g in range(3):
            v_ref[g, rows, :] = za[:, 2 * A_WIDTH + g * GROUP_W:2 * A_WIDTH + (g + 1) * GROUP_W]
        parts = _hgrn_inputs(jnp.dot(h, wc_ref[...], preferred_element_type=F32), lb_ref[...], gn_ref[...],
                             chunked)
        for i, part in enumerate(parts):
            zc_ref[rows, i * C_WIDTH:(i + 1) * C_WIDTH] = part


def _inproj(x2d, norm, w_a, w_c, cos, sin, lb, gnorm, tm, chunked):
    m = x2d.shape[0]
    assert not chunked or tm % CHUNK == 0
    n_pos = cos.shape[0] // tm
    row = lambda w: pl.BlockSpec((tm, w), lambda i: (i, 0))
    tab = pl.BlockSpec((tm, GROUP_W), lambda i: (i % n_pos, 0))
    grp = pl.BlockSpec((3, tm, GROUP_W), lambda i: (0, i, 0))
    return pl.pallas_call(
        functools.partial(_inproj_body, chunked=chunked),
        grid=(m // tm,),
        in_specs=[row(D_MODEL), _const_spec((1, D_MODEL)), _const_spec(w_a.shape), _const_spec(w_c.shape),
                  tab, tab, _const_spec((1, C_WIDTH)), _const_spec((1, C_WIDTH))],
        out_specs=[grp] * 3 + [row(5 * C_WIDTH)],
        out_shape=[jax.ShapeDtypeStruct((3, m, GROUP_W), F32)] * 3
        + [jax.ShapeDtypeStruct((m, 5 * C_WIDTH), F32)],
        compiler_params=_params(1),
        name="inproj",
    )(x2d, norm.reshape(1, D_MODEL), w_a, w_c, cos, sin, lb.reshape(1, C_WIDTH), gnorm.reshape(1, C_WIDTH))


def _rope_tables(pos):
    half = HEAD_DIM // 2
    inv = ROPE_THETA ** (-jnp.arange(half, dtype=F32) / half)
    ang = pos.astype(F32)[:, None] * inv[None, :]
    cos, sin = jnp.cos(ang), jnp.sin(ang)
    return jnp.tile(cos, (1, 4)), jnp.tile(jnp.concatenate([-sin, sin], axis=1), (1, 2))


def _attend(tiles):
    head0 = lax.broadcasted_iota(jnp.int32, (BAND, GROUP_W), 1) < HEAD_DIM
    scale = HEAD_DIM ** -0.5
    q2 = [jnp.concatenate([jnp.where(head0, q, 0.0), jnp.where(head0, 0.0, q)], axis=0).astype(BF16)
          for q, _ in tiles]
    s = [[jnp.where(valid, _dot_nt(qq, k) * scale, NEG_BIG) for k, _, valid in blocks]
         for qq, (_, blocks) in zip(q2, tiles)]
    m = [jnp.max(functools.reduce(jnp.maximum, x), axis=-1, keepdims=True) for x in s]
    p = [[jnp.exp(x - mm).astype(BF16) for x in ss] for ss, mm in zip(s, m)]
    ext = [functools.reduce(jnp.add, [jnp.dot(pp, v1, preferred_element_type=F32)
                                      for pp, (_, v1, _) in zip(pt, blocks)])
           for pt, (_, blocks) in zip(p, tiles)]
    results = []
    for e, mm in zip(ext, m):
        den = e[:, GROUP_W:]
        o2 = e[:, 0:GROUP_W] / den
        lse2 = mm + jnp.log(den)
        results.append((jnp.where(head0, o2[0:BAND], o2[BAND:]), jnp.where(head0, lse2[0:BAND], lse2[BAND:])))
    return results


def _attn_prompt_body(q_ref, k_ref, v_ref, ya_ref, c0_ref, c1_ref, c2_ref, o_scr, lse_scr, *, seq):
    qi = lax.broadcasted_iota(jnp.int32, (2 * BAND, BAND), 0) % BAND
    kj = lax.broadcasted_iota(jnp.int32, (2 * BAND, BAND), 1)
    cur_valid = kj <= qi
    prev_valid = kj >= qi
    ones = jnp.ones((BAND, GROUP_W), BF16)

    def rows(start, dil):
        return pl.ds(start, BAND, stride=dil) if dil > 1 else pl.ds(start, BAND)

    def key_block(g, r):
        return k_ref[g, r, :].astype(BF16), jnp.concatenate([v_ref[g, r, :].astype(BF16), ones], axis=1)

    def run(g, dil, chains):
        tiles, slices = [], []
        for starts, first_prev in chains:
            chain = [rows(s, dil) for s in starts]
            keys = [key_block(g, r) for r in chain]
            for j, r in enumerate(chain):
                blocks = [(*keys[j], cur_valid)]
                if j > 0:
                    blocks.append((*keys[j - 1], prev_valid))
                elif first_prev is not None:
                    blocks.append((*key_block(g, rows(first_prev[0], dil)), prev_valid & first_prev[1]))
                tiles.append((q_ref[g, r, :], blocks))
            slices += chain
        for r, (o, lse) in zip(slices, _attend(tiles)):
            o_scr[g, r, :] = o
            lse_scr[g, r, :] = lse

    for g, (_, dil) in enumerate(DSWA_GROUPS):
        n_blocks = seq // dil // BAND
        if dil == 1:
            def dense(i, carry):
                s0 = pl.multiple_of(4 * i * BAND, BAND)
                prev = pl.multiple_of(jnp.maximum(4 * i - 1, 0) * BAND, BAND)
                run(g, dil, [([s0 + u * BAND for u in range(4)], (prev, i > 0))])
                return carry

            lax.fori_loop(0, n_blocks // 4, dense, 0)
        elif n_blocks > 1:
            def residue(r, carry, g=g, dil=dil, n_blocks=n_blocks):
                run(g, dil, [([r + i * BAND * dil for i in range(n_blocks)], None)])
                return carry

            lax.fori_loop(0, dil, residue, 0)
        else:
            def residues(j, carry, g=g, dil=dil):
                run(g, dil, [([4 * j + u], None) for u in range(4)])
                return carry

            lax.fori_loop(0, dil // 4, residues, 0)

    step = 256

    def merge(i, carry):
        sl = pl.ds(pl.multiple_of(i * step, step), step)
        lse = [lse_scr[g, sl, :] for g in range(3)]
        m = jnp.maximum(jnp.maximum(lse[0], lse[1]), lse[2])
        e = [jnp.exp(x - m) for x in lse]
        acc = e[0] * o_scr[0, sl, :] + e[1] * o_scr[1, sl, :] + e[2] * o_scr[2, sl, :]
        ya_ref[0, sl, :] = acc / (e[0] + e[1] + e[2])
        return carry

    lax.fori_loop(0, seq // step, merge, 0)

    for g, (c_ref, (win, _)) in enumerate(zip((c0_ref, c1_ref, c2_ref), DSWA_GROUPS)):
        for j in range(0, win, 512):
            w = min(512, win - j)
            c_ref[0:GROUP_W, j:j + w] = k_ref[g, seq - win + j:seq - win + j + w, :].T
            c_ref[GROUP_W:2 * GROUP_W, j:j + w] = v_ref[g, seq - win + j:seq - win + j + w, :].T


def _attn_prompt(q, k, v, layer, caches):
    _, bsz, seq, _ = q.shape
    depth = 2
    assert seq % (BAND * 16) == 0 and seq >= 2048
    qkv_spec = pl.BlockSpec((3, None, seq, GROUP_W), lambda b: (0, b, 0, 0))
    cache_specs = [pl.BlockSpec((None, None, 2 * GROUP_W, win), lambda b: (layer, b, 0, 0))
                   for win, _ in DSWA_GROUPS]
    cache_shapes = [jax.ShapeDtypeStruct((depth, bsz, 2 * GROUP_W, win), F32) for win, _ in DSWA_GROUPS]
    in_specs = [qkv_spec] * 3
    args = [q, k, v]
    aliases = {}
    if caches is not None:
        in_specs = in_specs + [pl.BlockSpec(memory_space=pl.ANY)] * 3
        args = args + list(caches)
        aliases = {3: 1, 4: 2, 5: 3}

    def body(*refs):
        ins, rest = refs[:3], refs[3 + (3 if caches is not None else 0):]
        _attn_prompt_body(*ins, *rest, seq=seq)

    return pl.pallas_call(
        body,
        grid=(bsz,),
        in_specs=in_specs,
        out_specs=[pl.BlockSpec((1, seq, GROUP_W), lambda b: (b, 0, 0))] + cache_specs,
        out_shape=[jax.ShapeDtypeStruct((bsz, seq, GROUP_W), F32)] + cache_shapes,
        scratch_shapes=[pltpu.VMEM((3, seq, GROUP_W), F32), pltpu.VMEM((3, seq, GROUP_W), F32)],
        input_output_aliases=aliases,
        compiler_params=_params(1),
        name="attn_prompt",
    )(*args)


def _log_forget_and_kgate(zf, lb):
    log_sig = -(jnp.maximum(-zf, 0.0) + jnp.log1p(jnp.exp(-jnp.abs(zf))))
    a = jnp.log(jnp.maximum(lb, LB_FLOOR))
    c = jnp.log1p(-lb) + log_sig
    log_f = jnp.maximum(a, c) + jnp.log1p(jnp.exp(-jnp.abs(a - c)))
    kgate = (1.0 - lb) * jax.nn.sigmoid(-zf)
    return log_f, kgate


def _silu(x):
    return x * jax.nn.sigmoid(x)


def _split3(x):
    hi = x.astype(BF16)
    r1 = x - hi.astype(F32)
    mid = r1.astype(BF16)
    lo = (r1 - mid.astype(F32)).astype(BF16)
    return hi, mid, lo


def _hgrn_inputs(zc, lb, gnorm, chunked):
    q, zf, v, og = (zc[:, i * C_WIDTH:(i + 1) * C_WIDTH] for i in range(4))
    log_f, k = _log_forget_and_kgate(zf, lb)
    if chunked:
        n = zc.shape[0]
        r = lax.broadcasted_iota(jnp.int32, (n, n), 0)
        c = lax.broadcasted_iota(jnp.int32, (n, n), 1)
        tri = jnp.where((c <= r) & ((r // CHUNK) == (c // CHUNK)), 1.0, 0.0).astype(BF16)
        b = functools.reduce(jnp.add, [jnp.dot(tri, p, preferred_element_type=F32) for p in _split3(log_f)])
    else:
        b = log_f
    return q, b, k, v, gnorm * _silu(og)


def _hgrn_chunk(q, b, k, v, gate, st, kbv_scr):
    n = CHUNK
    row = lax.broadcasted_iota(jnp.int32, (n, C_WIDTH), 0)
    lane_head = lax.broadcasted_iota(jnp.int32, (n, C_WIDTH), 1) // 64

    hr = lax.broadcasted_iota(jnp.int32, (C_WIDTH, C_WIDTH), 0) // 64
    hc = lax.broadcasted_iota(jnp.int32, (C_WIDTH, C_WIDTH), 1) // 64
    same_head = hr == hc
    ones_bd = jnp.where(same_head, 1.0, 0.0).astype(BF16)

    kbv_scr[0, SUB:SUB + n, :] = k
    kbv_scr[1, SUB:SUB + n, :] = b
    kbv_scr[2, SUB:SUB + n, :] = v
    o = jnp.dot((q * k).astype(BF16), ones_bd, preferred_element_type=F32) * v
    for d in range(1, SUB):
        sl = pl.ds(SUB - d, n)
        k_d, b_d, v_d = kbv_scr[0, sl, :], kbv_scr[1, sl, :], kbv_scr[2, sl, :]
        in_tile = (row % SUB) >= d
        y = jnp.where(in_tile, q * k_d * jnp.exp(jnp.minimum(b - b_d, 0.0)), 0.0)
        o = o + jnp.dot(y.astype(BF16), ones_bd, preferred_element_type=F32) * v_d

    srow = lax.broadcasted_iota(jnp.int32, (C_HEADS * n, n), 0) % n
    scol = lax.broadcasted_iota(jnp.int32, (C_HEADS * n, n), 1)
    att = jnp.zeros((C_HEADS * n, n), F32)
    m = n // 2
    while m >= SUB:
        ref_q, ref_k = [], []
        for j in range(n // m):
            own = b[j * m:(j + 1) * m, :]
            if j % 2 == 1:
                ref_q.append(jnp.broadcast_to(b[j * m - 1:j * m, :], (m, C_WIDTH)))
                ref_k.append(own)
            else:
                ref_q.append(own)
                ref_k.append(jnp.broadcast_to(b[(j + 1) * m - 1:(j + 1) * m, :], (m, C_WIDTH)))
        upper = ((row // m) % 2) == 1
        qt = jnp.where(upper, q * jnp.exp(jnp.minimum(b - jnp.concatenate(ref_q, axis=0), 0.0)), 0.0)
        kt = jnp.where(upper, 0.0, k * jnp.exp(jnp.minimum(jnp.concatenate(ref_k, axis=0) - b, 0.0)))
        q_stack = jnp.concatenate([jnp.where(lane_head == h, qt, 0.0) for h in range(C_HEADS)], axis=0)
        a = _dot_nt(q_stack, kt)
        att = att + jnp.where((srow // (2 * m)) == (scol // (2 * m)), a, 0.0)
        m //= 2
    r = _dot(att, v)
    for h in range(C_HEADS):
        o = o + jnp.where(lane_head == h, r[h * n:(h + 1) * n, :], 0.0)

    o = o + _dot_nt(q * jnp.exp(b), st)
    b_last = b[n - 1:n, :]
    k_hat = k * jnp.exp(b_last - b)
    st = st * jnp.exp(b_last) + jnp.where(same_head, _dot_tn(v, k_hat), 0.0)

    ms = jnp.zeros_like(o)
    for h in range(C_HEADS):
        sel = lane_head == h
        ms = ms + jnp.where(sel, jnp.sum(jnp.where(sel, o * o, 0.0), axis=-1, keepdims=True), 0.0)
    o = o * lax.rsqrt(ms * (1.0 / 64) + EPS) * gate
    return o, st


def _hgrn_prompt_body(zc_ref, yc_ref, st_ref, st_scr, kbv_scr, *, rows):
    t = pl.program_id(1)
    kbv_scr[:, 0:SUB, :] = jnp.zeros((3, SUB, C_WIDTH), F32)

    @pl.when(t == 0)
    def _():
        st_scr[...] = jnp.zeros_like(st_scr)

    def chunk(c, carry):
        sl = pl.ds(pl.multiple_of(c * CHUNK, CHUNK), CHUNK)
        o, st = _hgrn_chunk(*(zc_ref[0, sl, i * C_WIDTH:(i + 1) * C_WIDTH] for i in range(5)),
                            st_scr[...], kbv_scr)
        st_scr[...] = st
        yc_ref[0, sl, :] = o
        return carry

    lax.fori_loop(0, rows // CHUNK, chunk, 0)

    @pl.when(t == pl.num_programs(1) - 1)
    def _():
        for h in range(C_HEADS):
            st_ref[0, h] = st_scr[h * 64:(h + 1) * 64, h * 64:(h + 1) * 64]


def _hgrn_prompt(zc, rows=ROW_TILE):
    bsz, seq, _ = zc.shape
    yc, st = pl.pallas_call(
        functools.partial(_hgrn_prompt_body, rows=rows),
        grid=(bsz, seq // rows),
        in_specs=[pl.BlockSpec((1, rows, 5 * C_WIDTH), lambda b, t: (b, t, 0))],
        out_specs=[pl.BlockSpec((1, rows, C_WIDTH), lambda b, t: (b, t, 0)),
                   pl.BlockSpec((1, C_HEADS, 64, 64), lambda b, t: (b, 0, 0, 0))],
        out_shape=[jax.ShapeDtypeStruct((bsz, seq, C_WIDTH), F32),
                   jax.ShapeDtypeStruct((bsz, C_HEADS, 64, 64), F32)],
        scratch_shapes=[pltpu.VMEM((C_WIDTH, C_WIDTH), F32), pltpu.VMEM((3, SUB + CHUNK, C_WIDTH), F32)],
        compiler_params=_params(2),
        name="hgrn_prompt",
    )(zc)
    return yc, jnp.swapaxes(st, -1, -2)


def _gated_merge(x, ya, yb, yc, zg, bg, wpa, wpb, wpc, wo):
    g = jax.nn.sigmoid(zg + bg)
    mix = (g[:, 0:D_MODEL] * _dot(ya, wpa) + g[:, D_MODEL:2 * D_MODEL] * _dot(yb, wpb)
           + g[:, 2 * D_MODEL:3 * D_MODEL] * _dot(yc, wpc))
    return x + _dot(mix, wo)


def _project_conv_and_gates(x, g1, wb, wg):
    h = _rmsnorm(x, g1).astype(BF16)
    return jnp.dot(h, wb, preferred_element_type=F32), jnp.dot(h, wg, preferred_element_type=F32)


def _merge_prompt_body(x_ref, ya_ref, yc_ref, g1_ref, wb_ref, wg_ref, bg_ref, cw_ref, wpa_ref, wpb_ref, wpc_ref,
                       wo_ref, x1_ref, nc_ref, u_scr, *, rows):
    @pl.when(pl.program_id(1) == 0)
    def _():
        u_scr[0:8, :] = jnp.zeros((8, CONV_WIDTH), F32)

    zb, zg = _project_conv_and_gates(x_ref[0], g1_ref[...], wb_ref[...], wg_ref[...])
    u = zb[:, 2 * CONV_WIDTH:3 * CONV_WIDTH] * zb[:, 0:CONV_WIDTH]
    u_scr[8:8 + rows, :] = u
    conv = (cw_ref[0:1, :] * u_scr[pl.ds(6, rows), :] + cw_ref[1:2, :] * u_scr[pl.ds(7, rows), :]
            + cw_ref[2:3, :] * u)
    yb = zb[:, CONV_WIDTH:2 * CONV_WIDTH] * conv
    u_scr[0:8, :] = u_scr[rows:rows + 8, :]
    nc_ref[0] = u[rows - (CONV_K - 1):rows, :]
    x1_ref[0] = _gated_merge(x_ref[0], ya_ref[0], yb, yc_ref[0], zg, bg_ref[...],
                             wpa_ref[...], wpb_ref[...], wpc_ref[...], wo_ref[...])


def _merge_prompt(x, ya, yc, norm1, w_b, w_g, b_gate, conv_w, wpa, wpb, wpc, wo, rows=ROW_TILE):
    bsz, seq, _ = x.shape
    blk = lambda w: pl.BlockSpec((1, rows, w), lambda b, t: (b, t, 0))
    return pl.pallas_call(
        functools.partial(_merge_prompt_body, rows=rows),
        grid=(bsz, seq // rows),
        in_specs=[blk(D_MODEL), blk(GROUP_W), blk(C_WIDTH),
                  _const_spec((1, D_MODEL)), _const_spec(w_b.shape), _const_spec(w_g.shape),
                  _const_spec((1, N_BRANCH * D_MODEL)), _const_spec((CONV_K, CONV_WIDTH)),
                  _const_spec((GROUP_W, D_MODEL)), _const_spec((CONV_WIDTH, D_MODEL)),
                  _const_spec((C_WIDTH, D_MODEL)), _const_spec((D_MODEL, D_MODEL))],
        out_specs=[blk(D_MODEL), pl.BlockSpec((1, CONV_K - 1, CONV_WIDTH), lambda b, t: (b, 0, 0))],
        out_shape=[jax.ShapeDtypeStruct((bsz, seq, D_MODEL), F32),
                   jax.ShapeDtypeStruct((bsz, CONV_K - 1, CONV_WIDTH), F32)],
        scratch_shapes=[pltpu.VMEM((rows + 8, CONV_WIDTH), F32)],
        compiler_params=_params(2),
        name="merge_prompt",
    )(x, ya, yc, norm1.reshape(1, D_MODEL), w_b, w_g, b_gate.reshape(1, -1), conv_w, wpa, wpb, wpc, wo)


def _merge_sample_body(x_ref, ya_ref, yc_ref, g1_ref, wb_ref, wg_ref, p0_ref, p1_ref, bg_ref, cw_ref, wpa_ref,
                       wpb_ref, wpc_ref, wo_ref, x1_ref, u_ref):
    zb, zg = _project_conv_and_gates(x_ref[...], g1_ref[...], wb_ref[...], wg_ref[...])
    u = zb[:, 2 * CONV_WIDTH:3 * CONV_WIDTH] * zb[:, 0:CONV_WIDTH]
    conv = cw_ref[0:1, :] * p0_ref[...] + cw_ref[1:2, :] * p1_ref[...] + cw_ref[2:3, :] * u
    yb = zb[:, CONV_WIDTH:2 * CONV_WIDTH] * conv
    u_ref[...] = u
    x1_ref[...] = _gated_merge(x_ref[...], ya_ref[...], yb, yc_ref[...], zg, bg_ref[...],
                               wpa_ref[...], wpb_ref[...], wpc_ref[...], wo_ref[...])


def _merge_sample(x, ya, yc, norm1, w_b, w_g, prev0, prev1, b_gate, conv_w, wpa, wpb, wpc, wo):
    m = x.shape[0]
    args = (x, ya, yc, norm1.reshape(1, D_MODEL), w_b, w_g, prev0, prev1, b_gate.reshape(1, -1), conv_w,
            wpa, wpb, wpc, wo)
    return pl.pallas_call(
        _merge_sample_body,
        grid=(1,),
        in_specs=[_const_spec(a.shape) for a in args],
        out_specs=[_const_spec((m, D_MODEL)), _const_spec((m, CONV_WIDTH))],
        out_shape=[jax.ShapeDtypeStruct((m, D_MODEL), F32), jax.ShapeDtypeStruct((m, CONV_WIDTH), F32)],
        compiler_params=_params(1),
        name="merge_sample",
    )(*args)


def _mlp_body(x_ref, g2_ref, wup_ref, wdn_ref, gf_ref, out_ref, *, final):
    x = x_ref[...]
    h = _rmsnorm(x, g2_ref[...]).astype(BF16)
    acc = x
    for c in range(D_FF // D_MODEL):
        cols = slice(c * D_MODEL, (c + 1) * D_MODEL)
        a = jnp.maximum(jnp.dot(h, wup_ref[:, cols], preferred_element_type=F32), 0.0)
        acc = acc + jnp.dot((a * a).astype(BF16), wdn_ref[cols, :], preferred_element_type=F32)
    out_ref[...] = _rmsnorm(acc, gf_ref[...]) if final else acc


def _mlp(x2d, norm2, wup, wdn, norm_f, final, tm):
    m = x2d.shape[0]
    row = pl.BlockSpec((tm, D_MODEL), lambda i: (i, 0))
    return pl.pallas_call(
        functools.partial(_mlp_body, final=final),
        grid=(m // tm,),
        in_specs=[row, _const_spec((1, D_MODEL)), _const_spec((D_MODEL, D_FF)), _const_spec((D_FF, D_MODEL)),
                  _const_spec((1, D_MODEL))],
        out_specs=row,
        out_shape=jax.ShapeDtypeStruct((m, D_MODEL), F32),
        compiler_params=_params(1),
        name="mlp",
    )(x2d, norm2.reshape(1, D_MODEL), wup, wdn, norm_f.reshape(1, D_MODEL))


def _sample_mix_body(q_ref, k_ref, v_ref, kvcol_ref, c0_ref, c1_ref, c2_ref, zrow_ref, zcol_ref,
                     s_ref, ya_ref, yc_ref, s_out_ref, n0_ref, n1_ref, n2_ref):
    lane = lax.broadcasted_iota(jnp.int32, (8, GROUP_W), 1)
    rowi = lax.broadcasted_iota(jnp.int32, (8, GROUP_W), 0)
    head_of_row = (lane // HEAD_DIM) == rowi
    head0 = lax.broadcasted_iota(jnp.int32, (1, GROUP_W), 1) < HEAD_DIM

    outs, lses = [], []
    for g, (c_ref, n_ref, (win, dil)) in enumerate(zip((c0_ref, c1_ref, c2_ref), (n0_ref, n1_ref, n2_ref),
                                                        DSWA_GROUPS)):
        q, k_new, v_new = q_ref[g], k_ref[g], v_ref[g]
        buf = c_ref[...]
        pos = lax.broadcasted_iota(jnp.int32, (8, win), 1)
        q2 = jnp.where(head_of_row, jnp.broadcast_to(q, (8, GROUP_W)), 0.0)
        scale = HEAD_DIM ** -0.5
        s = jnp.where(pos % dil == 0, _dot(q2, buf[0:GROUP_W, :]) * scale, NEG_BIG)
        s_self = jnp.sum(q2 * k_new, axis=-1, keepdims=True) * scale
        m = jnp.maximum(jnp.max(s, axis=-1, keepdims=True), s_self)
        p, p_self = jnp.exp(s - m), jnp.exp(s_self - m)
        l = jnp.sum(p, axis=-1, keepdims=True) + p_self
        o = (_dot_nt(p, buf[GROUP_W:2 * GROUP_W, :]) + p_self * v_new) / l
        lse = m + jnp.log(l)
        outs.append(jnp.where(head0, o[0:1], o[1:2]))
        lses.append(jnp.where(head0, jnp.broadcast_to(lse[0:1], (1, GROUP_W)),
                              jnp.broadcast_to(lse[1:2], (1, GROUP_W))))
        last = lax.broadcasted_iota(jnp.int32, buf.shape, 1) == win - 1
        n_ref[...] = jnp.where(last, kvcol_ref[g], pltpu.roll(buf, win - 1, 1))
    m = jnp.maximum(jnp.maximum(lses[0], lses[1]), lses[2])
    e = [jnp.exp(x - m) for x in lses]
    ya_ref[0] = (e[0] * outs[0] + e[1] * outs[1] + e[2] * outs[2]) / (e[0] + e[1] + e[2])

    zrow = zrow_ref[0]
    q_col = zcol_ref[0, 0:C_WIDTH, :]
    log_f = zcol_ref[0, C_WIDTH:2 * C_WIDTH, :]
    k_col = zcol_ref[0, 2 * C_WIDTH:3 * C_WIDTH, :]
    v_row = zrow[:, 3 * C_WIDTH:4 * C_WIDTH]
    gate_row = zrow[:, 4 * C_WIDTH:5 * C_WIDTH]
    v_sel = jnp.concatenate([jnp.broadcast_to(v_row[:, h * 64:(h + 1) * 64], (64, 64)) for h in range(C_HEADS)],
                            axis=0)
    s_new = jnp.exp(log_f) * s_ref[...] + k_col * v_sel
    s_out_ref[...] = s_new
    qs = q_col * s_new
    o4 = jnp.concatenate([jnp.sum(qs[h * 64:(h + 1) * 64, :], axis=0, keepdims=True) for h in range(C_HEADS)],
                         axis=0)
    gate4 = jnp.concatenate([gate_row[:, h * 64:(h + 1) * 64] for h in range(C_HEADS)], axis=0)
    o4 = o4 * lax.rsqrt(jnp.mean(o4 * o4, axis=-1, keepdims=True) + EPS) * gate4
    yc_ref[0] = o4


def _sample_mix(q, k, v, caches, zc, state, layer, new_caches):
    n = q.shape[1]
    depth = caches[0].shape[0]
    row3 = lambda w: pl.BlockSpec((1, 1, w), lambda b: (b, 0, 0))
    grp = pl.BlockSpec((3, None, 1, GROUP_W), lambda b: (0, b, 0, 0))
    cache_specs = [pl.BlockSpec((None, None, 2 * GROUP_W, win), lambda b: (layer, b, 0, 0))
                   for win, _ in DSWA_GROUPS]
    in_specs = ([grp] * 3 + [pl.BlockSpec((3, None, 2 * GROUP_W, 1), lambda b: (0, b, 0, 0))] + cache_specs
                + [row3(5 * C_WIDTH), pl.BlockSpec((1, 5 * C_WIDTH, 1), lambda b: (b, 0, 0)),
                   pl.BlockSpec((None, None, C_WIDTH, 64), lambda b: (layer, b, 0, 0))])
    kv_col = jnp.concatenate([k, v], axis=-1).reshape(3, n, 2 * GROUP_W, 1)
    args = [q.reshape(3, n, 1, GROUP_W), k.reshape(3, n, 1, GROUP_W), v.reshape(3, n, 1, GROUP_W), kv_col,
            *caches, zc.reshape(n, 1, 5 * C_WIDTH), zc.reshape(n, 5 * C_WIDTH, 1), state]
    aliases = {}
    n_extra = 0
    if new_caches is not None:
        n_extra = 3
        aliases = {len(args) + i: 3 + i for i in range(3)}
        in_specs = in_specs + [pl.BlockSpec(memory_space=pl.ANY)] * 3
        args = args + list(new_caches)

    def body(*refs):
        n_in = len(args) - n_extra
        _sample_mix_body(*refs[:n_in], *refs[n_in + n_extra:])

    return pl.pallas_call(
        body,
        grid=(n,),
        in_specs=in_specs,
        out_specs=[pl.BlockSpec((1, 1, GROUP_W), lambda b: (b, 0, 0)),
                   pl.BlockSpec((1, C_HEADS, 64), lambda b: (b, 0, 0)),
                   pl.BlockSpec((None, C_WIDTH, 64), lambda b: (b, 0, 0))] + cache_specs,
        out_shape=[jax.ShapeDtypeStruct((n, 1, GROUP_W), F32), jax.ShapeDtypeStruct((n, C_HEADS, 64), F32),
                   jax.ShapeDtypeStruct((n, C_WIDTH, 64), F32)]
        + [jax.ShapeDtypeStruct((depth, n, 2 * GROUP_W, win), F32) for win, _ in DSWA_GROUPS],
        input_output_aliases=aliases,
        compiler_params=_params(1),
        name="sample_mix",
    )(*args)


def kernel(x_prompt, x_sample, cache_kv_w128, cache_kv_w512, cache_kv_w2048, state_conv, state_hgrn, w_in, b_gate,
           norm1, conv_w, hgrn_lb, hgrn_norm, w_pa, w_pb, w_pc, w_o, norm2, w_up, w_down, norm_f):
    bp, tp, d = x_prompt.shape
    bs, ts, _ = x_sample.shape
    depth = w_in.shape[0]
    assert d == D_MODEL and ts == 1 and depth == 2
    assert PAST_LEN >= max(win for win, _ in DSWA_GROUPS)

    sm = jax.nn.softmax(hgrn_lb.astype(F32), axis=0)
    lower = jnp.cumsum(sm, axis=0) - sm[0:1]

    cos_p, sin_p = _rope_tables(jnp.arange(tp, dtype=jnp.int32))
    cos_s, sin_s = _rope_tables(jnp.full((bs,), PAST_LEN, dtype=jnp.int32))

    caches = [jnp.transpose(c, (0, 1, 3, 4, 5, 2)).reshape(depth, bs, 2 * GROUP_W, win)
              for c, (win, _) in zip((cache_kv_w128, cache_kv_w512, cache_kv_w2048), DSWA_GROUPS)]
    state = state_hgrn.reshape(depth, bs, C_WIDTH, 64)

    xp = x_prompt.reshape(bp * tp, d)
    xs = x_sample.reshape(bs, d)
    kv_p, kv_s = None, None
    conv_p, hgrn_p, conv_s, hgrn_s = [], [], [], []
    for l in range(depth):
        w_a, w_b, w_c, w_g = (w_in[l, :, c0:c1].astype(BF16) for c0, c1 in
                              ((0, COL_B), (COL_B, COL_C), (COL_C, COL_G), (COL_G, IN_COLS)))
        wpa, wpb, wpc, wo = (w[l].astype(BF16) for w in (w_pa, w_pb, w_pc, w_o))
        wup, wdn = w_up[l].astype(BF16), w_down[l].astype(BF16)
        final = l == depth - 1

        q, k, v, zc = _inproj(xp, norm1[l], w_a, w_c, cos_p, sin_p, lower[l], hgrn_norm[l], tm=2 * ROW_TILE,
                              chunked=True)
        ya, *kv_p = _attn_prompt(q.reshape(3, bp, tp, -1), k.reshape(3, bp, tp, -1), v.reshape(3, bp, tp, -1),
                                 l, kv_p)
        yc, st = _hgrn_prompt(zc.reshape(bp, tp, -1))
        x1, nc = _merge_prompt(xp.reshape(bp, tp, d), ya, yc, norm1[l], w_b, w_g,
                               b_gate[l], conv_w[l], wpa, wpb, wpc, wo)
        xp = _mlp(x1.reshape(bp * tp, d), norm2[l], wup, wdn, norm_f, final, tm=ROW_TILE)
        conv_p.append(nc)
        hgrn_p.append(st)

        q, k, v, zc = _inproj(xs, norm1[l], w_a, w_c, cos_s, sin_s, lower[l], hgrn_norm[l], tm=bs, chunked=False)
        ya, yc, s_new, *kv_s = _sample_mix(q, k, v, caches, zc, state, l, kv_s)
        x1, u = _merge_sample(xs, ya.reshape(bs, GROUP_W), yc.reshape(bs, C_WIDTH), norm1[l], w_b, w_g,
                              state_conv[l, :, 0], state_conv[l, :, 1], b_gate[l], conv_w[l], wpa, wpb, wpc, wo)
        xs = _mlp(x1, norm2[l], wup, wdn, norm_f, final, tm=bs)
        conv_s.append(jnp.stack([state_conv[l, :, 1], u], axis=1))
        hgrn_s.append(s_new.reshape(bs, C_HEADS, 64, 64))

    def kv6(c, n):
        return jnp.transpose(c.reshape(depth, n, 2, 2, HEAD_DIM, c.shape[3]), (0, 1, 5, 2, 3, 4))

    return (xp.reshape(bp, tp, d), xs.reshape(bs, ts, d),
            kv6(kv_p[0], bp), kv6(kv_p[1], bp), kv6(kv_p[2], bp), jnp.stack(conv_p), jnp.stack(hgrn_p),
            kv6(kv_s[0], bs), kv6(kv_s[1], bs), kv6(kv_s[2], bs), jnp.stack(conv_s), jnp.stack(hgrn_s))
```

```python
import functools

import jax
import jax.numpy as jnp
from jax import lax
from jax.experimental import pallas as pl
from jax.experimental.pallas import tpu as pltpu

F32 = jnp.float32
BF16 = jnp.bfloat16

D_MODEL = 1024
HEAD_DIM = 64
DSWA_GROUPS = ((128, 1), (512, 4), (2048, 16))
BAND = 128
GROUP_W = 2 * HEAD_DIM
A_WIDTH = 3 * GROUP_W
CONV_WIDTH = 384
CONV_K = 3
C_HEADS = 4
C_WIDTH = C_HEADS * 64
N_BRANCH = 3
D_FF = 4 * D_MODEL
ROPE_THETA = 10000.0
EPS = 1e-6
NEG_BIG = -1e30
LB_FLOOR = 1e-30
PAST_LEN = 8192
IN_COLS = 3 * A_WIDTH + 3 * CONV_WIDTH + 4 * C_WIDTH + N_BRANCH * D_MODEL
COL_B = 3 * A_WIDTH
COL_C = COL_B + 3 * CONV_WIDTH
COL_G = COL_C + 4 * C_WIDTH
CHUNK = 64
SUB = 8
LEVELS = (32, 16, 8)
VMEM_LIMIT = 56 * 1024 * 1024
ROW_TILE = 512
SUB_ROWS = 256


def _params(n_grid):
    return pltpu.CompilerParams(dimension_semantics=("arbitrary",) * n_grid,
                                vmem_limit_bytes=VMEM_LIMIT)


def _const_spec(shape):
    nd = len(shape)
    return pl.BlockSpec(shape, lambda *_: (0,) * nd, pipeline_mode=pl.Buffered(1))


def _dot(a, b):
    return jnp.dot(a.astype(BF16), b.astype(BF16), preferred_element_type=F32)


def _dot_nt(a, b):
    return lax.dot_general(a.astype(BF16), b.astype(BF16), (((1,), (1,)), ((), ())),
                           preferred_element_type=F32)


def _dot_tn(a, b):
    return lax.dot_general(a.astype(BF16), b.astype(BF16), (((0,), (0,)), ((), ())),
                           preferred_element_type=F32)


def _rmsnorm(x, g):
    return x * lax.rsqrt(jnp.mean(x * x, axis=-1, keepdims=True) + EPS) * g


def _inproj_body(x_ref, g_ref, wa_ref, wc_ref, cos_ref, sin_ref, lb_ref, gn_ref, q_ref, k_ref, v_ref, zc_ref,
                 *, chunked):
    tm = x_ref.shape[0]
    sub = min(tm, SUB_ROWS)
    lane = lax.broadcasted_iota(jnp.int32, (sub, GROUP_W), 1)
    first_half = (lane % HEAD_DIM) < HEAD_DIM // 2

    for r0 in range(0, tm, sub):
        rows = slice(r0, r0 + sub)
        h = _rmsnorm(x_ref[rows, :], g_ref[...]).astype(BF16)
        cos = cos_ref[rows, :]
        sin = sin_ref[rows, :]
        za = jnp.dot(h, wa_ref[...], preferred_element_type=F32)
        for out_ref, c0 in ((q_ref, 0), (k_ref, A_WIDTH)):
            for g in range(3):
                z = za[:, c0 + g * GROUP_W:c0 + (g + 1) * GROUP_W]
                partner = jnp.where(first_half, pltpu.roll(z, GROUP_W - HEAD_DIM // 2, 1),
                                    pltpu.roll(z, HEAD_DIM // 2, 1))
                out_ref[g, rows, :] = z * cos + partner * sin
        for g in range(3):
            v_ref[g, rows, :] = za[:, 2 * A_WIDTH + g * GROUP_W:2 * A_WIDTH + (g + 1) * GROUP_W]
        parts = _hgrn_inputs(jnp.dot(h, wc_ref[...], preferred_element_type=F32), lb_ref[...], gn_ref[...],
                             chunked)
        for i, part in enumerate(parts):
            zc_ref[rows, i * C_WIDTH:(i + 1) * C_WIDTH] = part


def _inproj(x2d, norm, w_a, w_c, cos, sin, lb, gnorm, tm, chunked):
    m = x2d.shape[0]
    assert not chunked or tm % CHUNK == 0
    n_pos = cos.shape[0] // tm
    row = lambda w: pl.BlockSpec((tm, w), lambda i: (i, 0))
    tab = pl.BlockSpec((tm, GROUP_W), lambda i: (i % n_pos, 0))
    grp = pl.BlockSpec((3, tm, GROUP_W), lambda i: (0, i, 0))
    return pl.pallas_call(
        functools.partial(_inproj_body, chunked=chunked),
        grid=(m // tm,),
        in_specs=[row(D_MODEL), _const_spec((1, D_MODEL)), _const_spec(w_a.shape), _const_spec(w_c.shape),
                  tab, tab, _const_spec((1, C_WIDTH)), _const_spec((1, C_WIDTH))],
        out_specs=[grp] * 3 + [row(5 * C_WIDTH)],
        out_shape=[jax.ShapeDtypeStruct((3, m, GROUP_W), F32)] * 3
        + [jax.ShapeDtypeStruct((m, 5 * C_WIDTH), F32)],
        compiler_params=_params(1),
        name="inproj",
    )(x2d, norm.reshape(1, D_MODEL), w_a, w_c, cos, sin, lb.reshape(1, C_WIDTH), gnorm.reshape(1, C_WIDTH))


def _rope_tables(pos):
    half = HEAD_DIM // 2
    inv = ROPE_THETA ** (-jnp.arange(half, dtype=F32) / half)
    ang = pos.astype(F32)[:, None] * inv[None, :]
    cos, sin = jnp.cos(ang), jnp.sin(ang)
    return jnp.tile(cos, (1, 4)), jnp.tile(jnp.concatenate([-sin, sin], axis=1), (1, 2))


def _attend(tiles):
    head0 = lax.broadcasted_iota(jnp.int32, (BAND, GROUP_W), 1) < HEAD_DIM
    scale = HEAD_DIM ** -0.5
    q2 = [jnp.concatenate([jnp.where(head0, q, 0.0), jnp.where(head0, 0.0, q)], axis=0).astype(BF16)
          for q, _ in tiles]
    s = [[jnp.where(valid, _dot_nt(qq, k) * scale, NEG_BIG) for k, _, valid in blocks]
         for qq, (_, blocks) in zip(q2, tiles)]
    m = [jnp.max(functools.reduce(jnp.maximum, x), axis=-1, keepdims=True) for x in s]
    p = [[jnp.exp(x - mm).astype(BF16) for x in ss] for ss, mm in zip(s, m)]
    ext = [functools.reduce(jnp.add, [jnp.dot(pp, v1, preferred_element_type=F32)
                                      for pp, (_, v1, _) in zip(pt, blocks)])
           for pt, (_, blocks) in zip(p, tiles)]
    results = []
    for e, mm in zip(ext, m):
        den = e[:, GROUP_W:]
        o2 = e[:, 0:GROUP_W] / den
        lse2 = mm + jnp.log(den)
        results.append((jnp.where(head0, o2[0:BAND], o2[BAND:]), jnp.where(head0, lse2[0:BAND], lse2[BAND:])))
    return results


def _attn_prompt_body(q_ref, k_ref, v_ref, ya_ref, c0_ref, c1_ref, c2_ref, o_scr, lse_scr, *, seq):
    qi = lax.broadcasted_iota(jnp.int32, (2 * BAND, BAND), 0) % BAND
    kj = lax.broadcasted_iota(jnp.int32, (2 * BAND, BAND), 1)
    cur_valid = kj <= qi
    prev_valid = kj >= qi
    ones = jnp.ones((BAND, GROUP_W), BF16)

    def rows(start, dil):
        return pl.ds(start, BAND, stride=dil) if dil > 1 else pl.ds(start, BAND)

    def key_block(g, r):
        return k_ref[g, r, :].astype(BF16), jnp.concatenate([v_ref[g, r, :].astype(BF16), ones], axis=1)

    def run(g, dil, chains):
        tiles, slices = [], []
        for starts, first_prev in chains:
            chain = [rows(s, dil) for s in starts]
            keys = [key_block(g, r) for r in chain]
            for j, r in enumerate(chain):
                blocks = [(*keys[j], cur_valid)]
                if j > 0:
                    blocks.append((*keys[j - 1], prev_valid))
                elif first_prev is not None:
                    blocks.append((*key_block(g, rows(first_prev[0], dil)), prev_valid & first_prev[1]))
                tiles.append((q_ref[g, r, :], blocks))
            slices += chain
        for r, (o, lse) in zip(slices, _attend(tiles)):
            o_scr[g, r, :] = o
            lse_scr[g, r, :] = lse

    for g, (_, dil) in enumerate(DSWA_GROUPS):
        n_blocks = seq // dil // BAND
        if dil == 1:
            def dense(i, carry):
                s0 = pl.multiple_of(4 * i * BAND, BAND)
                prev = pl.multiple_of(jnp.maximum(4 * i - 1, 0) * BAND, BAND)
                run(g, dil, [([s0 + u * BAND for u in range(4)], (prev, i > 0))])
                return carry

            lax.fori_loop(0, n_blocks // 4, dense, 0)
        elif n_blocks > 1:
            def residue(r, carry, g=g, dil=dil, n_blocks=n_blocks):
                run(g, dil, [([r + i * BAND * dil for i in range(n_blocks)], None)])
                return carry

            lax.fori_loop(0, dil, residue, 0)
        else:
            def residues(j, carry, g=g, dil=dil):
                run(g, dil, [([4 * j + u], None) for u in range(4)])
                return carry

            lax.fori_loop(0, dil // 4, residues, 0)

    step = 256

    def merge(i, carry):
        sl = pl.ds(pl.multiple_of(i * step, step), step)
        lse = [lse_scr[g, sl, :] for g in range(3)]
        m = jnp.maximum(jnp.maximum(lse[0], lse[1]), lse[2])
        e = [jnp.exp(x - m) for x in lse]
        acc = e[0] * o_scr[0, sl, :] + e[1] * o_scr[1, sl, :] + e[2] * o_scr[2, sl, :]
        ya_ref[0, sl, :] = acc / (e[0] + e[1] + e[2])
        return carry

    lax.fori_loop(0, seq // step, merge, 0)

    for g, (c_ref, (win, _)) in enumerate(zip((c0_ref, c1_ref, c2_ref), DSWA_GROUPS)):
        for j in range(0, win, 512):
            w = min(512, win - j)
            c_ref[0:GROUP_W, j:j + w] = k_ref[g, seq - win + j:seq - win + j + w, :].T
            c_ref[GROUP_W:2 * GROUP_W, j:j + w] = v_ref[g, seq - win + j:seq - win + j + w, :].T


def _attn_prompt(q, k, v, layer, caches):
    _, bsz, seq, _ = q.shape
    depth = 2
    assert seq % (BAND * 16) == 0 and seq >= 2048
    qkv_spec = pl.BlockSpec((3, None, seq, GROUP_W), lambda b: (0, b, 0, 0))
    cache_specs = [pl.BlockSpec((None, None, 2 * GROUP_W, win), lambda b: (layer, b, 0, 0))
                   for win, _ in DSWA_GROUPS]
    cache_shapes = [jax.ShapeDtypeStruct((depth, bsz, 2 * GROUP_W, win), F32) for win, _ in DSWA_GROUPS]
    in_specs = [qkv_spec] * 3
    args = [q, k, v]
    aliases = {}
    if caches is not None:
        in_specs = in_specs + [pl.BlockSpec(memory_space=pl.ANY)] * 3
        args = args + list(caches)
        aliases = {3: 1, 4: 2, 5: 3}

    def body(*refs):
        ins, rest = refs[:3], refs[3 + (3 if caches is not None else 0):]
        _attn_prompt_body(*ins, *rest, seq=seq)

    return pl.pallas_call(
        body,
        grid=(bsz,),
        in_specs=in_specs,
        out_specs=[pl.BlockSpec((1, seq, GROUP_W), lambda b: (b, 0, 0))] + cache_specs,
        out_shape=[jax.ShapeDtypeStruct((bsz, seq, GROUP_W), F32)] + cache_shapes,
        scratch_shapes=[pltpu.VMEM((3, seq, GROUP_W), F32), pltpu.VMEM((3, seq, GROUP_W), F32)],
        input_output_aliases=aliases,
        compiler_params=_params(1),
        name="attn_prompt",
    )(*args)


def _log_forget_and_kgate(zf, lb):
    log_sig = -(jnp.maximum(-zf, 0.0) + jnp.log1p(jnp.exp(-jnp.abs(zf))))
    a = jnp.log(jnp.maximum(lb, LB_FLOOR))
    c = jnp.log1p(-lb) + log_sig
    log_f = jnp.maximum(a, c) + jnp.log1p(jnp.exp(-jnp.abs(a - c)))
    kgate = (1.0 - lb) * jax.nn.sigmoid(-zf)
    return log_f, kgate


def _silu(x):
    return x * jax.nn.sigmoid(x)


def _split3(x):
    hi = x.astype(BF16)
    r1 = x - hi.astype(F32)
    mid = r1.astype(BF16)
    lo = (r1 - mid.astype(F32)).astype(BF16)
    return hi, mid, lo


def _hgrn_inputs(zc, lb, gnorm, chunked):
    q, zf, v, og = (zc[:, i * C_WIDTH:(i + 1) * C_WIDTH] for i in range(4))
    log_f, k = _log_forget_and_kgate(zf, lb)
    if chunked:
        n = zc.shape[0]
        r = lax.broadcasted_iota(jnp.int32, (n, n), 0)
        c = lax.broadcasted_iota(jnp.int32, (n, n), 1)
        tri = jnp.where((c <= r) & ((r // CHUNK) == (c // CHUNK)), 1.0, 0.0).astype(BF16)
        b = functools.reduce(jnp.add, [jnp.dot(tri, p, preferred_element_type=F32) for p in _split3(log_f)])
    else:
        b = log_f
    return q, b, k, v, gnorm * _silu(og)


def _hgrn_chunk(q, b, k, v, gate, st, cst):
    n = CHUNK
    ones_bd = cst["ones_bd"]

    def head_sums(x):
        return jnp.dot(x.astype(BF16), ones_bd, preferred_element_type=F32)

    tiles = (n // SUB, SUB, C_WIDTH)
    q3, k3, b3 = (x.reshape(tiles) for x in (q, k, b))
    f3 = jnp.exp(jnp.minimum(b3 - pltpu.roll(b3, 1, 1), 0.0))
    ys, decay = [q * k], None
    for d in range(1, SUB):
        decay = f3 if d == 1 else decay * pltpu.roll(f3, d - 1, 1)
        ys.append((q3 * pltpu.roll(k3, d, 1) * decay).reshape(n, C_WIDTH))
    att_d = head_sums(jnp.concatenate(ys, axis=0))
    att = functools.reduce(jnp.add, [att_d[d * n:(d + 1) * n, :] * cst["place"][d] for d in range(SUB)])

    for lvl, m in enumerate(LEVELS):
        ref_q, ref_k = [], []
        for j in range(n // m):
            own = b[j * m:(j + 1) * m, :]
            if j % 2 == 1:
                ref_q.append(jnp.broadcast_to(b[j * m - 1:j * m, :], (m, C_WIDTH)))
                ref_k.append(own)
            else:
                ref_q.append(own)
                ref_k.append(jnp.broadcast_to(b[(j + 1) * m - 1:(j + 1) * m, :], (m, C_WIDTH)))
        upper = cst["upper"][lvl]
        qt = q * jnp.exp(jnp.minimum(b - jnp.concatenate(ref_q, axis=0), 0.0)) * upper
        kt = k * jnp.exp(jnp.minimum(jnp.concatenate(ref_k, axis=0) - b, 0.0)) * (1.0 - upper)
        k_stack = jnp.concatenate([kt.astype(BF16)] * C_HEADS, axis=0) * cst["stack"]
        att = att + _dot_nt(qt, k_stack) * cst["same_block"][lvl]
    v_stack = jnp.concatenate([v.astype(BF16)] * C_HEADS, axis=0) * cst["stack"]
    o = jnp.dot(att.astype(BF16), v_stack, preferred_element_type=F32)

    o = o + _dot_nt(q * jnp.exp(b), st)
    b_last = b[n - 1:n, :]
    k_hat = k * jnp.exp(b_last - b)
    st = st * jnp.exp(b_last) + _dot_tn(v, k_hat) * cst["same_head"]

    sq = o * o
    sq_hi = sq.astype(BF16)
    ms = head_sums(sq_hi) + head_sums(sq - sq_hi.astype(F32))
    o = o * lax.rsqrt(ms * (1.0 / 64) + EPS) * gate
    return o, st


def _hgrn_constants():
    n = CHUNK
    row = lax.broadcasted_iota(jnp.int32, (n, C_WIDTH), 0)
    hr = lax.broadcasted_iota(jnp.int32, (C_WIDTH, C_WIDTH), 0) // 64
    hc = lax.broadcasted_iota(jnp.int32, (C_WIDTH, C_WIDTH), 1) // 64
    same_head = jnp.where(hr == hc, 1.0, 0.0)
    srow = lax.broadcasted_iota(jnp.int32, (n, C_HEADS * n), 0)
    scol = lax.broadcasted_iota(jnp.int32, (n, C_HEADS * n), 1) % n
    return {
        "ones_bd": same_head.astype(BF16),
        "same_head": same_head,
        "stack": same_head.astype(BF16),
        "place": [jnp.where((scol == srow - d) & (srow % SUB >= d), 1.0, 0.0) for d in range(SUB)],
        "upper": [jnp.where((row // m) % 2 == 1, 1.0, 0.0) for m in LEVELS],
        "same_block": [jnp.where(srow // (2 * m) == scol // (2 * m), 1.0, 0.0) for m in LEVELS],
    }


def _hgrn_prompt_body(zc_ref, yc_ref, st_ref, st_scr, *, rows, n_seq):
    t = pl.program_id(1)

    @pl.when(t == 0)
    def _():
        st_scr[...] = jnp.zeros_like(st_scr)

    cst = _hgrn_constants()

    def chunk(c, carry):
        sl = pl.ds(pl.multiple_of(c * CHUNK, CHUNK), CHUNK)
        for i in range(n_seq):
            o, st = _hgrn_chunk(*(zc_ref[i, sl, j * C_WIDTH:(j + 1) * C_WIDTH] for j in range(5)),
                                st_scr[i], cst)
            st_scr[i] = st
            yc_ref[i, sl, :] = o
        return carry

    lax.fori_loop(0, rows // CHUNK, chunk, 0)

    @pl.when(t == pl.num_programs(1) - 1)
    def _():
        for i in range(n_seq):
            for h in range(C_HEADS):
                st_ref[i, h] = st_scr[i, h * 64:(h + 1) * 64, h * 64:(h + 1) * 64]


def _hgrn_prompt(zc, rows=ROW_TILE):
    bsz, seq, _ = zc.shape
    n_seq = max(n for n in (4, 2, 1) if bsz % n == 0)
    assert CHUNK == 64
    yc, st = pl.pallas_call(
        functools.partial(_hgrn_prompt_body, rows=rows, n_seq=n_seq),
        grid=(bsz // n_seq, seq // rows),
        in_specs=[pl.BlockSpec((n_seq, rows, 5 * C_WIDTH), lambda b, t: (b, t, 0))],
        out_specs=[pl.BlockSpec((n_seq, rows, C_WIDTH), lambda b, t: (b, t, 0)),
                   pl.BlockSpec((n_seq, C_HEADS, 64, 64), lambda b, t: (b, 0, 0, 0))],
        out_shape=[jax.ShapeDtypeStruct((bsz, seq, C_WIDTH), F32),
                   jax.ShapeDtypeStruct((bsz, C_HEADS, 64, 64), F32)],
        scratch_shapes=[pltpu.VMEM((n_seq, C_WIDTH, C_WIDTH), F32)],
        compiler_params=_params(2),
        name="hgrn_prompt",
    )(zc)
    return yc, jnp.swapaxes(st, -1, -2)


def _gated_merge(x, ya, yb, yc, zg, bg, wpa, wpb, wpc, wo):
    g = jax.nn.sigmoid(zg + bg)
    mix = (g[:, 0:D_MODEL] * _dot(ya, wpa) + g[:, D_MODEL:2 * D_MODEL] * _dot(yb, wpb)
           + g[:, 2 * D_MODEL:3 * D_MODEL] * _dot(yc, wpc))
    return x + _dot(mix, wo)


def _project_conv_and_gates(x, g1, wb, wg):
    h = _rmsnorm(x, g1).astype(BF16)
    return jnp.dot(h, wb, preferred_element_type=F32), jnp.dot(h, wg, preferred_element_type=F32)


def _merge_prompt_body(x_ref, ya_ref, yc_ref, g1_ref, wb_ref, wg_ref, bg_ref, cw_ref, wpa_ref, wpb_ref, wpc_ref,
                       wo_ref, x1_ref, nc_ref, u_scr, *, rows):
    @pl.when(pl.program_id(1) == 0)
    def _():
        u_scr[0:8, :] = jnp.zeros((8, CONV_WIDTH), F32)

    zb, zg = _project_conv_and_gates(x_ref[0], g1_ref[...], wb_ref[...], wg_ref[...])
    u = zb[:, 2 * CONV_WIDTH:3 * CONV_WIDTH] * zb[:, 0:CONV_WIDTH]
    u_scr[8:8 + rows, :] = u
    conv = (cw_ref[0:1, :] * u_scr[pl.ds(6, rows), :] + cw_ref[1:2, :] * u_scr[pl.ds(7, rows), :]
            + cw_ref[2:3, :] * u)
    yb = zb[:, CONV_WIDTH:2 * CONV_WIDTH] * conv
    u_scr[0:8, :] = u_scr[rows:rows + 8, :]
    nc_ref[0] = u[rows - (CONV_K - 1):rows, :]
    x1_ref[0] = _gated_merge(x_ref[0], ya_ref[0], yb, yc_ref[0], zg, bg_ref[...],
                             wpa_ref[...], wpb_ref[...], wpc_ref[...], wo_ref[...])


def _merge_prompt(x, ya, yc, norm1, w_b, w_g, b_gate, conv_w, wpa, wpb, wpc, wo, rows=ROW_TILE):
    bsz, seq, _ = x.shape
    blk = lambda w: pl.BlockSpec((1, rows, w), lambda b, t: (b, t, 0))
    return pl.pallas_call(
        functools.partial(_merge_prompt_body, rows=rows),
        grid=(bsz, seq // rows),
        in_specs=[blk(D_MODEL), blk(GROUP_W), blk(C_WIDTH),
                  _const_spec((1, D_MODEL)), _const_spec(w_b.shape), _const_spec(w_g.shape),
                  _const_spec((1, N_BRANCH * D_MODEL)), _const_spec((CONV_K, CONV_WIDTH)),
                  _const_spec((GROUP_W, D_MODEL)), _const_spec((CONV_WIDTH, D_MODEL)),
                  _const_spec((C_WIDTH, D_MODEL)), _const_spec((D_MODEL, D_MODEL))],
        out_specs=[blk(D_MODEL), pl.BlockSpec((1, CONV_K - 1, CONV_WIDTH), lambda b, t: (b, 0, 0))],
        out_shape=[jax.ShapeDtypeStruct((bsz, seq, D_MODEL), F32),
                   jax.ShapeDtypeStruct((bsz, CONV_K - 1, CONV_WIDTH), F32)],
        scratch_shapes=[pltpu.VMEM((rows + 8, CONV_WIDTH), F32)],
        compiler_params=_params(2),
        name="merge_prompt",
    )(x, ya, yc, norm1.reshape(1, D_MODEL), w_b, w_g, b_gate.reshape(1, -1), conv_w, wpa, wpb, wpc, wo)


def _merge_sample_body(x_ref, ya_ref, yc_ref, g1_ref, wb_ref, wg_ref, p0_ref, p1_ref, bg_ref, cw_ref, wpa_ref,
                       wpb_ref, wpc_ref, wo_ref, x1_ref, u_ref):
    zb, zg = _project_conv_and_gates(x_ref[...], g1_ref[...], wb_ref[...], wg_ref[...])
    u = zb[:, 2 * CONV_WIDTH:3 * CONV_WIDTH] * zb[:, 0:CONV_WIDTH]
    conv = cw_ref[0:1, :] * p0_ref[...] + cw_ref[1:2, :] * p1_ref[...] + cw_ref[2:3, :] * u
    yb = zb[:, CONV_WIDTH:2 * CONV_WIDTH] * conv
    u_ref[...] = u
    x1_ref[...] = _gated_merge(x_ref[...], ya_ref[...], yb, yc_ref[...], zg, bg_ref[...],
                               wpa_ref[...], wpb_ref[...], wpc_ref[...], wo_ref[...])


def _merge_sample(x, ya, yc, norm1, w_b, w_g, prev0, prev1, b_gate, conv_w, wpa, wpb, wpc, wo):
    m = x.shape[0]
    args = (x, ya, yc, norm1.reshape(1, D_MODEL), w_b, w_g, prev0, prev1, b_gate.reshape(1, -1), conv_w,
            wpa, wpb, wpc, wo)
    return pl.pallas_call(
        _merge_sample_body,
        grid=(1,),
        in_specs=[_const_spec(a.shape) for a in args],
        out_specs=[_const_spec((m, D_MODEL)), _const_spec((m, CONV_WIDTH))],
        out_shape=[jax.ShapeDtypeStruct((m, D_MODEL), F32), jax.ShapeDtypeStruct((m, CONV_WIDTH), F32)],
        compiler_params=_params(1),
        name="merge_sample",
    )(*args)


def _mlp_body(x_ref, g2_ref, wup_ref, wdn_ref, gf_ref, out_ref, *, final):
    x = x_ref[...]
    h = _rmsnorm(x, g2_ref[...]).astype(BF16)
    acc = x
    for c in range(D_FF // D_MODEL):
        cols = slice(c * D_MODEL, (c + 1) * D_MODEL)
        a = jnp.maximum(jnp.dot(h, wup_ref[:, cols], preferred_element_type=F32), 0.0)
        acc = acc + jnp.dot((a * a).astype(BF16), wdn_ref[cols, :], preferred_element_type=F32)
    out_ref[...] = _rmsnorm(acc, gf_ref[...]) if final else acc


def _mlp(x2d, norm2, wup, wdn, norm_f, final, tm):
    m = x2d.shape[0]
    row = pl.BlockSpec((tm, D_MODEL), lambda i: (i, 0))
    return pl.pallas_call(
        functools.partial(_mlp_body, final=final),
        grid=(m // tm,),
        in_specs=[row, _const_spec((1, D_MODEL)), _const_spec((D_MODEL, D_FF)), _const_spec((D_FF, D_MODEL)),
                  _const_spec((1, D_MODEL))],
        out_specs=row,
        out_shape=jax.ShapeDtypeStruct((m, D_MODEL), F32),
        compiler_params=_params(1),
        name="mlp",
    )(x2d, norm2.reshape(1, D_MODEL), wup, wdn, norm_f.reshape(1, D_MODEL))


def _sample_mix_body(q_ref, k_ref, v_ref, kvcol_ref, c0_ref, c1_ref, c2_ref, zrow_ref, zcol_ref,
                     s_ref, ya_ref, yc_ref, s_out_ref, n0_ref, n1_ref, n2_ref):
    lane = lax.broadcasted_iota(jnp.int32, (8, GROUP_W), 1)
    rowi = lax.broadcasted_iota(jnp.int32, (8, GROUP_W), 0)
    head_of_row = (lane // HEAD_DIM) == rowi
    head0 = lax.broadcasted_iota(jnp.int32, (1, GROUP_W), 1) < HEAD_DIM

    outs, lses = [], []
    for g, (c_ref, n_ref, (win, dil)) in enumerate(zip((c0_ref, c1_ref, c2_ref), (n0_ref, n1_ref, n2_ref),
                                                        DSWA_GROUPS)):
        q, k_new, v_new = q_ref[g], k_ref[g], v_ref[g]
        buf = c_ref[...]
        pos = lax.broadcasted_iota(jnp.int32, (8, win), 1)
        q2 = jnp.where(head_of_row, jnp.broadcast_to(q, (8, GROUP_W)), 0.0)
        scale = HEAD_DIM ** -0.5
        s = jnp.where(pos % dil == 0, _dot(q2, buf[0:GROUP_W, :]) * scale, NEG_BIG)
        s_self = jnp.sum(q2 * k_new, axis=-1, keepdims=True) * scale
        m = jnp.maximum(jnp.max(s, axis=-1, keepdims=True), s_self)
        p, p_self = jnp.exp(s - m), jnp.exp(s_self - m)
        l = jnp.sum(p, axis=-1, keepdims=True) + p_self
        o = (_dot_nt(p, buf[GROUP_W:2 * GROUP_W, :]) + p_self * v_new) / l
        lse = m + jnp.log(l)
        outs.append(jnp.where(head0, o[0:1], o[1:2]))
        lses.append(jnp.where(head0, jnp.broadcast_to(lse[0:1], (1, GROUP_W)),
                              jnp.broadcast_to(lse[1:2], (1, GROUP_W))))
        last = lax.broadcasted_iota(jnp.int32, buf.shape, 1) == win - 1
        n_ref[...] = jnp.where(last, kvcol_ref[g], pltpu.roll(buf, win - 1, 1))
    m = jnp.maximum(jnp.maximum(lses[0], lses[1]), lses[2])
    e = [jnp.exp(x - m) for x in lses]
    ya_ref[0] = (e[0] * outs[0] + e[1] * outs[1] + e[2] * outs[2]) / (e[0] + e[1] + e[2])

    zrow = zrow_ref[0]
    q_col = zcol_ref[0, 0:C_WIDTH, :]
    log_f = zcol_ref[0, C_WIDTH:2 * C_WIDTH, :]
    k_col = zcol_ref[0, 2 * C_WIDTH:3 * C_WIDTH, :]
    v_row = zrow[:, 3 * C_WIDTH:4 * C_WIDTH]
    gate_row = zrow[:, 4 * C_WIDTH:5 * C_WIDTH]
    v_sel = jnp.concatenate([jnp.broadcast_to(v_row[:, h * 64:(h + 1) * 64], (64, 64)) for h in range(C_HEADS)],
                            axis=0)
    s_new = jnp.exp(log_f) * s_ref[...] + k_col * v_sel
    s_out_ref[...] = s_new
    qs = q_col * s_new
    o4 = jnp.concatenate([jnp.sum(qs[h * 64:(h + 1) * 64, :], axis=0, keepdims=True) for h in range(C_HEADS)],
                         axis=0)
    gate4 = jnp.concatenate([gate_row[:, h * 64:(h + 1) * 64] for h in range(C_HEADS)], axis=0)
    o4 = o4 * lax.rsqrt(jnp.mean(o4 * o4, axis=-1, keepdims=True) + EPS) * gate4
    yc_ref[0] = o4


def _sample_mix(q, k, v, caches, zc, state, layer, new_caches):
    n = q.shape[1]
    depth = caches[0].shape[0]
    row3 = lambda w: pl.BlockSpec((1, 1, w), lambda b: (b, 0, 0))
    grp = pl.BlockSpec((3, None, 1, GROUP_W), lambda b: (0, b, 0, 0))
    cache_specs = [pl.BlockSpec((None, None, 2 * GROUP_W, win), lambda b: (layer, b, 0, 0))
                   for win, _ in DSWA_GROUPS]
    in_specs = ([grp] * 3 + [pl.BlockSpec((3, None, 2 * GROUP_W, 1), lambda b: (0, b, 0, 0))] + cache_specs
                + [row3(5 * C_WIDTH), pl.BlockSpec((1, 5 * C_WIDTH, 1), lambda b: (b, 0, 0)),
                   pl.BlockSpec((None, None, C_WIDTH, 64), lambda b: (layer, b, 0, 0))])
    kv_col = jnp.concatenate([k, v], axis=-1).reshape(3, n, 2 * GROUP_W, 1)
    args = [q.reshape(3, n, 1, GROUP_W), k.reshape(3, n, 1, GROUP_W), v.reshape(3, n, 1, GROUP_W), kv_col,
            *caches, zc.reshape(n, 1, 5 * C_WIDTH), zc.reshape(n, 5 * C_WIDTH, 1), state]
    aliases = {}
    n_extra = 0
    if new_caches is not None:
        n_extra = 3
        aliases = {len(args) + i: 3 + i for i in range(3)}
        in_specs = in_specs + [pl.BlockSpec(memory_space=pl.ANY)] * 3
        args = args + list(new_caches)

    def body(*refs):
        n_in = len(args) - n_extra
        _sample_mix_body(*refs[:n_in], *refs[n_in + n_extra:])

    return pl.pallas_call(
        body,
        grid=(n,),
        in_specs=in_specs,
        out_specs=[pl.BlockSpec((1, 1, GROUP_W), lambda b: (b, 0, 0)),
                   pl.BlockSpec((1, C_HEADS, 64), lambda b: (b, 0, 0)),
                   pl.BlockSpec((None, C_WIDTH, 64), lambda b: (b, 0, 0))] + cache_specs,
        out_shape=[jax.ShapeDtypeStruct((n, 1, GROUP_W), F32), jax.ShapeDtypeStruct((n, C_HEADS, 64), F32),
                   jax.ShapeDtypeStruct((n, C_WIDTH, 64), F32)]
        + [jax.ShapeDtypeStruct((depth, n, 2 * GROUP_W, win), F32) for win, _ in DSWA_GROUPS],
        input_output_aliases=aliases,
        compiler_params=_params(1),
        name="sample_mix",
    )(*args)


def kernel(x_prompt, x_sample, cache_kv_w128, cache_kv_w512, cache_kv_w2048, state_conv, state_hgrn, w_in, b_gate,
           norm1, conv_w, hgrn_lb, hgrn_norm, w_pa, w_pb, w_pc, w_o, norm2, w_up, w_down, norm_f):
    bp, tp, d = x_prompt.shape
    bs, ts, _ = x_sample.shape
    depth = w_in.shape[0]
    assert d == D_MODEL and ts == 1 and depth == 2
    assert PAST_LEN >= max(win for win, _ in DSWA_GROUPS)

    sm = jax.nn.softmax(hgrn_lb.astype(F32), axis=0)
    lower = jnp.cumsum(sm, axis=0) - sm[0:1]

    cos_p, sin_p = _rope_tables(jnp.arange(tp, dtype=jnp.int32))
    cos_s, sin_s = _rope_tables(jnp.full((bs,), PAST_LEN, dtype=jnp.int32))

    caches = [jnp.transpose(c, (0, 1, 3, 4, 5, 2)).reshape(depth, bs, 2 * GROUP_W, win)
              for c, (win, _) in zip((cache_kv_w128, cache_kv_w512, cache_kv_w2048), DSWA_GROUPS)]
    state = state_hgrn.reshape(depth, bs, C_WIDTH, 64)

    xp = x_prompt.reshape(bp * tp, d)
    xs = x_sample.reshape(bs, d)
    kv_p, kv_s = None, None
    conv_p, hgrn_p, conv_s, hgrn_s = [], [], [], []
    for l in range(depth):
        w_a, w_b, w_c, w_g = (w_in[l, :, c0:c1].astype(BF16) for c0, c1 in
                              ((0, COL_B), (COL_B, COL_C), (COL_C, COL_G), (COL_G, IN_COLS)))
        wpa, wpb, wpc, wo = (w[l].astype(BF16) for w in (w_pa, w_pb, w_pc, w_o))
        wup, wdn = w_up[l].astype(BF16), w_down[l].astype(BF16)
        final = l == depth - 1

        q, k, v, zc = _inproj(xp, norm1[l], w_a, w_c, cos_p, sin_p, lower[l], hgrn_norm[l], tm=2 * ROW_TILE,
                              chunked=True)
        ya, *kv_p = _attn_prompt(q.reshape(3, bp, tp, -1), k.reshape(3, bp, tp, -1), v.reshape(3, bp, tp, -1),
                                 l, kv_p)
        yc, st = _hgrn_prompt(zc.reshape(bp, tp, -1))
        x1, nc = _merge_prompt(xp.reshape(bp, tp, d), ya, yc, norm1[l], w_b, w_g,
                               b_gate[l], conv_w[l], wpa, wpb, wpc, wo)
        xp = _mlp(x1.reshape(bp * tp, d), norm2[l], wup, wdn, norm_f, final, tm=ROW_TILE)
        conv_p.append(nc)
        hgrn_p.append(st)

        q, k, v, zc = _inproj(xs, norm1[l], w_a, w_c, cos_s, sin_s, lower[l], hgrn_norm[l], tm=bs, chunked=False)
        ya, yc, s_new, *kv_s = _sample_mix(q, k, v, caches, zc, state, l, kv_s)
        x1, u = _merge_sample(xs, ya.reshape(bs, GROUP_W), yc.reshape(bs, C_WIDTH), norm1[l], w_b, w_g,
                              state_conv[l, :, 0], state_conv[l, :, 1], b_gate[l], conv_w[l], wpa, wpb, wpc, wo)
        xs = _mlp(x1, norm2[l], wup, wdn, norm_f, final, tm=bs)
        conv_s.append(jnp.stack([state_conv[l, :, 1], u], axis=1))
        hgrn_s.append(s_new.reshape(bs, C_HEADS, 64, 64))

    def kv6(c, n):
        return jnp.transpose(c.reshape(depth, n, 2, 2, HEAD_DIM, c.shape[3]), (0, 1, 5, 2, 3, 4))

    return (xp.reshape(bp, tp, d), xs.reshape(bs, ts, d),
            kv6(kv_p[0], bp), kv6(kv_p[1], bp), kv6(kv_p[2], bp), jnp.stack(conv_p), jnp.stack(hgrn_p),
            kv6(kv_s[0], bs), kv6(kv_s[1], bs), kv6(kv_s[2], bs), jnp.stack(conv_s), jnp.stack(hgrn_s))
```

```python
import functools

import jax
import jax.numpy as jnp
from jax import lax
from jax.experimental import pallas as pl
from jax.experimental.pallas import tpu as pltpu

F32 = jnp.float32
BF16 = jnp.bfloat16

D_MODEL = 1024
HEAD_DIM = 64
DSWA_GROUPS = ((128, 1), (512, 4), (2048, 16))
BAND = 128
GROUP_W = 2 * HEAD_DIM
A_WIDTH = 3 * GROUP_W
CONV_WIDTH = 384
CONV_K = 3
C_HEADS = 4
C_WIDTH = C_HEADS * 64
N_BRANCH = 3
D_FF = 4 * D_MODEL
ROPE_THETA = 10000.0
EPS = 1e-6
NEG_BIG = -1e30
LB_FLOOR = 1e-30
PAST_LEN = 8192
IN_COLS = 3 * A_WIDTH + 3 * CONV_WIDTH + 4 * C_WIDTH + N_BRANCH * D_MODEL
COL_B = 3 * A_WIDTH
COL_C = COL_B + 3 * CONV_WIDTH
COL_G = COL_C + 4 * C_WIDTH
CHUNK = 64
SUB = 8
LEVELS = (32, 16, 8)
VMEM_LIMIT = 56 * 1024 * 1024
ROW_TILE = 512
SUB_ROWS = 256


def _params(n_grid):
    return pltpu.CompilerParams(dimension_semantics=("arbitrary",) * n_grid,
                                vmem_limit_bytes=VMEM_LIMIT)


def _const_spec(shape):
    nd = len(shape)
    return pl.BlockSpec(shape, lambda *_: (0,) * nd, pipeline_mode=pl.Buffered(1))


def _layer_spec(stacked, layer):
    zeros = (0,) * (stacked.ndim - 1)
    return pl.BlockSpec((None,) + stacked.shape[1:], lambda *_: (layer,) + zeros, pipeline_mode=pl.Buffered(1))


def _dot(a, b):
    return jnp.dot(a.astype(BF16), b.astype(BF16), preferred_element_type=F32)


def _dot_nt(a, b):
    return lax.dot_general(a.astype(BF16), b.astype(BF16), (((1,), (1,)), ((), ())),
                           preferred_element_type=F32)


def _dot_tn(a, b):
    return lax.dot_general(a.astype(BF16), b.astype(BF16), (((0,), (0,)), ((), ())),
                           preferred_element_type=F32)


def _rmsnorm(x, g):
    return x * lax.rsqrt(jnp.mean(x * x, axis=-1, keepdims=True) + EPS) * g


def _inproj_body(x_ref, g_ref, w_ref, cos_ref, sin_ref, lb_ref, gn_ref, q_ref, k_ref, v_ref, zc_ref,
                 *, chunked):
    tm = x_ref.shape[0]
    sub = min(tm, SUB_ROWS)
    lane = lax.broadcasted_iota(jnp.int32, (sub, GROUP_W), 1)
    first_half = (lane % HEAD_DIM) < HEAD_DIM // 2

    for r0 in range(0, tm, sub):
        rows = slice(r0, r0 + sub)
        h = _rmsnorm(x_ref[rows, :], g_ref[...]).astype(BF16)
        cos = cos_ref[rows, :]
        sin = sin_ref[rows, :]
        za = jnp.dot(h, w_ref[:, 0:COL_B], preferred_element_type=F32)
        for out_ref, c0 in ((q_ref, 0), (k_ref, A_WIDTH)):
            for g in range(3):
                z = za[:, c0 + g * GROUP_W:c0 + (g + 1) * GROUP_W]
                partner = jnp.where(first_half, pltpu.roll(z, GROUP_W - HEAD_DIM // 2, 1),
                                    pltpu.roll(z, HEAD_DIM // 2, 1))
                out_ref[g, rows, :] = z * cos + partner * sin
        for g in range(3):
            v_ref[g, rows, :] = za[:, 2 * A_WIDTH + g * GROUP_W:2 * A_WIDTH + (g + 1) * GROUP_W]
        parts = _hgrn_inputs(jnp.dot(h, w_ref[:, COL_C:COL_G], preferred_element_type=F32), lb_ref[...],
                             gn_ref[...], chunked)
        for i, part in enumerate(parts):
            zc_ref[rows, i * C_WIDTH:(i + 1) * C_WIDTH] = part


def _inproj(x2d, norm, w_in, layer, cos, sin, lb, gnorm, tm, chunked):
    m = x2d.shape[0]
    assert not chunked or tm % CHUNK == 0
    n_pos = cos.shape[0] // tm
    row = lambda w: pl.BlockSpec((tm, w), lambda i: (i, 0))
    tab = pl.BlockSpec((tm, GROUP_W), lambda i: (i % n_pos, 0))
    grp = pl.BlockSpec((3, tm, GROUP_W), lambda i: (0, i, 0))
    return pl.pallas_call(
        functools.partial(_inproj_body, chunked=chunked),
        grid=(m // tm,),
        in_specs=[row(D_MODEL), _const_spec((1, D_MODEL)), _layer_spec(w_in, layer),
                  tab, tab, _const_spec((1, C_WIDTH)), _const_spec((1, C_WIDTH))],
        out_specs=[grp] * 3 + [row(5 * C_WIDTH)],
        out_shape=[jax.ShapeDtypeStruct((3, m, GROUP_W), F32)] * 3
        + [jax.ShapeDtypeStruct((m, 5 * C_WIDTH), F32)],
        compiler_params=_params(1),
        name="inproj",
    )(x2d, norm.reshape(1, D_MODEL), w_in, cos, sin, lb.reshape(1, C_WIDTH), gnorm.reshape(1, C_WIDTH))


def _rope_tables(pos):
    half = HEAD_DIM // 2
    inv = ROPE_THETA ** (-jnp.arange(half, dtype=F32) / half)
    ang = pos.astype(F32)[:, None] * inv[None, :]
    cos, sin = jnp.cos(ang), jnp.sin(ang)
    return jnp.tile(cos, (1, 4)), jnp.tile(jnp.concatenate([-sin, sin], axis=1), (1, 2))


def _attend(tiles):
    head0 = lax.broadcasted_iota(jnp.int32, (BAND, GROUP_W), 1) < HEAD_DIM
    scale = HEAD_DIM ** -0.5
    q2 = [jnp.concatenate([jnp.where(head0, q, 0.0), jnp.where(head0, 0.0, q)], axis=0).astype(BF16)
          for q, _ in tiles]
    s = [[jnp.where(valid, _dot_nt(qq, k) * scale, NEG_BIG) for k, _, valid in blocks]
         for qq, (_, blocks) in zip(q2, tiles)]
    m = [jnp.max(functools.reduce(jnp.maximum, x), axis=-1, keepdims=True) for x in s]
    p = [[jnp.exp(x - mm).astype(BF16) for x in ss] for ss, mm in zip(s, m)]
    ext = [functools.reduce(jnp.add, [jnp.dot(pp, v1, preferred_element_type=F32)
                                      for pp, (_, v1, _) in zip(pt, blocks)])
           for pt, (_, blocks) in zip(p, tiles)]
    results = []
    for e, mm in zip(ext, m):
        den = e[:, GROUP_W:]
        o2 = e[:, 0:GROUP_W] / den
        lse2 = mm + jnp.log(den)
        results.append((jnp.where(head0, o2[0:BAND], o2[BAND:]), jnp.where(head0, lse2[0:BAND], lse2[BAND:])))
    return results


def _attn_prompt_body(q_ref, k_ref, v_ref, ya_ref, c0_ref, c1_ref, c2_ref, o_scr, lse_scr, *, seq):
    qi = lax.broadcasted_iota(jnp.int32, (2 * BAND, BAND), 0) % BAND
    kj = lax.broadcasted_iota(jnp.int32, (2 * BAND, BAND), 1)
    cur_valid = kj <= qi
    prev_valid = kj >= qi
    ones = jnp.ones((BAND, GROUP_W), BF16)

    def rows(start, dil):
        return pl.ds(start, BAND, stride=dil) if dil > 1 else pl.ds(start, BAND)

    def key_block(g, r):
        return k_ref[g, r, :].astype(BF16), jnp.concatenate([v_ref[g, r, :].astype(BF16), ones], axis=1)

    def run(g, dil, chains):
        tiles, slices = [], []
        for starts, first_prev in chains:
            chain = [rows(s, dil) for s in starts]
            keys = [key_block(g, r) for r in chain]
            for j, r in enumerate(chain):
                blocks = [(*keys[j], cur_valid)]
                if j > 0:
                    blocks.append((*keys[j - 1], prev_valid))
                elif first_prev is not None:
                    blocks.append((*key_block(g, rows(first_prev[0], dil)), prev_valid & first_prev[1]))
                tiles.append((q_ref[g, r, :], blocks))
            slices += chain
        for r, (o, lse) in zip(slices, _attend(tiles)):
            o_scr[g, r, :] = o
            lse_scr[g, r, :] = lse

    for g, (_, dil) in enumerate(DSWA_GROUPS):
        n_blocks = seq // dil // BAND
        if dil == 1:
            def dense(i, carry):
                s0 = pl.multiple_of(4 * i * BAND, BAND)
                prev = pl.multiple_of(jnp.maximum(4 * i - 1, 0) * BAND, BAND)
                run(g, dil, [([s0 + u * BAND for u in range(4)], (prev, i > 0))])
                return carry

            lax.fori_loop(0, n_blocks // 4, dense, 0)
        elif n_blocks > 1:
            def residue(r, carry, g=g, dil=dil, n_blocks=n_blocks):
                run(g, dil, [([r + i * BAND * dil for i in range(n_blocks)], None)])
                return carry

            lax.fori_loop(0, dil, residue, 0)
        else:
            def residues(j, carry, g=g, dil=dil):
                run(g, dil, [([4 * j + u], None) for u in range(4)])
                return carry

            lax.fori_loop(0, dil // 4, residues, 0)

    step = 256

    def merge(i, carry):
        sl = pl.ds(pl.multiple_of(i * step, step), step)
        lse = [lse_scr[g, sl, :] for g in range(3)]
        m = jnp.maximum(jnp.maximum(lse[0], lse[1]), lse[2])
        e = [jnp.exp(x - m) for x in lse]
        acc = e[0] * o_scr[0, sl, :] + e[1] * o_scr[1, sl, :] + e[2] * o_scr[2, sl, :]
        ya_ref[0, sl, :] = acc / (e[0] + e[1] + e[2])
        return carry

    lax.fori_loop(0, seq // step, merge, 0)

    for g, (c_ref, (win, _)) in enumerate(zip((c0_ref, c1_ref, c2_ref), DSWA_GROUPS)):
        for j in range(0, win, 512):
            w = min(512, win - j)
            c_ref[0:GROUP_W, j:j + w] = k_ref[g, seq - win + j:seq - win + j + w, :].T
            c_ref[GROUP_W:2 * GROUP_W, j:j + w] = v_ref[g, seq - win + j:seq - win + j + w, :].T


def _attn_prompt(q, k, v, layer, caches):
    _, bsz, seq, _ = q.shape
    depth = 2
    assert seq % (BAND * 16) == 0 and seq >= 2048
    qkv_spec = pl.BlockSpec((3, None, seq, GROUP_W), lambda b: (0, b, 0, 0))
    cache_specs = [pl.BlockSpec((None, None, 2 * GROUP_W, win), lambda b: (layer, b, 0, 0))
                   for win, _ in DSWA_GROUPS]
    cache_shapes = [jax.ShapeDtypeStruct((depth, bsz, 2 * GROUP_W, win), F32) for win, _ in DSWA_GROUPS]
    in_specs = [qkv_spec] * 3
    args = [q, k, v]
    aliases = {}
    if caches is not None:
        in_specs = in_specs + [pl.BlockSpec(memory_space=pl.ANY)] * 3
        args = args + list(caches)
        aliases = {3: 1, 4: 2, 5: 3}

    def body(*refs):
        ins, rest = refs[:3], refs[3 + (3 if caches is not None else 0):]
        _attn_prompt_body(*ins, *rest, seq=seq)

    return pl.pallas_call(
        body,
        grid=(bsz,),
        in_specs=in_specs,
        out_specs=[pl.BlockSpec((1, seq, GROUP_W), lambda b: (b, 0, 0))] + cache_specs,
        out_shape=[jax.ShapeDtypeStruct((bsz, seq, GROUP_W), F32)] + cache_shapes,
        scratch_shapes=[pltpu.VMEM((3, seq, GROUP_W), F32), pltpu.VMEM((3, seq, GROUP_W), F32)],
        input_output_aliases=aliases,
        compiler_params=_params(1),
        name="attn_prompt",
    )(*args)


def _log_forget_and_kgate(zf, lb):
    log_sig = -(jnp.maximum(-zf, 0.0) + jnp.log1p(jnp.exp(-jnp.abs(zf))))
    a = jnp.log(jnp.maximum(lb, LB_FLOOR))
    c = jnp.log1p(-lb) + log_sig
    log_f = jnp.maximum(a, c) + jnp.log1p(jnp.exp(-jnp.abs(a - c)))
    kgate = (1.0 - lb) * jax.nn.sigmoid(-zf)
    return log_f, kgate


def _silu(x):
    return x * jax.nn.sigmoid(x)


def _split3(x):
    hi = x.astype(BF16)
    r1 = x - hi.astype(F32)
    mid = r1.astype(BF16)
    lo = (r1 - mid.astype(F32)).astype(BF16)
    return hi, mid, lo


def _hgrn_inputs(zc, lb, gnorm, chunked):
    q, zf, v, og = (zc[:, i * C_WIDTH:(i + 1) * C_WIDTH] for i in range(4))
    log_f, k = _log_forget_and_kgate(zf, lb)
    if chunked:
        n = zc.shape[0]
        r = lax.broadcasted_iota(jnp.int32, (n, n), 0)
        c = lax.broadcasted_iota(jnp.int32, (n, n), 1)
        tri = jnp.where((c <= r) & ((r // CHUNK) == (c // CHUNK)), 1.0, 0.0).astype(BF16)
        b = functools.reduce(jnp.add, [jnp.dot(tri, p, preferred_element_type=F32) for p in _split3(log_f)])
    else:
        b = log_f
    return q, b, k, v, gnorm * _silu(og)


def _hgrn_chunk(q, b, k, v, gate, st, cst):
    n = CHUNK
    ones_bd = cst["ones_bd"]

    def head_sums(x):
        return jnp.dot(x.astype(BF16), ones_bd, preferred_element_type=F32)

    tiles = (n // SUB, SUB, C_WIDTH)
    q3, k3, b3 = (x.reshape(tiles) for x in (q, k, b))
    f3 = jnp.exp(jnp.minimum(b3 - pltpu.roll(b3, 1, 1), 0.0))
    ys, decay = [q * k], None
    for d in range(1, SUB):
        decay = f3 if d == 1 else decay * pltpu.roll(f3, d - 1, 1)
        ys.append((q3 * pltpu.roll(k3, d, 1) * decay).reshape(n, C_WIDTH))
    att_d = head_sums(jnp.concatenate(ys, axis=0))
    att = functools.reduce(jnp.add, [att_d[d * n:(d + 1) * n, :] * cst["place"][d] for d in range(SUB)])

    for lvl, m in enumerate(LEVELS):
        ref_q, ref_k = [], []
        for j in range(n // m):
            own = b[j * m:(j + 1) * m, :]
            if j % 2 == 1:
                ref_q.append(jnp.broadcast_to(b[j * m - 1:j * m, :], (m, C_WIDTH)))
                ref_k.append(own)
            else:
                ref_q.append(own)
                ref_k.append(jnp.broadcast_to(b[(j + 1) * m - 1:(j + 1) * m, :], (m, C_WIDTH)))
        upper = cst["upper"][lvl]
        qt = q * jnp.exp(jnp.minimum(b - jnp.concatenate(ref_q, axis=0), 0.0)) * upper
        kt = k * jnp.exp(jnp.minimum(jnp.concatenate(ref_k, axis=0) - b, 0.0)) * (1.0 - upper)
        k_stack = jnp.concatenate([kt.astype(BF16)] * C_HEADS, axis=0) * cst["stack"]
        att = att + _dot_nt(qt, k_stack) * cst["same_block"][lvl]
    v_stack = jnp.concatenate([v.astype(BF16)] * C_HEADS, axis=0) * cst["stack"]
    o = jnp.dot(att.astype(BF16), v_stack, preferred_element_type=F32)

    o = o + _dot_nt(q * jnp.exp(b), st)
    b_last = b[n - 1:n, :]
    k_hat = k * jnp.exp(b_last - b)
    st = st * jnp.exp(b_last) + _dot_tn(v, k_hat) * cst["same_head"]

    sq = o * o
    sq_hi = sq.astype(BF16)
    ms = head_sums(sq_hi) + head_sums(sq - sq_hi.astype(F32))
    o = o * lax.rsqrt(ms * (1.0 / 64) + EPS) * gate
    return o, st


def _hgrn_constants():
    n = CHUNK
    row = lax.broadcasted_iota(jnp.int32, (n, C_WIDTH), 0)
    hr = lax.broadcasted_iota(jnp.int32, (C_WIDTH, C_WIDTH), 0) // 64
    hc = lax.broadcasted_iota(jnp.int32, (C_WIDTH, C_WIDTH), 1) // 64
    same_head = jnp.where(hr == hc, 1.0, 0.0)
    srow = lax.broadcasted_iota(jnp.int32, (n, C_HEADS * n), 0)
    scol = lax.broadcasted_iota(jnp.int32, (n, C_HEADS * n), 1) % n
    return {
        "ones_bd": same_head.astype(BF16),
        "same_head": same_head,
        "stack": same_head.astype(BF16),
        "place": [jnp.where((scol == srow - d) & (srow % SUB >= d), 1.0, 0.0) for d in range(SUB)],
        "upper": [jnp.where((row // m) % 2 == 1, 1.0, 0.0) for m in LEVELS],
        "same_block": [jnp.where(srow // (2 * m) == scol // (2 * m), 1.0, 0.0) for m in LEVELS],
    }


def _hgrn_prompt_body(zc_ref, yc_ref, st_ref, st_scr, *, rows, n_seq):
    t = pl.program_id(1)

    @pl.when(t == 0)
    def _():
        st_scr[...] = jnp.zeros_like(st_scr)

    cst = _hgrn_constants()

    def chunk(c, carry):
        sl = pl.ds(pl.multiple_of(c * CHUNK, CHUNK), CHUNK)
        for i in range(n_seq):
            o, st = _hgrn_chunk(*(zc_ref[i, sl, j * C_WIDTH:(j + 1) * C_WIDTH] for j in range(5)),
                                st_scr[i], cst)
            st_scr[i] = st
            yc_ref[i, sl, :] = o
        return carry

    lax.fori_loop(0, rows // CHUNK, chunk, 0)

    @pl.when(t == pl.num_programs(1) - 1)
    def _():
        for i in range(n_seq):
            for h in range(C_HEADS):
                st_ref[i, h] = st_scr[i, h * 64:(h + 1) * 64, h * 64:(h + 1) * 64]


def _hgrn_prompt(zc, rows=ROW_TILE):
    bsz, seq, _ = zc.shape
    n_seq = max(n for n in (4, 2, 1) if bsz % n == 0)
    assert CHUNK == 64
    yc, st = pl.pallas_call(
        functools.partial(_hgrn_prompt_body, rows=rows, n_seq=n_seq),
        grid=(bsz // n_seq, seq // rows),
        in_specs=[pl.BlockSpec((n_seq, rows, 5 * C_WIDTH), lambda b, t: (b, t, 0))],
        out_specs=[pl.BlockSpec((n_seq, rows, C_WIDTH), lambda b, t: (b, t, 0)),
                   pl.BlockSpec((n_seq, C_HEADS, 64, 64), lambda b, t: (b, 0, 0, 0))],
        out_shape=[jax.ShapeDtypeStruct((bsz, seq, C_WIDTH), F32),
                   jax.ShapeDtypeStruct((bsz, C_HEADS, 64, 64), F32)],
        scratch_shapes=[pltpu.VMEM((n_seq, C_WIDTH, C_WIDTH), F32)],
        compiler_params=_params(2),
        name="hgrn_prompt",
    )(zc)
    return yc, jnp.swapaxes(st, -1, -2)


def _gated_merge(x, ya, yb, yc, zg, bg, wpa, wpb, wpc, wo):
    g = jax.nn.sigmoid(zg + bg)
    mix = (g[:, 0:D_MODEL] * _dot(ya, wpa) + g[:, D_MODEL:2 * D_MODEL] * _dot(yb, wpb)
           + g[:, 2 * D_MODEL:3 * D_MODEL] * _dot(yc, wpc))
    return x + _dot(mix, wo)


def _project_conv_and_gates(x, g1, w_ref):
    h = _rmsnorm(x, g1).astype(BF16)
    return (jnp.dot(h, w_ref[:, COL_B:COL_C], preferred_element_type=F32),
            jnp.dot(h, w_ref[:, COL_G:IN_COLS], preferred_element_type=F32))


def _merge_prompt_body(x_ref, ya_ref, yc_ref, g1_ref, w_ref, bg_ref, cw_ref, wpa_ref, wpb_ref, wpc_ref,
                       wo_ref, x1_ref, nc_ref, u_scr, *, rows):
    @pl.when(pl.program_id(1) == 0)
    def _():
        u_scr[0:8, :] = jnp.zeros((8, CONV_WIDTH), F32)

    sub = min(rows, ROW_TILE)
    for r0 in range(0, rows, sub):
        sl = slice(r0, r0 + sub)
        x = x_ref[0, sl, :]
        zb, zg = _project_conv_and_gates(x, g1_ref[...], w_ref)
        u = zb[:, 2 * CONV_WIDTH:3 * CONV_WIDTH] * zb[:, 0:CONV_WIDTH]
        u_scr[8 + r0:8 + r0 + sub, :] = u
        conv = (cw_ref[0:1, :] * u_scr[pl.ds(6 + r0, sub), :] + cw_ref[1:2, :] * u_scr[pl.ds(7 + r0, sub), :]
                + cw_ref[2:3, :] * u)
        yb = zb[:, CONV_WIDTH:2 * CONV_WIDTH] * conv
        x1_ref[0, sl, :] = _gated_merge(x, ya_ref[0, sl, :], yb, yc_ref[0, sl, :], zg, bg_ref[...],
                                        wpa_ref[...], wpb_ref[...], wpc_ref[...], wo_ref[...])
    nc_ref[0] = u_scr[8 + rows - (CONV_K - 1):8 + rows, :]
    u_scr[0:8, :] = u_scr[rows:rows + 8, :]


def _merge_prompt(x, ya, yc, norm1, layer, w_in, b_gate, conv_w, wpa, wpb, wpc, wo, rows=2 * ROW_TILE):
    bsz, seq, _ = x.shape
    blk = lambda w: pl.BlockSpec((1, rows, w), lambda b, t: (b, t, 0))
    return pl.pallas_call(
        functools.partial(_merge_prompt_body, rows=rows),
        grid=(bsz, seq // rows),
        in_specs=[blk(D_MODEL), blk(GROUP_W), blk(C_WIDTH),
                  _const_spec((1, D_MODEL)), _layer_spec(w_in, layer),
                  _const_spec((1, N_BRANCH * D_MODEL)), _const_spec((CONV_K, CONV_WIDTH)),
                  _layer_spec(wpa, layer), _layer_spec(wpb, layer), _layer_spec(wpc, layer), _layer_spec(wo, layer)],
        out_specs=[blk(D_MODEL), pl.BlockSpec((1, CONV_K - 1, CONV_WIDTH), lambda b, t: (b, 0, 0))],
        out_shape=[jax.ShapeDtypeStruct((bsz, seq, D_MODEL), F32),
                   jax.ShapeDtypeStruct((bsz, CONV_K - 1, CONV_WIDTH), F32)],
        scratch_shapes=[pltpu.VMEM((rows + 8, CONV_WIDTH), F32)],
        compiler_params=_params(2),
        name="merge_prompt",
    )(x, ya, yc, norm1.reshape(1, D_MODEL), w_in, b_gate.reshape(1, -1), conv_w, wpa, wpb, wpc, wo)


def _merge_sample_body(x_ref, ya_ref, yc_ref, g1_ref, w_ref, p0_ref, p1_ref, bg_ref, cw_ref, wpa_ref,
                       wpb_ref, wpc_ref, wo_ref, x1_ref, u_ref):
    zb, zg = _project_conv_and_gates(x_ref[...], g1_ref[...], w_ref)
    u = zb[:, 2 * CONV_WIDTH:3 * CONV_WIDTH] * zb[:, 0:CONV_WIDTH]
    conv = cw_ref[0:1, :] * p0_ref[...] + cw_ref[1:2, :] * p1_ref[...] + cw_ref[2:3, :] * u
    yb = zb[:, CONV_WIDTH:2 * CONV_WIDTH] * conv
    u_ref[...] = u
    x1_ref[...] = _gated_merge(x_ref[...], ya_ref[...], yb, yc_ref[...], zg, bg_ref[...],
                               wpa_ref[...], wpb_ref[...], wpc_ref[...], wo_ref[...])


def _merge_sample(x, ya, yc, norm1, layer, w_in, prev0, prev1, b_gate, conv_w, wpa, wpb, wpc, wo):
    m = x.shape[0]
    small = (x, ya, yc, norm1.reshape(1, D_MODEL))
    mid = (prev0, prev1, b_gate.reshape(1, -1), conv_w)
    stacks = (wpa, wpb, wpc, wo)
    args = small + (w_in,) + mid + stacks
    return pl.pallas_call(
        _merge_sample_body,
        grid=(1,),
        in_specs=([_const_spec(a.shape) for a in small] + [_layer_spec(w_in, layer)]
                  + [_const_spec(a.shape) for a in mid] + [_layer_spec(w, layer) for w in stacks]),
        out_specs=[_const_spec((m, D_MODEL)), _const_spec((m, CONV_WIDTH))],
        out_shape=[jax.ShapeDtypeStruct((m, D_MODEL), F32), jax.ShapeDtypeStruct((m, CONV_WIDTH), F32)],
        compiler_params=_params(1),
        name="merge_sample",
    )(*args)


def _mlp_body(x_ref, g2_ref, wup_ref, wdn_ref, gf_ref, out_ref, *, final):
    x = x_ref[...]
    h = _rmsnorm(x, g2_ref[...]).astype(BF16)
    acc = x
    for c in range(D_FF // D_MODEL):
        cols = slice(c * D_MODEL, (c + 1) * D_MODEL)
        a = jnp.maximum(jnp.dot(h, wup_ref[:, cols], preferred_element_type=F32), 0.0)
        acc = acc + jnp.dot((a * a).astype(BF16), wdn_ref[cols, :], preferred_element_type=F32)
    out_ref[...] = _rmsnorm(acc, gf_ref[...]) if final else acc


def _mlp(x2d, norm2, layer, wup, wdn, norm_f, final, tm):
    m = x2d.shape[0]
    row = pl.BlockSpec((tm, D_MODEL), lambda i: (i, 0))
    return pl.pallas_call(
        functools.partial(_mlp_body, final=final),
        grid=(m // tm,),
        in_specs=[row, _const_spec((1, D_MODEL)), _layer_spec(wup, layer), _layer_spec(wdn, layer),
                  _const_spec((1, D_MODEL))],
        out_specs=row,
        out_shape=jax.ShapeDtypeStruct((m, D_MODEL), F32),
        compiler_params=_params(1),
        name="mlp",
    )(x2d, norm2.reshape(1, D_MODEL), wup, wdn, norm_f.reshape(1, D_MODEL))


def _as_column(row):
    return jnp.concatenate([jnp.broadcast_to(row[:, c:c + 128], (128, 128)).T for c in range(0, row.shape[1], 128)],
                           axis=0)


def _sample_mix_body(q_ref, k_ref, v_ref, c0_ref, c1_ref, c2_ref, zrow_ref,
                     s_ref, ya_ref, yc_ref, s_out_ref, n0_ref, n1_ref, n2_ref):
    lane = lax.broadcasted_iota(jnp.int32, (8, GROUP_W), 1)
    rowi = lax.broadcasted_iota(jnp.int32, (8, GROUP_W), 0)
    head_of_row = (lane // HEAD_DIM) == rowi
    head0 = lax.broadcasted_iota(jnp.int32, (1, GROUP_W), 1) < HEAD_DIM

    outs, lses = [], []
    for g, (c_ref, n_ref, (win, dil)) in enumerate(zip((c0_ref, c1_ref, c2_ref), (n0_ref, n1_ref, n2_ref),
                                                        DSWA_GROUPS)):
        q, k_new, v_new = q_ref[g], k_ref[g], v_ref[g]
        buf = c_ref[...]
        pos = lax.broadcasted_iota(jnp.int32, (8, win), 1)
        q2 = jnp.where(head_of_row, jnp.broadcast_to(q, (8, GROUP_W)), 0.0)
        scale = HEAD_DIM ** -0.5
        s = jnp.where(pos % dil == 0, _dot(q2, buf[0:GROUP_W, :]) * scale, NEG_BIG)
        s_self = jnp.sum(q2 * k_new, axis=-1, keepdims=True) * scale
        m = jnp.maximum(jnp.max(s, axis=-1, keepdims=True), s_self)
        p, p_self = jnp.exp(s - m), jnp.exp(s_self - m)
        l = jnp.sum(p, axis=-1, keepdims=True) + p_self
        o = (_dot_nt(p, buf[GROUP_W:2 * GROUP_W, :]) + p_self * v_new) / l
        lse = m + jnp.log(l)
        outs.append(jnp.where(head0, o[0:1], o[1:2]))
        lses.append(jnp.where(head0, jnp.broadcast_to(lse[0:1], (1, GROUP_W)),
                              jnp.broadcast_to(lse[1:2], (1, GROUP_W))))
        rolled = pltpu.roll(buf, win - 1, 1)
        new_col = jnp.concatenate([_as_column(k_new), _as_column(v_new)], axis=0)
        last = lax.broadcasted_iota(jnp.int32, new_col.shape, 1) == 127
        if win > 128:
            n_ref[:, 0:win - 128] = rolled[:, 0:win - 128]
        n_ref[:, win - 128:win] = jnp.where(last, new_col, rolled[:, win - 128:win])
    m = jnp.maximum(jnp.maximum(lses[0], lses[1]), lses[2])
    e = [jnp.exp(x - m) for x in lses]
    ya_ref[0] = (e[0] * outs[0] + e[1] * outs[1] + e[2] * outs[2]) / (e[0] + e[1] + e[2])

    zrow = zrow_ref[0]
    q_col, log_f, k_col = (_as_column(zrow[:, j * C_WIDTH:(j + 1) * C_WIDTH])[:, 0:64] for j in range(3))
    v_row = zrow[:, 3 * C_WIDTH:4 * C_WIDTH]
    gate_row = zrow[:, 4 * C_WIDTH:5 * C_WIDTH]
    v_sel = jnp.concatenate([jnp.broadcast_to(v_row[:, h * 64:(h + 1) * 64], (64, 64)) for h in range(C_HEADS)],
                            axis=0)
    s_new = jnp.exp(log_f) * s_ref[...] + k_col * v_sel
    s_out_ref[...] = s_new
    qs = q_col * s_new
    o4 = jnp.concatenate([jnp.sum(qs[h * 64:(h + 1) * 64, :], axis=0, keepdims=True) for h in range(C_HEADS)],
                         axis=0)
    gate4 = jnp.concatenate([gate_row[:, h * 64:(h + 1) * 64] for h in range(C_HEADS)], axis=0)
    o4 = o4 * lax.rsqrt(jnp.mean(o4 * o4, axis=-1, keepdims=True) + EPS) * gate4
    yc_ref[0] = o4


def _sample_mix(q, k, v, caches, zc, state, layer, new_caches):
    n = q.shape[1]
    depth = caches[0].shape[0]
    row3 = lambda w: pl.BlockSpec((1, 1, w), lambda b: (b, 0, 0))
    grp = pl.BlockSpec((3, None, 1, GROUP_W), lambda b: (0, b, 0, 0))
    cache_specs = [pl.BlockSpec((None, None, 2 * GROUP_W, win), lambda b: (layer, b, 0, 0))
                   for win, _ in DSWA_GROUPS]
    in_specs = ([grp] * 3 + cache_specs
                + [row3(5 * C_WIDTH), pl.BlockSpec((None, None, C_WIDTH, 64), lambda b: (layer, b, 0, 0))])
    args = [q.reshape(3, n, 1, GROUP_W), k.reshape(3, n, 1, GROUP_W), v.reshape(3, n, 1, GROUP_W),
            *caches, zc.reshape(n, 1, 5 * C_WIDTH), state]
    aliases = {}
    n_extra = 0
    if new_caches is not None:
        n_extra = 3
        aliases = {len(args) + i: 3 + i for i in range(3)}
        in_specs = in_specs + [pl.BlockSpec(memory_space=pl.ANY)] * 3
        args = args + list(new_caches)

    def body(*refs):
        n_in = len(args) - n_extra
        _sample_mix_body(*refs[:n_in], *refs[n_in + n_extra:])

    return pl.pallas_call(
        body,
        grid=(n,),
        in_specs=in_specs,
        out_specs=[pl.BlockSpec((1, 1, GROUP_W), lambda b: (b, 0, 0)),
                   pl.BlockSpec((1, C_HEADS, 64), lambda b: (b, 0, 0)),
                   pl.BlockSpec((None, C_WIDTH, 64), lambda b: (b, 0, 0))] + cache_specs,
        out_shape=[jax.ShapeDtypeStruct((n, 1, GROUP_W), F32), jax.ShapeDtypeStruct((n, C_HEADS, 64), F32),
                   jax.ShapeDtypeStruct((n, C_WIDTH, 64), F32)]
        + [jax.ShapeDtypeStruct((depth, n, 2 * GROUP_W, win), F32) for win, _ in DSWA_GROUPS],
        input_output_aliases=aliases,
        compiler_params=_params(1),
        name="sample_mix",
    )(*args)


def kernel(x_prompt, x_sample, cache_kv_w128, cache_kv_w512, cache_kv_w2048, state_conv, state_hgrn, w_in, b_gate,
           norm1, conv_w, hgrn_lb, hgrn_norm, w_pa, w_pb, w_pc, w_o, norm2, w_up, w_down, norm_f):
    bp, tp, d = x_prompt.shape
    bs, ts, _ = x_sample.shape
    depth = w_in.shape[0]
    assert d == D_MODEL and ts == 1 and depth == 2
    assert PAST_LEN >= max(win for win, _ in DSWA_GROUPS)

    sm = jax.nn.softmax(hgrn_lb.astype(F32), axis=0)
    lower = jnp.cumsum(sm, axis=0) - sm[0:1]

    cos_p, sin_p = _rope_tables(jnp.arange(tp, dtype=jnp.int32))
    cos_s, sin_s = _rope_tables(jnp.full((bs,), PAST_LEN, dtype=jnp.int32))

    caches = [jnp.transpose(c, (0, 1, 3, 4, 5, 2)).reshape(depth, bs, 2 * GROUP_W, win)
              for c, (win, _) in zip((cache_kv_w128, cache_kv_w512, cache_kv_w2048), DSWA_GROUPS)]
    state = state_hgrn.reshape(depth, bs, C_WIDTH, 64)

    xp = x_prompt.reshape(bp * tp, d)
    xs = x_sample.reshape(bs, d)
    kv_p, kv_s = None, None
    conv_p, hgrn_p, conv_s, hgrn_s = [], [], [], []
    w_in_bf, wpa, wpb, wpc, wo, wup, wdn = (w.astype(BF16) for w in (w_in, w_pa, w_pb, w_pc, w_o, w_up, w_down))
    for l in range(depth):
        final = l == depth - 1

        q, k, v, zc = _inproj(xp, norm1[l], w_in_bf, l, cos_p, sin_p, lower[l], hgrn_norm[l], tm=2 * ROW_TILE,
                              chunked=True)
        ya, *kv_p = _attn_prompt(q.reshape(3, bp, tp, -1), k.reshape(3, bp, tp, -1), v.reshape(3, bp, tp, -1),
                                 l, kv_p)
        yc, st = _hgrn_prompt(zc.reshape(bp, tp, -1))
        x1, nc = _merge_prompt(xp.reshape(bp, tp, d), ya, yc, norm1[l], l, w_in_bf,
                               b_gate[l], conv_w[l], wpa, wpb, wpc, wo)
        xp = _mlp(x1.reshape(bp * tp, d), norm2[l], l, wup, wdn, norm_f, final, tm=ROW_TILE)
        conv_p.append(nc)
        hgrn_p.append(st)

        q, k, v, zc = _inproj(xs, norm1[l], w_in_bf, l, cos_s, sin_s, lower[l], hgrn_norm[l], tm=bs,
                              chunked=False)
        ya, yc, s_new, *kv_s = _sample_mix(q, k, v, caches, zc, state, l, kv_s)
        x1, u = _merge_sample(xs, ya.reshape(bs, GROUP_W), yc.reshape(bs, C_WIDTH), norm1[l], l, w_in_bf,
                              state_conv[l, :, 0], state_conv[l, :, 1], b_gate[l], conv_w[l], wpa, wpb, wpc, wo)
        xs = _mlp(x1, norm2[l], l, wup, wdn, norm_f, final, tm=bs)
        conv_s.append(jnp.stack([state_conv[l, :, 1], u], axis=1))
        hgrn_s.append(s_new.reshape(bs, C_HEADS, 64, 64))

    def kv6(c, n):
        return jnp.transpose(c.reshape(depth, n, 2, 2, HEAD_DIM, c.shape[3]), (0, 1, 5, 2, 3, 4))

    return (xp.reshape(bp, tp, d), xs.reshape(bs, ts, d),
            kv6(kv_p[0], bp), kv6(kv_p[1], bp), kv6(kv_p[2], bp), jnp.stack(conv_p), jnp.stack(hgrn_p),
            kv6(kv_s[0], bs), kv6(kv_s[1], bs), kv6(kv_s[2], bs), jnp.stack(conv_s), jnp.stack(hgrn_s))
```

```python
import functools

import jax
import jax.numpy as jnp
from jax import lax
from jax.experimental import pallas as pl
from jax.experimental.pallas import tpu as pltpu

F32 = jnp.float32
BF16 = jnp.bfloat16

D_MODEL = 1024
HEAD_DIM = 64
DSWA_GROUPS = ((128, 1), (512, 4), (2048, 16))
BAND = 128
GROUP_W = 2 * HEAD_DIM
A_WIDTH = 3 * GROUP_W
CONV_WIDTH = 384
CONV_K = 3
C_HEADS = 4
C_WIDTH = C_HEADS * 64
N_BRANCH = 3
D_FF = 4 * D_MODEL
ROPE_THETA = 10000.0
EPS = 1e-6
NEG_BIG = -1e30
LB_FLOOR = 1e-30
PAST_LEN = 8192
IN_COLS = 3 * A_WIDTH + 3 * CONV_WIDTH + 4 * C_WIDTH + N_BRANCH * D_MODEL
COL_B = 3 * A_WIDTH
COL_C = COL_B + 3 * CONV_WIDTH
COL_G = COL_C + 4 * C_WIDTH
CHUNK = 64
SUB = 8
LEVELS = (32, 16, 8)
VMEM_LIMIT = 56 * 1024 * 1024
ROW_TILE = 512
SUB_ROWS = 256


def _params(n_grid):
    return pltpu.CompilerParams(dimension_semantics=("arbitrary",) * n_grid,
                                vmem_limit_bytes=VMEM_LIMIT)


def _const_spec(shape):
    nd = len(shape)
    return pl.BlockSpec(shape, lambda *_: (0,) * nd, pipeline_mode=pl.Buffered(1))


def _layer_spec(stacked, layer):
    zeros = (0,) * (stacked.ndim - 1)
    return pl.BlockSpec((None,) + stacked.shape[1:], lambda *_: (layer,) + zeros, pipeline_mode=pl.Buffered(1))


def _dot(a, b):
    return jnp.dot(a.astype(BF16), b.astype(BF16), preferred_element_type=F32)


def _dot_nt(a, b):
    return lax.dot_general(a.astype(BF16), b.astype(BF16), (((1,), (1,)), ((), ())),
                           preferred_element_type=F32)


def _dot_tn(a, b):
    return lax.dot_general(a.astype(BF16), b.astype(BF16), (((0,), (0,)), ((), ())),
                           preferred_element_type=F32)


def _rmsnorm(x, g):
    return x * lax.rsqrt(jnp.mean(x * x, axis=-1, keepdims=True) + EPS) * g


def _store_qkv_and_hgrn(za, zc, cos, sin, lb, gnorm, chunked, q_ref, k_ref, v_ref, zc_ref, rows):
    lane = lax.broadcasted_iota(jnp.int32, cos.shape, 1)
    first_half = (lane % HEAD_DIM) < HEAD_DIM // 2
    for out_ref, c0 in ((q_ref, 0), (k_ref, A_WIDTH)):
        for g in range(3):
            z = za[:, c0 + g * GROUP_W:c0 + (g + 1) * GROUP_W]
            partner = jnp.where(first_half, pltpu.roll(z, GROUP_W - HEAD_DIM // 2, 1),
                                pltpu.roll(z, HEAD_DIM // 2, 1))
            out_ref[g, rows, :] = z * cos + partner * sin
    for g in range(3):
        v_ref[g, rows, :] = za[:, 2 * A_WIDTH + g * GROUP_W:2 * A_WIDTH + (g + 1) * GROUP_W]
    for i, part in enumerate(_hgrn_inputs(zc, lb, gnorm, chunked)):
        zc_ref[rows, i * C_WIDTH:(i + 1) * C_WIDTH] = part


def _inproj_body(x_ref, g_ref, w_ref, cos_ref, sin_ref, lb_ref, gn_ref, q_ref, k_ref, v_ref, zc_ref):
    tm = x_ref.shape[0]
    sub = min(tm, SUB_ROWS)
    for r0 in range(0, tm, sub):
        rows = slice(r0, r0 + sub)
        h = _rmsnorm(x_ref[rows, :], g_ref[...]).astype(BF16)
        za = jnp.dot(h, w_ref[:, 0:COL_B], preferred_element_type=F32)
        zc = jnp.dot(h, w_ref[:, COL_C:COL_G], preferred_element_type=F32)
        _store_qkv_and_hgrn(za, zc, cos_ref[rows, :], sin_ref[rows, :], lb_ref[...], gn_ref[...], True,
                            q_ref, k_ref, v_ref, zc_ref, rows)


def _inproj(x2d, norm, w_in, cos, sin, lb, gnorm, tm):
    m = x2d.shape[0]
    assert tm % CHUNK == 0
    n_pos = cos.shape[0] // tm
    row = lambda w: pl.BlockSpec((tm, w), lambda i: (i, 0))
    tab = pl.BlockSpec((tm, GROUP_W), lambda i: (i % n_pos, 0))
    grp = pl.BlockSpec((3, tm, GROUP_W), lambda i: (0, i, 0))
    return pl.pallas_call(
        _inproj_body,
        grid=(m // tm,),
        in_specs=[row(D_MODEL), _const_spec((1, D_MODEL)), _const_spec(w_in.shape),
                  tab, tab, _const_spec((1, C_WIDTH)), _const_spec((1, C_WIDTH))],
        out_specs=[grp] * 3 + [row(5 * C_WIDTH)],
        out_shape=[jax.ShapeDtypeStruct((3, m, GROUP_W), F32)] * 3
        + [jax.ShapeDtypeStruct((m, 5 * C_WIDTH), F32)],
        compiler_params=_params(1),
        name="inproj",
    )(x2d, norm.reshape(1, D_MODEL), w_in, cos, sin, lb.reshape(1, C_WIDTH), gnorm.reshape(1, C_WIDTH))


def _sample_inproj_body(x_ref, g_ref, w_ref, cos_ref, sin_ref, lb_ref, gn_ref,
                        q_ref, k_ref, v_ref, zc_ref, zb_ref, zg_ref, wbf_ref, z_scr, *, n_blocks):
    j = pl.program_id(0)
    w = w_ref[...].astype(BF16)
    wbf_ref[...] = w
    h = _rmsnorm(x_ref[...], g_ref[...]).astype(BF16)
    z_scr[j] = jnp.dot(h, w, preferred_element_type=F32)

    @pl.when(j == n_blocks - 1)
    def _():
        z = jnp.concatenate([z_scr[i] for i in range(n_blocks)], axis=1)
        _store_qkv_and_hgrn(z[:, 0:COL_B], z[:, COL_C:COL_G], cos_ref[...], sin_ref[...], lb_ref[...],
                            gn_ref[...], False, q_ref, k_ref, v_ref, zc_ref, slice(None))
        zb_ref[...] = z[:, COL_B:COL_C]
        zg_ref[...] = z[:, COL_G:IN_COLS]


def _sample_inproj(x2d, norm, w_in_f32, layer, cos, sin, lb, gnorm, n_blocks=5):
    m = x2d.shape[0]
    bw = IN_COLS // n_blocks
    assert bw * n_blocks == IN_COLS and bw % 128 == 0
    widths = (5 * C_WIDTH, 3 * CONV_WIDTH, N_BRANCH * D_MODEL)
    return pl.pallas_call(
        functools.partial(_sample_inproj_body, n_blocks=n_blocks),
        grid=(n_blocks,),
        in_specs=[_const_spec((m, D_MODEL)), _const_spec((1, D_MODEL)),
                  pl.BlockSpec((None, D_MODEL, bw), lambda j: (layer, 0, j)),
                  _const_spec((m, GROUP_W)), _const_spec((m, GROUP_W)),
                  _const_spec((1, C_WIDTH)), _const_spec((1, C_WIDTH))],
        out_specs=[_const_spec((3, m, GROUP_W))] * 3 + [_const_spec((m, w)) for w in widths]
        + [pl.BlockSpec((D_MODEL, bw), lambda j: (0, j))],
        out_shape=[jax.ShapeDtypeStruct((3, m, GROUP_W), F32)] * 3
        + [jax.ShapeDtypeStruct((m, w), F32) for w in widths]
        + [jax.ShapeDtypeStruct((D_MODEL, IN_COLS), BF16)],
        scratch_shapes=[pltpu.VMEM((n_blocks, m, bw), F32)],
        compiler_params=_params(1),
        name="sample_inproj",
    )(x2d, norm.reshape(1, D_MODEL), w_in_f32, cos, sin, lb.reshape(1, C_WIDTH), gnorm.reshape(1, C_WIDTH))


def _rope_tables(pos):
    half = HEAD_DIM // 2
    inv = ROPE_THETA ** (-jnp.arange(half, dtype=F32) / half)
    ang = pos.astype(F32)[:, None] * inv[None, :]
    cos, sin = jnp.cos(ang), jnp.sin(ang)
    return jnp.tile(cos, (1, 4)), jnp.tile(jnp.concatenate([-sin, sin], axis=1), (1, 2))


def _attend(tiles):
    head0 = lax.broadcasted_iota(jnp.int32, (BAND, GROUP_W), 1) < HEAD_DIM
    scale = HEAD_DIM ** -0.5
    q2 = [jnp.concatenate([jnp.where(head0, q, 0.0), jnp.where(head0, 0.0, q)], axis=0).astype(BF16)
          for q, _ in tiles]
    keys = [jnp.concatenate([k for k, _, _ in blocks], axis=0) for _, blocks in tiles]
    vals = [jnp.concatenate([v1 for _, v1, _ in blocks], axis=0) for _, blocks in tiles]
    valid = [jnp.concatenate([ok for _, _, ok in blocks], axis=1) for _, blocks in tiles]
    s = [jnp.where(ok, _dot_nt(qq, kk) * scale, NEG_BIG) for qq, kk, ok in zip(q2, keys, valid)]
    m = [jnp.max(x, axis=-1, keepdims=True) for x in s]
    p = [jnp.exp(x - mm).astype(BF16) for x, mm in zip(s, m)]
    ext = [jnp.dot(pp, vv, preferred_element_type=F32) for pp, vv in zip(p, vals)]
    results = []
    for e, mm in zip(ext, m):
        den = e[:, GROUP_W:]
        o2 = e[:, 0:GROUP_W] / den
        lse2 = mm + jnp.log(den)
        results.append((jnp.where(head0, o2[0:BAND], o2[BAND:]), jnp.where(head0, lse2[0:BAND], lse2[BAND:])))
    return results


def _attn_prompt_body(q_ref, k_ref, v_ref, ya_ref, c0_ref, c1_ref, c2_ref, o_scr, lse_scr, *, seq):
    qi = lax.broadcasted_iota(jnp.int32, (2 * BAND, BAND), 0) % BAND
    kj = lax.broadcasted_iota(jnp.int32, (2 * BAND, BAND), 1)
    cur_valid = kj <= qi
    prev_valid = kj >= qi
    ones = jnp.ones((BAND, GROUP_W), BF16)

    def rows(start, dil):
        return pl.ds(start, BAND, stride=dil) if dil > 1 else pl.ds(start, BAND)

    def key_block(g, r):
        return k_ref[g, r, :].astype(BF16), jnp.concatenate([v_ref[g, r, :].astype(BF16), ones], axis=1)

    def run(g, dil, chains):
        tiles, slices = [], []
        for starts, first_prev in chains:
            chain = [rows(s, dil) for s in starts]
            keys = [key_block(g, r) for r in chain]
            for j, r in enumerate(chain):
                blocks = [(*keys[j], cur_valid)]
                if j > 0:
                    blocks.append((*keys[j - 1], prev_valid))
                elif first_prev is not None:
                    blocks.append((*key_block(g, rows(first_prev[0], dil)), prev_valid & first_prev[1]))
                tiles.append((q_ref[g, r, :], blocks))
            slices += chain
        for r, (o, lse) in zip(slices, _attend(tiles)):
            o_scr[g, r, :] = o
            lse_scr[g, r, :] = lse

    for g, (_, dil) in enumerate(DSWA_GROUPS):
        n_blocks = seq // dil // BAND
        if dil == 1:
            def dense(i, carry):
                s0 = pl.multiple_of(4 * i * BAND, BAND)
                prev = pl.multiple_of(jnp.maximum(4 * i - 1, 0) * BAND, BAND)
                run(g, dil, [([s0 + u * BAND for u in range(4)], (prev, i > 0))])
                return carry

            lax.fori_loop(0, n_blocks // 4, dense, 0)
        elif n_blocks > 1:
            def residue(r, carry, g=g, dil=dil, n_blocks=n_blocks):
                run(g, dil, [([r + i * BAND * dil for i in range(n_blocks)], None)])
                return carry

            lax.fori_loop(0, dil, residue, 0)
        else:
            def residues(j, carry, g=g, dil=dil):
                run(g, dil, [([4 * j + u], None) for u in range(4)])
                return carry

            lax.fori_loop(0, dil // 4, residues, 0)

    step = 256

    def merge(i, carry):
        sl = pl.ds(pl.multiple_of(i * step, step), step)
        lse = [lse_scr[g, sl, :] for g in range(3)]
        m = jnp.maximum(jnp.maximum(lse[0], lse[1]), lse[2])
        e = [jnp.exp(x - m) for x in lse]
        acc = e[0] * o_scr[0, sl, :] + e[1] * o_scr[1, sl, :] + e[2] * o_scr[2, sl, :]
        ya_ref[0, sl, :] = acc / (e[0] + e[1] + e[2])
        return carry

    lax.fori_loop(0, seq // step, merge, 0)

    for g, (c_ref, (win, _)) in enumerate(zip((c0_ref, c1_ref, c2_ref), DSWA_GROUPS)):
        for j in range(0, win, 512):
            w = min(512, win - j)
            c_ref[0:GROUP_W, j:j + w] = k_ref[g, seq - win + j:seq - win + j + w, :].T
            c_ref[GROUP_W:2 * GROUP_W, j:j + w] = v_ref[g, seq - win + j:seq - win + j + w, :].T


def _attn_prompt(q, k, v, layer, caches):
    _, bsz, seq, _ = q.shape
    depth = 2
    assert seq % (BAND * 16) == 0 and seq >= 2048
    qkv_spec = pl.BlockSpec((3, None, seq, GROUP_W), lambda b: (0, b, 0, 0))
    cache_specs = [pl.BlockSpec((None, None, 2 * GROUP_W, win), lambda b: (layer, b, 0, 0))
                   for win, _ in DSWA_GROUPS]
    cache_shapes = [jax.ShapeDtypeStruct((depth, bsz, 2 * GROUP_W, win), F32) for win, _ in DSWA_GROUPS]
    in_specs = [qkv_spec] * 3
    args = [q, k, v]
    aliases = {}
    if caches is not None:
        in_specs = in_specs + [pl.BlockSpec(memory_space=pl.ANY)] * 3
        args = args + list(caches)
        aliases = {3: 1, 4: 2, 5: 3}

    def body(*refs):
        ins, rest = refs[:3], refs[3 + (3 if caches is not None else 0):]
        _attn_prompt_body(*ins, *rest, seq=seq)

    return pl.pallas_call(
        body,
        grid=(bsz,),
        in_specs=in_specs,
        out_specs=[pl.BlockSpec((1, seq, GROUP_W), lambda b: (b, 0, 0))] + cache_specs,
        out_shape=[jax.ShapeDtypeStruct((bsz, seq, GROUP_W), F32)] + cache_shapes,
        scratch_shapes=[pltpu.VMEM((3, seq, GROUP_W), F32), pltpu.VMEM((3, seq, GROUP_W), F32)],
        input_output_aliases=aliases,
        compiler_params=_params(1),
        name="attn_prompt",
    )(*args)


def _log_forget_and_kgate(zf, lb):
    log_sig = -(jnp.maximum(-zf, 0.0) + jnp.log(1.0 + jnp.exp(-jnp.abs(zf))))
    a = jnp.log(jnp.maximum(lb, LB_FLOOR))
    c = jnp.log1p(-lb) + log_sig
    log_f = jnp.maximum(a, c) + jnp.log(1.0 + jnp.exp(-jnp.abs(a - c)))
    kgate = (1.0 - lb) * jax.nn.sigmoid(-zf)
    return log_f, kgate


def _silu(x):
    return x * jax.nn.sigmoid(x)


def _split3(x):
    hi = x.astype(BF16)
    r1 = x - hi.astype(F32)
    mid = r1.astype(BF16)
    lo = (r1 - mid.astype(F32)).astype(BF16)
    return hi, mid, lo


def _hgrn_inputs(zc, lb, gnorm, chunked):
    q, zf, v, og = (zc[:, i * C_WIDTH:(i + 1) * C_WIDTH] for i in range(4))
    log_f, k = _log_forget_and_kgate(zf, lb)
    if chunked:
        n = zc.shape[0]
        r = lax.broadcasted_iota(jnp.int32, (n, n), 0)
        c = lax.broadcasted_iota(jnp.int32, (n, n), 1)
        tri = jnp.where((c <= r) & ((r // CHUNK) == (c // CHUNK)), 1.0, 0.0).astype(BF16)
        b = functools.reduce(jnp.add, [jnp.dot(tri, p, preferred_element_type=F32) for p in _split3(log_f)])
    else:
        b = log_f
    return q, b, k, v, gnorm * _silu(og)


def _hgrn_chunk(q, b, k, v, gate, st, cst):
    n = CHUNK
    ones_bd = cst["ones_bd"]

    def head_sums(x):
        return jnp.dot(x.astype(BF16), ones_bd, preferred_element_type=F32)

    tiles = (n // SUB, SUB, C_WIDTH)
    q3, k3, b3 = (x.reshape(tiles) for x in (q, k, b))
    f3 = jnp.exp(jnp.minimum(b3 - pltpu.roll(b3, 1, 1), 0.0))
    ys, decay = [q * k], None
    for d in range(1, SUB):
        decay = f3 if d == 1 else decay * pltpu.roll(f3, d - 1, 1)
        ys.append((q3 * pltpu.roll(k3, d, 1) * decay).reshape(n, C_WIDTH))
    att_d = head_sums(jnp.concatenate(ys, axis=0))
    att = functools.reduce(jnp.add, [att_d[d * n:(d + 1) * n, :] * cst["place"][d] for d in range(SUB)])

    for lvl, m in enumerate(LEVELS):
        ref_q, ref_k = [], []
        for j in range(n // m):
            own = b[j * m:(j + 1) * m, :]
            if j % 2 == 1:
                ref_q.append(jnp.broadcast_to(b[j * m - 1:j * m, :], (m, C_WIDTH)))
                ref_k.append(own)
            else:
                ref_q.append(own)
                ref_k.append(jnp.broadcast_to(b[(j + 1) * m - 1:(j + 1) * m, :], (m, C_WIDTH)))
        upper = cst["upper"][lvl]
        qt = q * jnp.exp(jnp.minimum(b - jnp.concatenate(ref_q, axis=0), 0.0)) * upper
        kt = k * jnp.exp(jnp.minimum(jnp.concatenate(ref_k, axis=0) - b, 0.0)) * (1.0 - upper)
        k_stack = jnp.concatenate([kt.astype(BF16)] * C_HEADS, axis=0) * cst["stack"]
        att = att + _dot_nt(qt, k_stack) * cst["same_block"][lvl]
    v_stack = jnp.concatenate([v.astype(BF16)] * C_HEADS, axis=0) * cst["stack"]
    o = jnp.dot(att.astype(BF16), v_stack, preferred_element_type=F32)

    o = o + _dot_nt(q * jnp.exp(b), st)
    b_last = b[n - 1:n, :]
    k_hat = k * jnp.exp(b_last - b)
    st = st * jnp.exp(b_last) + _dot_tn(v, k_hat) * cst["same_head"]

    sq = o * o
    sq_hi = sq.astype(BF16)
    ms = head_sums(sq_hi) + head_sums(sq - sq_hi.astype(F32))
    o = o * lax.rsqrt(ms * (1.0 / 64) + EPS) * gate
    return o, st


def _hgrn_constants():
    n = CHUNK
    row = lax.broadcasted_iota(jnp.int32, (n, C_WIDTH), 0)
    hr = lax.broadcasted_iota(jnp.int32, (C_WIDTH, C_WIDTH), 0) // 64
    hc = lax.broadcasted_iota(jnp.int32, (C_WIDTH, C_WIDTH), 1) // 64
    same_head = jnp.where(hr == hc, 1.0, 0.0)
    srow = lax.broadcasted_iota(jnp.int32, (n, C_HEADS * n), 0)
    scol = lax.broadcasted_iota(jnp.int32, (n, C_HEADS * n), 1) % n
    return {
        "ones_bd": same_head.astype(BF16),
        "same_head": same_head,
        "stack": same_head.astype(BF16),
        "place": [jnp.where((scol == srow - d) & (srow % SUB >= d), 1.0, 0.0) for d in range(SUB)],
        "upper": [jnp.where((row // m) % 2 == 1, 1.0, 0.0) for m in LEVELS],
        "same_block": [jnp.where(srow // (2 * m) == scol // (2 * m), 1.0, 0.0) for m in LEVELS],
    }


def _hgrn_prompt_body(zc_ref, yc_ref, st_ref, st_scr, *, rows, n_seq):
    t = pl.program_id(1)

    @pl.when(t == 0)
    def _():
        st_scr[...] = jnp.zeros_like(st_scr)

    cst = _hgrn_constants()

    def chunk(c, carry):
        sl = pl.ds(pl.multiple_of(c * CHUNK, CHUNK), CHUNK)
        for i in range(n_seq):
            o, st = _hgrn_chunk(*(zc_ref[i, sl, j * C_WIDTH:(j + 1) * C_WIDTH] for j in range(5)),
                                st_scr[i], cst)
            st_scr[i] = st
            yc_ref[i, sl, :] = o
        return carry

    lax.fori_loop(0, rows // CHUNK, chunk, 0)

    @pl.when(t == pl.num_programs(1) - 1)
    def _():
        for i in range(n_seq):
            for h in range(C_HEADS):
                st_ref[i, h] = st_scr[i, h * 64:(h + 1) * 64, h * 64:(h + 1) * 64]


def _hgrn_prompt(zc, rows=ROW_TILE):
    bsz, seq, _ = zc.shape
    n_seq = max(n for n in (4, 2, 1) if bsz % n == 0)
    assert CHUNK == 64
    yc, st = pl.pallas_call(
        functools.partial(_hgrn_prompt_body, rows=rows, n_seq=n_seq),
        grid=(bsz // n_seq, seq // rows),
        in_specs=[pl.BlockSpec((n_seq, rows, 5 * C_WIDTH), lambda b, t: (b, t, 0))],
        out_specs=[pl.BlockSpec((n_seq, rows, C_WIDTH), lambda b, t: (b, t, 0)),
                   pl.BlockSpec((n_seq, C_HEADS, 64, 64), lambda b, t: (b, 0, 0, 0))],
        out_shape=[jax.ShapeDtypeStruct((bsz, seq, C_WIDTH), F32),
                   jax.ShapeDtypeStruct((bsz, C_HEADS, 64, 64), F32)],
        scratch_shapes=[pltpu.VMEM((n_seq, C_WIDTH, C_WIDTH), F32)],
        compiler_params=_params(2),
        name="hgrn_prompt",
    )(zc)
    return yc, jnp.swapaxes(st, -1, -2)


def _gated_merge(x, ya, yb, yc, zg, bg, wpa, wpb, wpc, wo):
    g = jax.nn.sigmoid(zg + bg)
    mix = (g[:, 0:D_MODEL] * _dot(ya, wpa) + g[:, D_MODEL:2 * D_MODEL] * _dot(yb, wpb)
           + g[:, 2 * D_MODEL:3 * D_MODEL] * _dot(yc, wpc))
    return x + _dot(mix, wo)


def _project_conv_and_gates(x, g1, w_ref):
    h = _rmsnorm(x, g1).astype(BF16)
    return (jnp.dot(h, w_ref[:, COL_B:COL_C], preferred_element_type=F32),
            jnp.dot(h, w_ref[:, COL_G:IN_COLS], preferred_element_type=F32))


def _merge_prompt_body(x_ref, ya_ref, yc_ref, g1_ref, w_ref, bg_ref, cw_ref, wpa_ref, wpb_ref, wpc_ref,
                       wo_ref, x1_ref, nc_ref, u_scr, *, rows):
    @pl.when(pl.program_id(1) == 0)
    def _():
        u_scr[0:8, :] = jnp.zeros((8, CONV_WIDTH), F32)

    sub = min(rows, ROW_TILE)
    for r0 in range(0, rows, sub):
        sl = slice(r0, r0 + sub)
        x = x_ref[0, sl, :]
        zb, zg = _project_conv_and_gates(x, g1_ref[...], w_ref)
        u = zb[:, 2 * CONV_WIDTH:3 * CONV_WIDTH] * zb[:, 0:CONV_WIDTH]
        u_scr[8 + r0:8 + r0 + sub, :] = u
        conv = (cw_ref[0:1, :] * u_scr[pl.ds(6 + r0, sub), :] + cw_ref[1:2, :] * u_scr[pl.ds(7 + r0, sub), :]
                + cw_ref[2:3, :] * u)
        yb = zb[:, CONV_WIDTH:2 * CONV_WIDTH] * conv
        x1_ref[0, sl, :] = _gated_merge(x, ya_ref[0, sl, :], yb, yc_ref[0, sl, :], zg, bg_ref[...],
                                        wpa_ref[...], wpb_ref[...], wpc_ref[...], wo_ref[...])
    nc_ref[0] = u_scr[8 + rows - (CONV_K - 1):8 + rows, :]
    u_scr[0:8, :] = u_scr[rows:rows + 8, :]


def _merge_prompt(x, ya, yc, norm1, w_in, b_gate, conv_w, wpa, wpb, wpc, wo, rows=2 * ROW_TILE):
    bsz, seq, _ = x.shape
    blk = lambda w: pl.BlockSpec((1, rows, w), lambda b, t: (b, t, 0))
    return pl.pallas_call(
        functools.partial(_merge_prompt_body, rows=rows),
        grid=(bsz, seq // rows),
        in_specs=[blk(D_MODEL), blk(GROUP_W), blk(C_WIDTH),
                  _const_spec((1, D_MODEL)), _const_spec(w_in.shape),
                  _const_spec((1, N_BRANCH * D_MODEL)), _const_spec((CONV_K, CONV_WIDTH)),
                  _const_spec(wpa.shape), _const_spec(wpb.shape), _const_spec(wpc.shape), _const_spec(wo.shape)],
        out_specs=[blk(D_MODEL), pl.BlockSpec((1, CONV_K - 1, CONV_WIDTH), lambda b, t: (b, 0, 0))],
        out_shape=[jax.ShapeDtypeStruct((bsz, seq, D_MODEL), F32),
                   jax.ShapeDtypeStruct((bsz, CONV_K - 1, CONV_WIDTH), F32)],
        scratch_shapes=[pltpu.VMEM((rows + 8, CONV_WIDTH), F32)],
        compiler_params=_params(2),
        name="merge_prompt",
    )(x, ya, yc, norm1.reshape(1, D_MODEL), w_in, b_gate.reshape(1, -1), conv_w, wpa, wpb, wpc, wo)


def _merge_sample_body(x_ref, ya_ref, yc_ref, zb_ref, zg_ref, p0_ref, p1_ref, bg_ref, cw_ref, wpa_ref,
                       wpb_ref, wpc_ref, wo_ref, x1_ref, u_ref, wpa_bf_ref, wpb_bf_ref, wpc_bf_ref, wo_bf_ref):
    bf = []
    for w_ref, bf_ref in ((wpa_ref, wpa_bf_ref), (wpb_ref, wpb_bf_ref), (wpc_ref, wpc_bf_ref),
                          (wo_ref, wo_bf_ref)):
        bf.append(w_ref[...].astype(BF16))
        bf_ref[...] = bf[-1]
    zb = zb_ref[...]
    u = zb[:, 2 * CONV_WIDTH:3 * CONV_WIDTH] * zb[:, 0:CONV_WIDTH]
    conv = cw_ref[0:1, :] * p0_ref[...] + cw_ref[1:2, :] * p1_ref[...] + cw_ref[2:3, :] * u
    yb = zb[:, CONV_WIDTH:2 * CONV_WIDTH] * conv
    u_ref[...] = u
    x1_ref[...] = _gated_merge(x_ref[...], ya_ref[...], yb, yc_ref[...], zg_ref[...], bg_ref[...], *bf)


def _merge_sample(x, ya, yc, zb, zg, prev0, prev1, b_gate, conv_w, layer, w_pa, w_pb, w_pc, w_o):
    m = x.shape[0]
    small = (x, ya, yc, zb, zg, prev0, prev1, b_gate.reshape(1, -1), conv_w)
    stacks = (w_pa, w_pb, w_pc, w_o)
    return pl.pallas_call(
        _merge_sample_body,
        grid=(1,),
        in_specs=[_const_spec(a.shape) for a in small] + [_layer_spec(w, layer) for w in stacks],
        out_specs=[_const_spec((m, D_MODEL)), _const_spec((m, CONV_WIDTH))]
        + [_const_spec(w.shape[1:]) for w in stacks],
        out_shape=[jax.ShapeDtypeStruct((m, D_MODEL), F32), jax.ShapeDtypeStruct((m, CONV_WIDTH), F32)]
        + [jax.ShapeDtypeStruct(w.shape[1:], BF16) for w in stacks],
        compiler_params=_params(1),
        name="merge_sample",
    )(*small, *stacks)


def _mlp_body(x_ref, g2_ref, wup_ref, wdn_ref, gf_ref, out_ref, *, final):
    tm = x_ref.shape[0]
    sub = min(tm, ROW_TILE)
    for r0 in range(0, tm, sub):
        rows = slice(r0, r0 + sub)
        x = x_ref[rows, :]
        h = _rmsnorm(x, g2_ref[...]).astype(BF16)
        acc = x
        for c in range(D_FF // D_MODEL):
            cols = slice(c * D_MODEL, (c + 1) * D_MODEL)
            a = jnp.maximum(jnp.dot(h, wup_ref[:, cols], preferred_element_type=F32), 0.0)
            acc = acc + jnp.dot((a * a).astype(BF16), wdn_ref[cols, :], preferred_element_type=F32)
        out_ref[rows, :] = _rmsnorm(acc, gf_ref[...]) if final else acc


def _mlp(x2d, norm2, wup, wdn, norm_f, final, tm):
    m = x2d.shape[0]
    row = pl.BlockSpec((tm, D_MODEL), lambda i: (i, 0))
    return pl.pallas_call(
        functools.partial(_mlp_body, final=final),
        grid=(m // tm,),
        in_specs=[row, _const_spec((1, D_MODEL)), _const_spec(wup.shape), _const_spec(wdn.shape),
                  _const_spec((1, D_MODEL))],
        out_specs=row,
        out_shape=jax.ShapeDtypeStruct((m, D_MODEL), F32),
        compiler_params=_params(1),
        name="mlp",
    )(x2d, norm2.reshape(1, D_MODEL), wup, wdn, norm_f.reshape(1, D_MODEL))


def _mlp_sample_body(x_ref, g2_ref, wup_ref, wdn_ref, gf_ref, out_ref, wup_bf_ref, wdn_bf_ref, acc_scr, *, final):
    c = pl.program_id(0)
    wu = wup_ref[...].astype(BF16)
    wd = wdn_ref[...].astype(BF16)
    wup_bf_ref[...] = wu
    wdn_bf_ref[...] = wd
    x = x_ref[...]
    h = _rmsnorm(x, g2_ref[...]).astype(BF16)
    a = jnp.maximum(jnp.dot(h, wu, preferred_element_type=F32), 0.0)
    part = jnp.dot((a * a).astype(BF16), wd, preferred_element_type=F32)

    @pl.when(c == 0)
    def _():
        acc_scr[...] = x + part

    @pl.when(c > 0)
    def _():
        acc_scr[...] = acc_scr[...] + part

    @pl.when(c == pl.num_programs(0) - 1)
    def _():
        acc = acc_scr[...]
        out_ref[...] = _rmsnorm(acc, gf_ref[...]) if final else acc


def _mlp_sample(x2d, norm2, layer, w_up, w_down, norm_f, final):
    m = x2d.shape[0]
    n_chunks = D_FF // D_MODEL
    return pl.pallas_call(
        functools.partial(_mlp_sample_body, final=final),
        grid=(n_chunks,),
        in_specs=[_const_spec((m, D_MODEL)), _const_spec((1, D_MODEL)),
                  pl.BlockSpec((None, D_MODEL, D_MODEL), lambda c: (layer, 0, c)),
                  pl.BlockSpec((None, D_MODEL, D_MODEL), lambda c: (layer, c, 0)),
                  _const_spec((1, D_MODEL))],
        out_specs=[_const_spec((m, D_MODEL)), pl.BlockSpec((D_MODEL, D_MODEL), lambda c: (0, c)),
                   pl.BlockSpec((D_MODEL, D_MODEL), lambda c: (c, 0))],
        out_shape=[jax.ShapeDtypeStruct((m, D_MODEL), F32), jax.ShapeDtypeStruct((D_MODEL, D_FF), BF16),
                   jax.ShapeDtypeStruct((D_FF, D_MODEL), BF16)],
        scratch_shapes=[pltpu.VMEM((m, D_MODEL), F32)],
        compiler_params=_params(1),
        name="mlp_sample",
    )(x2d, norm2.reshape(1, D_MODEL), w_up, w_down, norm_f.reshape(1, D_MODEL))


def _as_column(row):
    return jnp.concatenate([jnp.broadcast_to(row[:, c:c + 128], (128, 128)).T for c in range(0, row.shape[1], 128)],
                           axis=0)


def _sample_mix_body(q_ref, k_ref, v_ref, c0_ref, c1_ref, c2_ref, zrow_ref,
                     s_ref, ya_ref, yc_ref, s_out_ref, n0_ref, n1_ref, n2_ref):
    lane = lax.broadcasted_iota(jnp.int32, (8, GROUP_W), 1)
    rowi = lax.broadcasted_iota(jnp.int32, (8, GROUP_W), 0)
    head_of_row = (lane // HEAD_DIM) == rowi
    head0 = lax.broadcasted_iota(jnp.int32, (1, GROUP_W), 1) < HEAD_DIM

    outs, lses = [], []
    for g, (c_ref, n_ref, (win, dil)) in enumerate(zip((c0_ref, c1_ref, c2_ref), (n0_ref, n1_ref, n2_ref),
                                                        DSWA_GROUPS)):
        q, k_new, v_new = q_ref[g], k_ref[g], v_ref[g]
        buf = c_ref[...]
        pos = lax.broadcasted_iota(jnp.int32, (8, win), 1)
        q2 = jnp.where(head_of_row, jnp.broadcast_to(q, (8, GROUP_W)), 0.0)
        scale = HEAD_DIM ** -0.5
        s = jnp.where(pos % dil == 0, _dot(q2, buf[0:GROUP_W, :]) * scale, NEG_BIG)
        s_self = jnp.sum(q2 * k_new, axis=-1, keepdims=True) * scale
        m = jnp.maximum(jnp.max(s, axis=-1, keepdims=True), s_self)
        p, p_self = jnp.exp(s - m), jnp.exp(s_self - m)
        l = jnp.sum(p, axis=-1, keepdims=True) + p_self
        o = (_dot_nt(p, buf[GROUP_W:2 * GROUP_W, :]) + p_self * v_new) / l
        lse = m + jnp.log(l)
        outs.append(jnp.where(head0, o[0:1], o[1:2]))
        lses.append(jnp.where(head0, jnp.broadcast_to(lse[0:1], (1, GROUP_W)),
                              jnp.broadcast_to(lse[1:2], (1, GROUP_W))))
        rolled = pltpu.roll(buf, win - 1, 1)
        new_col = jnp.concatenate([_as_column(k_new), _as_column(v_new)], axis=0)
        last = lax.broadcasted_iota(jnp.int32, new_col.shape, 1) == 127
        if win > 128:
            n_ref[:, 0:win - 128] = rolled[:, 0:win - 128]
        n_ref[:, win - 128:win] = jnp.where(last, new_col, rolled[:, win - 128:win])
    m = jnp.maximum(jnp.maximum(lses[0], lses[1]), lses[2])
    e = [jnp.exp(x - m) for x in lses]
    ya_ref[0] = (e[0] * outs[0] + e[1] * outs[1] + e[2] * outs[2]) / (e[0] + e[1] + e[2])

    zrow = zrow_ref[0]
    q_col, log_f, k_col = (_as_column(zrow[:, j * C_WIDTH:(j + 1) * C_WIDTH])[:, 0:64] for j in range(3))
    v_row = zrow[:, 3 * C_WIDTH:4 * C_WIDTH]
    gate_row = zrow[:, 4 * C_WIDTH:5 * C_WIDTH]
    v_sel = jnp.concatenate([jnp.broadcast_to(v_row[:, h * 64:(h + 1) * 64], (64, 64)) for h in range(C_HEADS)],
                            axis=0)
    s_new = jnp.exp(log_f) * s_ref[...] + k_col * v_sel
    s_out_ref[...] = s_new
    qs = q_col * s_new
    o4 = jnp.concatenate([jnp.sum(qs[h * 64:(h + 1) * 64, :], axis=0, keepdims=True) for h in range(C_HEADS)],
                         axis=0)
    gate4 = jnp.concatenate([gate_row[:, h * 64:(h + 1) * 64] for h in range(C_HEADS)], axis=0)
    o4 = o4 * lax.rsqrt(jnp.mean(o4 * o4, axis=-1, keepdims=True) + EPS) * gate4
    yc_ref[0] = o4


def _sample_mix(q, k, v, caches, zc, state, layer, new_caches):
    n = q.shape[1]
    depth = caches[0].shape[0]
    row3 = lambda w: pl.BlockSpec((1, 1, w), lambda b: (b, 0, 0))
    grp = pl.BlockSpec((3, None, 1, GROUP_W), lambda b: (0, b, 0, 0))
    cache_specs = [pl.BlockSpec((None, None, 2 * GROUP_W, win), lambda b: (layer, b, 0, 0))
                   for win, _ in DSWA_GROUPS]
    in_specs = ([grp] * 3 + cache_specs
                + [row3(5 * C_WIDTH), pl.BlockSpec((None, None, C_WIDTH, 64), lambda b: (layer, b, 0, 0))])
    args = [q.reshape(3, n, 1, GROUP_W), k.reshape(3, n, 1, GROUP_W), v.reshape(3, n, 1, GROUP_W),
            *caches, zc.reshape(n, 1, 5 * C_WIDTH), state]
    aliases = {}
    n_extra = 0
    if new_caches is not None:
        n_extra = 3
        aliases = {len(args) + i: 3 + i for i in range(3)}
        in_specs = in_specs + [pl.BlockSpec(memory_space=pl.ANY)] * 3
        args = args + list(new_caches)

    def body(*refs):
        n_in = len(args) - n_extra
        _sample_mix_body(*refs[:n_in], *refs[n_in + n_extra:])

    return pl.pallas_call(
        body,
        grid=(n,),
        in_specs=in_specs,
        out_specs=[pl.BlockSpec((1, 1, GROUP_W), lambda b: (b, 0, 0)),
                   pl.BlockSpec((1, C_HEADS, 64), lambda b: (b, 0, 0)),
                   pl.BlockSpec((None, C_WIDTH, 64), lambda b: (b, 0, 0))] + cache_specs,
        out_shape=[jax.ShapeDtypeStruct((n, 1, GROUP_W), F32), jax.ShapeDtypeStruct((n, C_HEADS, 64), F32),
                   jax.ShapeDtypeStruct((n, C_WIDTH, 64), F32)]
        + [jax.ShapeDtypeStruct((depth, n, 2 * GROUP_W, win), F32) for win, _ in DSWA_GROUPS],
        input_output_aliases=aliases,
        compiler_params=_params(1),
        name="sample_mix",
    )(*args)


def kernel(x_prompt, x_sample, cache_kv_w128, cache_kv_w512, cache_kv_w2048, state_conv, state_hgrn, w_in, b_gate,
           norm1, conv_w, hgrn_lb, hgrn_norm, w_pa, w_pb, w_pc, w_o, norm2, w_up, w_down, norm_f):
    bp, tp, d = x_prompt.shape
    bs, ts, _ = x_sample.shape
    depth = w_in.shape[0]
    assert d == D_MODEL and ts == 1 and depth == 2
    assert PAST_LEN >= max(win for win, _ in DSWA_GROUPS)

    sm = jax.nn.softmax(hgrn_lb.astype(F32), axis=0)
    lower = jnp.cumsum(sm, axis=0) - sm[0:1]

    cos_p, sin_p = _rope_tables(jnp.arange(tp, dtype=jnp.int32))
    cos_s, sin_s = _rope_tables(jnp.full((bs,), PAST_LEN, dtype=jnp.int32))

    caches = [jnp.transpose(c, (0, 1, 3, 4, 5, 2)).reshape(depth, bs, 2 * GROUP_W, win)
              for c, (win, _) in zip((cache_kv_w128, cache_kv_w512, cache_kv_w2048), DSWA_GROUPS)]
    state = state_hgrn.reshape(depth, bs, C_WIDTH, 64)

    xp = x_prompt.reshape(bp * tp, d)
    xs = x_sample.reshape(bs, d)
    kv_p, kv_s = None, None
    conv_p, hgrn_p, conv_s, hgrn_s = [], [], [], []
    for l in range(depth):
        final = l == depth - 1

        q, k, v, zc, zb, zg, w_in_bf = _sample_inproj(xs, norm1[l], w_in, l, cos_s, sin_s, lower[l], hgrn_norm[l])
        ya, yc, s_new, *kv_s = _sample_mix(q, k, v, caches, zc, state, l, kv_s)
        x1, u, wpa, wpb, wpc, wo = _merge_sample(xs, ya.reshape(bs, GROUP_W), yc.reshape(bs, C_WIDTH), zb, zg,
                                                 state_conv[l, :, 0], state_conv[l, :, 1], b_gate[l], conv_w[l],
                                                 l, w_pa, w_pb, w_pc, w_o)
        xs, wup, wdn = _mlp_sample(x1, norm2[l], l, w_up, w_down, norm_f, final)
        conv_s.append(jnp.stack([state_conv[l, :, 1], u], axis=1))
        hgrn_s.append(s_new.reshape(bs, C_HEADS, 64, 64))

        q, k, v, zc = _inproj(xp, norm1[l], w_in_bf, cos_p, sin_p, lower[l], hgrn_norm[l], tm=2 * ROW_TILE)
        ya, *kv_p = _attn_prompt(q.reshape(3, bp, tp, -1), k.reshape(3, bp, tp, -1), v.reshape(3, bp, tp, -1),
                                 l, kv_p)
        yc, st = _hgrn_prompt(zc.reshape(bp, tp, -1))
        x1, nc = _merge_prompt(xp.reshape(bp, tp, d), ya, yc, norm1[l], w_in_bf,
                               b_gate[l], conv_w[l], wpa, wpb, wpc, wo)
        xp = _mlp(x1.reshape(bp * tp, d), norm2[l], wup, wdn, norm_f, final, tm=2 * ROW_TILE)
        conv_p.append(nc)
        hgrn_p.append(st)

    def kv6(c, n):
        return jnp.transpose(c.reshape(depth, n, 2, 2, HEAD_DIM, c.shape[3]), (0, 1, 5, 2, 3, 4))

    return (xp.reshape(bp, tp, d), xs.reshape(bs, ts, d),
            kv6(kv_p[0], bp), kv6(kv_p[1], bp), kv6(kv_p[2], bp), jnp.stack(conv_p), jnp.stack(hgrn_p),
            kv6(kv_s[0], bs), kv6(kv_s[1], bs), kv6(kv_s[2], bs), jnp.stack(conv_s), jnp.stack(hgrn_s))
```

```python
import functools

import jax
import jax.numpy as jnp
from jax import lax
from jax.experimental import pallas as pl
from jax.experimental.pallas import tpu as pltpu

F32 = jnp.float32
BF16 = jnp.bfloat16

D_MODEL = 1024
HEAD_DIM = 64
DSWA_GROUPS = ((128, 1), (512, 4), (2048, 16))
BAND = 128
GROUP_W = 2 * HEAD_DIM
A_WIDTH = 3 * GROUP_W
CONV_WIDTH = 384
CONV_K = 3
C_HEADS = 4
C_WIDTH = C_HEADS * 64
N_BRANCH = 3
D_FF = 4 * D_MODEL
ROPE_THETA = 10000.0
EPS = 1e-6
NEG_BIG = -1e30
LB_FLOOR = 1e-30
PAST_LEN = 8192
IN_COLS = 3 * A_WIDTH + 3 * CONV_WIDTH + 4 * C_WIDTH + N_BRANCH * D_MODEL
COL_B = 3 * A_WIDTH
COL_C = COL_B + 3 * CONV_WIDTH
COL_G = COL_C + 4 * C_WIDTH
CHUNK = 64
SUB = 8
LEVELS = (32, 16, 8)
VMEM_LIMIT = 56 * 1024 * 1024
ROW_TILE = 512
SUB_ROWS = 256


def _params(n_grid):
    return pltpu.CompilerParams(dimension_semantics=("arbitrary",) * n_grid,
                                vmem_limit_bytes=VMEM_LIMIT)


def _const_spec(shape):
    nd = len(shape)
    return pl.BlockSpec(shape, lambda *_: (0,) * nd, pipeline_mode=pl.Buffered(1))


def _layer_spec(stacked, layer):
    zeros = (0,) * (stacked.ndim - 1)
    return pl.BlockSpec((None,) + stacked.shape[1:], lambda *_: (layer,) + zeros, pipeline_mode=pl.Buffered(1))


def _dot(a, b):
    return jnp.dot(a.astype(BF16), b.astype(BF16), preferred_element_type=F32)


def _dot_nt(a, b):
    return lax.dot_general(a.astype(BF16), b.astype(BF16), (((1,), (1,)), ((), ())),
                           preferred_element_type=F32)


def _dot_tn(a, b):
    return lax.dot_general(a.astype(BF16), b.astype(BF16), (((0,), (0,)), ((), ())),
                           preferred_element_type=F32)


def _rmsnorm(x, g):
    return x * lax.rsqrt(jnp.mean(x * x, axis=-1, keepdims=True) + EPS) * g


def _store_qkv_and_hgrn(za, zc, cos, sin, lb, gnorm, chunked, q_ref, k_ref, v_ref, zc_ref, rows):
    lane = lax.broadcasted_iota(jnp.int32, cos.shape, 1)
    first_half = (lane % HEAD_DIM) < HEAD_DIM // 2
    for out_ref, c0 in ((q_ref, 0), (k_ref, A_WIDTH)):
        for g in range(3):
            z = za[:, c0 + g * GROUP_W:c0 + (g + 1) * GROUP_W]
            partner = jnp.where(first_half, pltpu.roll(z, GROUP_W - HEAD_DIM // 2, 1),
                                pltpu.roll(z, HEAD_DIM // 2, 1))
            out_ref[g, rows, :] = z * cos + partner * sin
    for g in range(3):
        v_ref[g, rows, :] = za[:, 2 * A_WIDTH + g * GROUP_W:2 * A_WIDTH + (g + 1) * GROUP_W]
    for i, part in enumerate(_hgrn_inputs(zc, lb, gnorm, chunked)):
        zc_ref[rows, i * C_WIDTH:(i + 1) * C_WIDTH] = part


def _inproj_body(x_ref, g_ref, w_ref, cos_ref, sin_ref, lb_ref, gn_ref, q_ref, k_ref, v_ref, zc_ref):
    tm = x_ref.shape[0]
    sub = min(tm, SUB_ROWS)
    for r0 in range(0, tm, sub):
        rows = slice(r0, r0 + sub)
        h = _rmsnorm(x_ref[rows, :], g_ref[...]).astype(BF16)
        za = jnp.dot(h, w_ref[:, 0:COL_B], preferred_element_type=F32)
        zc = jnp.dot(h, w_ref[:, COL_C:COL_G], preferred_element_type=F32)
        _store_qkv_and_hgrn(za, zc, cos_ref[rows, :], sin_ref[rows, :], lb_ref[...], gn_ref[...], True,
                            q_ref, k_ref, v_ref, zc_ref, rows)


def _inproj(x2d, norm, w_in, cos, sin, lb, gnorm, tm):
    m = x2d.shape[0]
    assert tm % CHUNK == 0
    n_pos = cos.shape[0] // tm
    row = lambda w: pl.BlockSpec((tm, w), lambda i: (i, 0))
    tab = pl.BlockSpec((tm, GROUP_W), lambda i: (i % n_pos, 0))
    grp = pl.BlockSpec((3, tm, GROUP_W), lambda i: (0, i, 0))
    return pl.pallas_call(
        _inproj_body,
        grid=(m // tm,),
        in_specs=[row(D_MODEL), _const_spec((1, D_MODEL)), _const_spec(w_in.shape),
                  tab, tab, _const_spec((1, C_WIDTH)), _const_spec((1, C_WIDTH))],
        out_specs=[grp] * 3 + [row(5 * C_WIDTH)],
        out_shape=[jax.ShapeDtypeStruct((3, m, GROUP_W), F32)] * 3
        + [jax.ShapeDtypeStruct((m, 5 * C_WIDTH), F32)],
        compiler_params=_params(1),
        name="inproj",
    )(x2d, norm.reshape(1, D_MODEL), w_in, cos, sin, lb.reshape(1, C_WIDTH), gnorm.reshape(1, C_WIDTH))


def _sample_inproj_body(x_ref, g_ref, w_ref, cos_ref, sin_ref, lb_ref, gn_ref,
                        q_ref, k_ref, v_ref, zc_ref, zb_ref, zg_ref, wbf_ref, z_scr, *, n_blocks):
    j = pl.program_id(0)
    w = w_ref[...].astype(BF16)
    wbf_ref[...] = w
    h = _rmsnorm(x_ref[...], g_ref[...]).astype(BF16)
    z_scr[j] = jnp.dot(h, w, preferred_element_type=F32)

    @pl.when(j == n_blocks - 1)
    def _():
        z = jnp.concatenate([z_scr[i] for i in range(n_blocks)], axis=1)
        _store_qkv_and_hgrn(z[:, 0:COL_B], z[:, COL_C:COL_G], cos_ref[...], sin_ref[...], lb_ref[...],
                            gn_ref[...], False, q_ref, k_ref, v_ref, zc_ref, slice(None))
        zb_ref[...] = z[:, COL_B:COL_C]
        zg_ref[...] = z[:, COL_G:IN_COLS]


def _sample_inproj(x2d, norm, w_in_f32, layer, cos, sin, lb, gnorm, n_blocks=5):
    m = x2d.shape[0]
    bw = IN_COLS // n_blocks
    assert bw * n_blocks == IN_COLS and bw % 128 == 0
    widths = (5 * C_WIDTH, 3 * CONV_WIDTH, N_BRANCH * D_MODEL)
    return pl.pallas_call(
        functools.partial(_sample_inproj_body, n_blocks=n_blocks),
        grid=(n_blocks,),
        in_specs=[_const_spec((m, D_MODEL)), _const_spec((1, D_MODEL)),
                  pl.BlockSpec((None, D_MODEL, bw), lambda j: (layer, 0, j)),
                  _const_spec((m, GROUP_W)), _const_spec((m, GROUP_W)),
                  _const_spec((1, C_WIDTH)), _const_spec((1, C_WIDTH))],
        out_specs=[_const_spec((3, m, GROUP_W))] * 3 + [_const_spec((m, w)) for w in widths]
        + [pl.BlockSpec((D_MODEL, bw), lambda j: (0, j))],
        out_shape=[jax.ShapeDtypeStruct((3, m, GROUP_W), F32)] * 3
        + [jax.ShapeDtypeStruct((m, w), F32) for w in widths]
        + [jax.ShapeDtypeStruct((D_MODEL, IN_COLS), BF16)],
        scratch_shapes=[pltpu.VMEM((n_blocks, m, bw), F32)],
        compiler_params=_params(1),
        name="sample_inproj",
    )(x2d, norm.reshape(1, D_MODEL), w_in_f32, cos, sin, lb.reshape(1, C_WIDTH), gnorm.reshape(1, C_WIDTH))


def _rope_tables(pos):
    half = HEAD_DIM // 2
    inv = ROPE_THETA ** (-jnp.arange(half, dtype=F32) / half)
    ang = pos.astype(F32)[:, None] * inv[None, :]
    cos, sin = jnp.cos(ang), jnp.sin(ang)
    return jnp.tile(cos, (1, 4)), jnp.tile(jnp.concatenate([-sin, sin], axis=1), (1, 2))


def _attend(tiles):
    head0 = lax.broadcasted_iota(jnp.int32, (BAND, GROUP_W), 1) < HEAD_DIM
    scale = HEAD_DIM ** -0.5
    q2 = [jnp.concatenate([jnp.where(head0, q, 0.0), jnp.where(head0, 0.0, q)], axis=0).astype(BF16)
          for q, _ in tiles]
    keys = [jnp.concatenate([k for k, _, _ in blocks], axis=0) for _, blocks in tiles]
    vals = [jnp.concatenate([v1 for _, v1, _ in blocks], axis=0) for _, blocks in tiles]
    valid = [jnp.concatenate([ok for _, _, ok in blocks], axis=1) for _, blocks in tiles]
    s = [jnp.where(ok, _dot_nt(qq, kk) * scale, NEG_BIG) for qq, kk, ok in zip(q2, keys, valid)]
    m = [jnp.max(x, axis=-1, keepdims=True) for x in s]
    p = [jnp.exp(x - mm).astype(BF16) for x, mm in zip(s, m)]
    ext = [jnp.dot(pp, vv, preferred_element_type=F32) for pp, vv in zip(p, vals)]
    results = []
    for e, mm in zip(ext, m):
        den = e[:, GROUP_W:]
        o2 = e[:, 0:GROUP_W] / den
        lse2 = mm + jnp.log(den)
        results.append((jnp.where(head0, o2[0:BAND], o2[BAND:]), jnp.where(head0, lse2[0:BAND], lse2[BAND:])))
    return results


def _attn_prompt_body(q_ref, k_ref, v_ref, ya_ref, c0_ref, c1_ref, c2_ref, o_scr, lse_scr, *, seq):
    qi = lax.broadcasted_iota(jnp.int32, (2 * BAND, BAND), 0) % BAND
    kj = lax.broadcasted_iota(jnp.int32, (2 * BAND, BAND), 1)
    cur_valid = kj <= qi
    prev_valid = kj >= qi
    ones = jnp.ones((BAND, GROUP_W), BF16)

    def rows(start, dil):
        return pl.ds(start, BAND, stride=dil) if dil > 1 else pl.ds(start, BAND)

    def key_block(g, r):
        return k_ref[g, r, :].astype(BF16), jnp.concatenate([v_ref[g, r, :].astype(BF16), ones], axis=1)

    def run(g, dil, chains):
        tiles, slices = [], []
        for starts, first_prev in chains:
            chain = [rows(s, dil) for s in starts]
            keys = [key_block(g, r) for r in chain]
            for j, r in enumerate(chain):
                blocks = [(*keys[j], cur_valid)]
                if j > 0:
                    blocks.append((*keys[j - 1], prev_valid))
                elif first_prev is not None:
                    blocks.append((*key_block(g, rows(first_prev[0], dil)), prev_valid & first_prev[1]))
                tiles.append((q_ref[g, r, :], blocks))
            slices += chain
        for r, (o, lse) in zip(slices, _attend(tiles)):
            o_scr[g, r, :] = o
            lse_scr[g, r, :] = lse

    for g, (_, dil) in enumerate(DSWA_GROUPS):
        n_blocks = seq // dil // BAND
        if dil == 1:
            def dense(i, carry):
                s0 = pl.multiple_of(4 * i * BAND, BAND)
                prev = pl.multiple_of(jnp.maximum(4 * i - 1, 0) * BAND, BAND)
                run(g, dil, [([s0 + u * BAND for u in range(4)], (prev, i > 0))])
                return carry

            lax.fori_loop(0, n_blocks // 4, dense, 0)
        elif n_blocks > 1:
            def residue(r, carry, g=g, dil=dil, n_blocks=n_blocks):
                run(g, dil, [([r + i * BAND * dil for i in range(n_blocks)], None)])
                return carry

            lax.fori_loop(0, dil, residue, 0)
        else:
            def residues(j, carry, g=g, dil=dil):
                run(g, dil, [([4 * j + u], None) for u in range(4)])
                return carry

            lax.fori_loop(0, dil // 4, residues, 0)

    step = 256

    def merge(i, carry):
        sl = pl.ds(pl.multiple_of(i * step, step), step)
        lse = [lse_scr[g, sl, :] for g in range(3)]
        m = jnp.maximum(jnp.maximum(lse[0], lse[1]), lse[2])
        e = [jnp.exp(x - m) for x in lse]
        acc = e[0] * o_scr[0, sl, :] + e[1] * o_scr[1, sl, :] + e[2] * o_scr[2, sl, :]
        ya_ref[0, sl, :] = acc / (e[0] + e[1] + e[2])
        return carry

    lax.fori_loop(0, seq // step, merge, 0)

    for g, (c_ref, (win, _)) in enumerate(zip((c0_ref, c1_ref, c2_ref), DSWA_GROUPS)):
        for j in range(0, win, 512):
            w = min(512, win - j)
            c_ref[0:GROUP_W, j:j + w] = k_ref[g, seq - win + j:seq - win + j + w, :].T
            c_ref[GROUP_W:2 * GROUP_W, j:j + w] = v_ref[g, seq - win + j:seq - win + j + w, :].T


def _attn_prompt(q, k, v, layer, caches):
    _, bsz, seq, _ = q.shape
    depth = 2
    assert seq % (BAND * 16) == 0 and seq >= 2048
    qkv_spec = pl.BlockSpec((3, None, seq, GROUP_W), lambda b: (0, b, 0, 0))
    cache_specs = [pl.BlockSpec((None, None, 2 * GROUP_W, win), lambda b: (layer, b, 0, 0))
                   for win, _ in DSWA_GROUPS]
    cache_shapes = [jax.ShapeDtypeStruct((depth, bsz, 2 * GROUP_W, win), F32) for win, _ in DSWA_GROUPS]
    in_specs = [qkv_spec] * 3
    args = [q, k, v]
    aliases = {}
    if caches is not None:
        in_specs = in_specs + [pl.BlockSpec(memory_space=pl.ANY)] * 3
        args = args + list(caches)
        aliases = {3: 1, 4: 2, 5: 3}

    def body(*refs):
        ins, rest = refs[:3], refs[3 + (3 if caches is not None else 0):]
        _attn_prompt_body(*ins, *rest, seq=seq)

    return pl.pallas_call(
        body,
        grid=(bsz,),
        in_specs=in_specs,
        out_specs=[pl.BlockSpec((1, seq, GROUP_W), lambda b: (b, 0, 0))] + cache_specs,
        out_shape=[jax.ShapeDtypeStruct((bsz, seq, GROUP_W), F32)] + cache_shapes,
        scratch_shapes=[pltpu.VMEM((3, seq, GROUP_W), F32), pltpu.VMEM((3, seq, GROUP_W), F32)],
        input_output_aliases=aliases,
        compiler_params=_params(1),
        name="attn_prompt",
    )(*args)


def _log_forget_and_kgate(zf, lb):
    log_sig = -(jnp.maximum(-zf, 0.0) + jnp.log(1.0 + jnp.exp(-jnp.abs(zf))))
    a = jnp.log(jnp.maximum(lb, LB_FLOOR))
    c = jnp.log1p(-lb) + log_sig
    log_f = jnp.maximum(a, c) + jnp.log(1.0 + jnp.exp(-jnp.abs(a - c)))
    kgate = (1.0 - lb) * jax.nn.sigmoid(-zf)
    return log_f, kgate


def _silu(x):
    return x * jax.nn.sigmoid(x)


def _split3(x):
    hi = x.astype(BF16)
    r1 = x - hi.astype(F32)
    mid = r1.astype(BF16)
    lo = (r1 - mid.astype(F32)).astype(BF16)
    return hi, mid, lo


def _hgrn_inputs(zc, lb, gnorm, chunked):
    q, zf, v, og = (zc[:, i * C_WIDTH:(i + 1) * C_WIDTH] for i in range(4))
    log_f, k = _log_forget_and_kgate(zf, lb)
    if chunked:
        n = zc.shape[0]
        r = lax.broadcasted_iota(jnp.int32, (n, n), 0)
        c = lax.broadcasted_iota(jnp.int32, (n, n), 1)
        tri = jnp.where((c <= r) & ((r // CHUNK) == (c // CHUNK)), 1.0, 0.0).astype(BF16)
        b = functools.reduce(jnp.add, [jnp.dot(tri, p, preferred_element_type=F32) for p in _split3(log_f)])
    else:
        b = log_f
    return q, b, k, v, gnorm * _silu(og)


def _hgrn_chunk(q, b, k, v, gate, st, cst):
    n = CHUNK
    ones_bd = cst["ones_bd"]

    def head_sums(x):
        return jnp.dot(x.astype(BF16), ones_bd, preferred_element_type=F32)

    tiles = (n // SUB, SUB, C_WIDTH)
    q3, k3, b3 = (x.reshape(tiles) for x in (q, k, b))
    f3 = jnp.exp(jnp.minimum(b3 - pltpu.roll(b3, 1, 1), 0.0))
    ys, decay = [q * k], None
    for d in range(1, SUB):
        decay = f3 if d == 1 else decay * pltpu.roll(f3, d - 1, 1)
        ys.append((q3 * pltpu.roll(k3, d, 1) * decay).reshape(n, C_WIDTH))
    att_d = head_sums(jnp.concatenate(ys, axis=0))
    att = functools.reduce(jnp.add, [att_d[d * n:(d + 1) * n, :] * cst["place"][d] for d in range(SUB)])

    for lvl, m in enumerate(LEVELS):
        ref_q, ref_k = [], []
        for j in range(n // m):
            own = b[j * m:(j + 1) * m, :]
            if j % 2 == 1:
                ref_q.append(jnp.broadcast_to(b[j * m - 1:j * m, :], (m, C_WIDTH)))
                ref_k.append(own)
            else:
                ref_q.append(own)
                ref_k.append(jnp.broadcast_to(b[(j + 1) * m - 1:(j + 1) * m, :], (m, C_WIDTH)))
        upper = cst["upper"][lvl]
        qt = q * jnp.exp(jnp.minimum(b - jnp.concatenate(ref_q, axis=0), 0.0)) * upper
        kt = k * jnp.exp(jnp.minimum(jnp.concatenate(ref_k, axis=0) - b, 0.0)) * (1.0 - upper)
        k_stack = jnp.concatenate([kt.astype(BF16)] * C_HEADS, axis=0) * cst["stack"]
        att = att + _dot_nt(qt, k_stack) * cst["same_block"][lvl]
    v_stack = jnp.concatenate([v.astype(BF16)] * C_HEADS, axis=0) * cst["stack"]
    o = jnp.dot(att.astype(BF16), v_stack, preferred_element_type=F32)

    o = o + _dot_nt(q * jnp.exp(b), st)
    b_last = b[n - 1:n, :]
    k_hat = k * jnp.exp(b_last - b)
    st = st * jnp.exp(b_last) + _dot_tn(v, k_hat) * cst["same_head"]

    sq = o * o
    sq_hi = sq.astype(BF16)
    ms = head_sums(sq_hi) + head_sums(sq - sq_hi.astype(F32))
    o = o * lax.rsqrt(ms * (1.0 / 64) + EPS) * gate
    return o, st


def _hgrn_constants():
    n = CHUNK
    row = lax.broadcasted_iota(jnp.int32, (n, C_WIDTH), 0)
    hr = lax.broadcasted_iota(jnp.int32, (C_WIDTH, C_WIDTH), 0) // 64
    hc = lax.broadcasted_iota(jnp.int32, (C_WIDTH, C_WIDTH), 1) // 64
    same_head = jnp.where(hr == hc, 1.0, 0.0)
    srow = lax.broadcasted_iota(jnp.int32, (n, C_HEADS * n), 0)
    scol = lax.broadcasted_iota(jnp.int32, (n, C_HEADS * n), 1) % n
    return {
        "ones_bd": same_head.astype(BF16),
        "same_head": same_head,
        "stack": same_head.astype(BF16),
        "place": [jnp.where((scol == srow - d) & (srow % SUB >= d), 1.0, 0.0) for d in range(SUB)],
        "upper": [jnp.where((row // m) % 2 == 1, 1.0, 0.0) for m in LEVELS],
        "same_block": [jnp.where(srow // (2 * m) == scol // (2 * m), 1.0, 0.0) for m in LEVELS],
    }


def _hgrn_prompt_body(zc_ref, yc_ref, st_ref, st_scr, *, rows, n_seq):
    t = pl.program_id(1)

    @pl.when(t == 0)
    def _():
        st_scr[...] = jnp.zeros_like(st_scr)

    cst = _hgrn_constants()

    def chunk(c, carry):
        sl = pl.ds(pl.multiple_of(c * CHUNK, CHUNK), CHUNK)
        for i in range(n_seq):
            o, st = _hgrn_chunk(*(zc_ref[i, sl, j * C_WIDTH:(j + 1) * C_WIDTH] for j in range(5)),
                                st_scr[i], cst)
            st_scr[i] = st
            yc_ref[i, sl, :] = o
        return carry

    lax.fori_loop(0, rows // CHUNK, chunk, 0)

    @pl.when(t == pl.num_programs(1) - 1)
    def _():
        for i in range(n_seq):
            for h in range(C_HEADS):
                st_ref[i, h] = st_scr[i, h * 64:(h + 1) * 64, h * 64:(h + 1) * 64]


def _hgrn_prompt(zc, rows=ROW_TILE):
    bsz, seq, _ = zc.shape
    n_seq = max(n for n in (4, 2, 1) if bsz % n == 0)
    assert CHUNK == 64
    yc, st = pl.pallas_call(
        functools.partial(_hgrn_prompt_body, rows=rows, n_seq=n_seq),
        grid=(bsz // n_seq, seq // rows),
        in_specs=[pl.BlockSpec((n_seq, rows, 5 * C_WIDTH), lambda b, t: (b, t, 0))],
        out_specs=[pl.BlockSpec((n_seq, rows, C_WIDTH), lambda b, t: (b, t, 0)),
                   pl.BlockSpec((n_seq, C_HEADS, 64, 64), lambda b, t: (b, 0, 0, 0))],
        out_shape=[jax.ShapeDtypeStruct((bsz, seq, C_WIDTH), F32),
                   jax.ShapeDtypeStruct((bsz, C_HEADS, 64, 64), F32)],
        scratch_shapes=[pltpu.VMEM((n_seq, C_WIDTH, C_WIDTH), F32)],
        compiler_params=_params(2),
        name="hgrn_prompt",
    )(zc)
    return yc, jnp.swapaxes(st, -1, -2)


def _gated_merge(x, ya, yb, yc, zg, bg, wpa, wpb, wpc, wo):
    g = jax.nn.sigmoid(zg + bg)
    mix = (g[:, 0:D_MODEL] * _dot(ya, wpa) + g[:, D_MODEL:2 * D_MODEL] * _dot(yb, wpb)
           + g[:, 2 * D_MODEL:3 * D_MODEL] * _dot(yc, wpc))
    return x + _dot(mix, wo)


def _project_conv_and_gates(x, g1, w_ref):
    h = _rmsnorm(x, g1).astype(BF16)
    return (jnp.dot(h, w_ref[:, COL_B:COL_C], preferred_element_type=F32),
            jnp.dot(h, w_ref[:, COL_G:IN_COLS], preferred_element_type=F32))


def _merge_prompt_body(x_ref, ya_ref, yc_ref, g1_ref, w_ref, bg_ref, cw_ref, wpa_ref, wpb_ref, wpc_ref,
                       wo_ref, x1_ref, nc_ref, u_scr, *, rows):
    @pl.when(pl.program_id(1) == 0)
    def _():
        u_scr[0:8, :] = jnp.zeros((8, CONV_WIDTH), F32)

    sub = min(rows, ROW_TILE)
    for r0 in range(0, rows, sub):
        sl = slice(r0, r0 + sub)
        x = x_ref[0, sl, :]
        zb, zg = _project_conv_and_gates(x, g1_ref[...], w_ref)
        u = zb[:, 2 * CONV_WIDTH:3 * CONV_WIDTH] * zb[:, 0:CONV_WIDTH]
        u_scr[8 + r0:8 + r0 + sub, :] = u
        conv = (cw_ref[0:1, :] * u_scr[pl.ds(6 + r0, sub), :] + cw_ref[1:2, :] * u_scr[pl.ds(7 + r0, sub), :]
                + cw_ref[2:3, :] * u)
        yb = zb[:, CONV_WIDTH:2 * CONV_WIDTH] * conv
        x1_ref[0, sl, :] = _gated_merge(x, ya_ref[0, sl, :], yb, yc_ref[0, sl, :], zg, bg_ref[...],
                                        wpa_ref[...], wpb_ref[...], wpc_ref[...], wo_ref[...])
    nc_ref[0] = u_scr[8 + rows - (CONV_K - 1):8 + rows, :]
    u_scr[0:8, :] = u_scr[rows:rows + 8, :]


def _merge_prompt(x, ya, yc, norm1, w_in, b_gate, conv_w, wpa, wpb, wpc, wo, rows=2 * ROW_TILE):
    bsz, seq, _ = x.shape
    blk = lambda w: pl.BlockSpec((1, rows, w), lambda b, t: (b, t, 0))
    return pl.pallas_call(
        functools.partial(_merge_prompt_body, rows=rows),
        grid=(bsz, seq // rows),
        in_specs=[blk(D_MODEL), blk(GROUP_W), blk(C_WIDTH),
                  _const_spec((1, D_MODEL)), _const_spec(w_in.shape),
                  _const_spec((1, N_BRANCH * D_MODEL)), _const_spec((CONV_K, CONV_WIDTH)),
                  _const_spec(wpa.shape), _const_spec(wpb.shape), _const_spec(wpc.shape), _const_spec(wo.shape)],
        out_specs=[blk(D_MODEL), pl.BlockSpec((1, CONV_K - 1, CONV_WIDTH), lambda b, t: (b, 0, 0))],
        out_shape=[jax.ShapeDtypeStruct((bsz, seq, D_MODEL), F32),
                   jax.ShapeDtypeStruct((bsz, CONV_K - 1, CONV_WIDTH), F32)],
        scratch_shapes=[pltpu.VMEM((rows + 8, CONV_WIDTH), F32)],
        compiler_params=_params(2),
        name="merge_prompt",
    )(x, ya, yc, norm1.reshape(1, D_MODEL), w_in, b_gate.reshape(1, -1), conv_w, wpa, wpb, wpc, wo)


def _merge_sample_body(x_ref, ya_ref, yc_ref, zb_ref, zg_ref, p0_ref, p1_ref, bg_ref, cw_ref, wpa_ref,
                       wpb_ref, wpc_ref, wo_ref, x1_ref, u_ref, wpa_bf_ref, wpb_bf_ref, wpc_bf_ref, wo_bf_ref):
    bf = []
    for w_ref, bf_ref in ((wpa_ref, wpa_bf_ref), (wpb_ref, wpb_bf_ref), (wpc_ref, wpc_bf_ref),
                          (wo_ref, wo_bf_ref)):
        bf.append(w_ref[...].astype(BF16))
        bf_ref[...] = bf[-1]
    zb = zb_ref[...]
    u = zb[:, 2 * CONV_WIDTH:3 * CONV_WIDTH] * zb[:, 0:CONV_WIDTH]
    conv = cw_ref[0:1, :] * p0_ref[...] + cw_ref[1:2, :] * p1_ref[...] + cw_ref[2:3, :] * u
    yb = zb[:, CONV_WIDTH:2 * CONV_WIDTH] * conv
    u_ref[...] = u
    x1_ref[...] = _gated_merge(x_ref[...], ya_ref[...], yb, yc_ref[...], zg_ref[...], bg_ref[...], *bf)


def _merge_sample(x, ya, yc, zb, zg, prev0, prev1, b_gate, conv_w, layer, w_pa, w_pb, w_pc, w_o):
    m = x.shape[0]
    small = (x, ya, yc, zb, zg, prev0, prev1, b_gate.reshape(1, -1), conv_w)
    stacks = (w_pa, w_pb, w_pc, w_o)
    return pl.pallas_call(
        _merge_sample_body,
        grid=(1,),
        in_specs=[_const_spec(a.shape) for a in small] + [_layer_spec(w, layer) for w in stacks],
        out_specs=[_const_spec((m, D_MODEL)), _const_spec((m, CONV_WIDTH))]
        + [_const_spec(w.shape[1:]) for w in stacks],
        out_shape=[jax.ShapeDtypeStruct((m, D_MODEL), F32), jax.ShapeDtypeStruct((m, CONV_WIDTH), F32)]
        + [jax.ShapeDtypeStruct(w.shape[1:], BF16) for w in stacks],
        compiler_params=_params(1),
        name="merge_sample",
    )(*small, *stacks)


def _as_column(row):
    return jnp.concatenate([jnp.broadcast_to(row[:, c:c + 128], (128, 128)).T for c in range(0, row.shape[1], 128)],
                           axis=0)


def _shift_window_buffers(k_ref, v_ref, cache_refs, new_refs):
    for g, (c_ref, n_ref, (win, _)) in enumerate(zip(cache_refs, new_refs, DSWA_GROUPS)):
        rolled = pltpu.roll(c_ref[...], win - 1, 1)
        new_col = jnp.concatenate([_as_column(k_ref[g]), _as_column(v_ref[g])], axis=0)
        last = lax.broadcasted_iota(jnp.int32, new_col.shape, 1) == 127
        if win > 128:
            n_ref[:, 0:win - 128] = rolled[:, 0:win - 128]
        n_ref[:, win - 128:win] = jnp.where(last, new_col, rolled[:, win - 128:win])


def _mlp_body(x_ref, g2_ref, wup_ref, wdn_ref, gf_ref, k_ref, v_ref, c0_ref, c1_ref, c2_ref,
              out_ref, n0_ref, n1_ref, n2_ref, *, final):
    _shift_window_buffers(k_ref, v_ref, (c0_ref, c1_ref, c2_ref), (n0_ref, n1_ref, n2_ref))
    tm = x_ref.shape[0]
    sub = min(tm, ROW_TILE)
    for r0 in range(0, tm, sub):
        rows = slice(r0, r0 + sub)
        x = x_ref[rows, :]
        h = _rmsnorm(x, g2_ref[...]).astype(BF16)
        acc = x
        for c in range(D_FF // D_MODEL):
            cols = slice(c * D_MODEL, (c + 1) * D_MODEL)
            a = jnp.maximum(jnp.dot(h, wup_ref[:, cols], preferred_element_type=F32), 0.0)
            acc = acc + jnp.dot((a * a).astype(BF16), wdn_ref[cols, :], preferred_element_type=F32)
        out_ref[rows, :] = _rmsnorm(acc, gf_ref[...]) if final else acc


def _mlp(x2d, norm2, wup, wdn, norm_f, final, tm, k_new, v_new, caches, layer, new_caches):
    m = x2d.shape[0]
    depth, n = caches[0].shape[:2]
    steps = m // tm
    assert n <= steps
    row = pl.BlockSpec((tm, D_MODEL), lambda i: (i, 0))
    seq = lambda i: jnp.minimum(i, n - 1)
    grp = pl.BlockSpec((3, None, 1, GROUP_W), lambda i: (0, seq(i), 0, 0))
    cache_specs = [pl.BlockSpec((None, None, 2 * GROUP_W, win), lambda i: (layer, seq(i), 0, 0))
                   for win, _ in DSWA_GROUPS]
    in_specs = [row, _const_spec((1, D_MODEL)), _const_spec(wup.shape), _const_spec(wdn.shape),
                _const_spec((1, D_MODEL)), grp, grp] + cache_specs
    args = [x2d, norm2.reshape(1, D_MODEL), wup, wdn, norm_f.reshape(1, D_MODEL),
            k_new.reshape(3, n, 1, GROUP_W), v_new.reshape(3, n, 1, GROUP_W), *caches]
    aliases = {}
    if new_caches is not None:
        aliases = {len(args) + i: 1 + i for i in range(3)}
        in_specs = in_specs + [pl.BlockSpec(memory_space=pl.ANY)] * 3
        args = args + list(new_caches)
    n_in = len(cache_specs) + 7

    def body(*refs):
        _mlp_body(*refs[:n_in], *refs[len(args):], final=final)

    return pl.pallas_call(
        body,
        grid=(steps,),
        in_specs=in_specs,
        out_specs=[row] + cache_specs,
        out_shape=[jax.ShapeDtypeStruct((m, D_MODEL), F32)]
        + [jax.ShapeDtypeStruct((depth, n, 2 * GROUP_W, win), F32) for win, _ in DSWA_GROUPS],
        input_output_aliases=aliases,
        compiler_params=_params(1),
        name="mlp",
    )(*args)


def _mlp_sample_body(x_ref, g2_ref, wup_ref, wdn_ref, gf_ref, out_ref, wup_bf_ref, wdn_bf_ref, acc_scr, *, final):
    c = pl.program_id(0)
    wu = wup_ref[...].astype(BF16)
    wd = wdn_ref[...].astype(BF16)
    wup_bf_ref[...] = wu
    wdn_bf_ref[...] = wd
    x = x_ref[...]
    h = _rmsnorm(x, g2_ref[...]).astype(BF16)
    a = jnp.maximum(jnp.dot(h, wu, preferred_element_type=F32), 0.0)
    part = jnp.dot((a * a).astype(BF16), wd, preferred_element_type=F32)

    @pl.when(c == 0)
    def _():
        acc_scr[...] = x + part

    @pl.when(c > 0)
    def _():
        acc_scr[...] = acc_scr[...] + part

    @pl.when(c == pl.num_programs(0) - 1)
    def _():
        acc = acc_scr[...]
        out_ref[...] = _rmsnorm(acc, gf_ref[...]) if final else acc


def _mlp_sample(x2d, norm2, layer, w_up, w_down, norm_f, final):
    m = x2d.shape[0]
    n_chunks = D_FF // D_MODEL
    return pl.pallas_call(
        functools.partial(_mlp_sample_body, final=final),
        grid=(n_chunks,),
        in_specs=[_const_spec((m, D_MODEL)), _const_spec((1, D_MODEL)),
                  pl.BlockSpec((None, D_MODEL, D_MODEL), lambda c: (layer, 0, c)),
                  pl.BlockSpec((None, D_MODEL, D_MODEL), lambda c: (layer, c, 0)),
                  _const_spec((1, D_MODEL))],
        out_specs=[_const_spec((m, D_MODEL)), pl.BlockSpec((D_MODEL, D_MODEL), lambda c: (0, c)),
                   pl.BlockSpec((D_MODEL, D_MODEL), lambda c: (c, 0))],
        out_shape=[jax.ShapeDtypeStruct((m, D_MODEL), F32), jax.ShapeDtypeStruct((D_MODEL, D_FF), BF16),
                   jax.ShapeDtypeStruct((D_FF, D_MODEL), BF16)],
        scratch_shapes=[pltpu.VMEM((m, D_MODEL), F32)],
        compiler_params=_params(1),
        name="mlp_sample",
    )(x2d, norm2.reshape(1, D_MODEL), w_up, w_down, norm_f.reshape(1, D_MODEL))


def _sample_mix_body(q_ref, k_ref, v_ref, c0_ref, c1_ref, c2_ref, zrow_ref, s_ref, ya_ref, yc_ref, s_out_ref):
    lane = lax.broadcasted_iota(jnp.int32, (8, GROUP_W), 1)
    rowi = lax.broadcasted_iota(jnp.int32, (8, GROUP_W), 0)
    head_of_row = (lane // HEAD_DIM) == rowi
    head0 = lax.broadcasted_iota(jnp.int32, (1, GROUP_W), 1) < HEAD_DIM

    outs, lses = [], []
    for g, (c_ref, (win, dil)) in enumerate(zip((c0_ref, c1_ref, c2_ref), DSWA_GROUPS)):
        q, k_new, v_new = q_ref[g], k_ref[g], v_ref[g]
        buf = c_ref[...]
        pos = lax.broadcasted_iota(jnp.int32, (8, win), 1)
        q2 = jnp.where(head_of_row, jnp.broadcast_to(q, (8, GROUP_W)), 0.0)
        scale = HEAD_DIM ** -0.5
        s = jnp.where(pos % dil == 0, _dot(q2, buf[0:GROUP_W, :]) * scale, NEG_BIG)
        s_self = jnp.sum(q2 * k_new, axis=-1, keepdims=True) * scale
        m = jnp.maximum(jnp.max(s, axis=-1, keepdims=True), s_self)
        p, p_self = jnp.exp(s - m), jnp.exp(s_self - m)
        l = jnp.sum(p, axis=-1, keepdims=True) + p_self
        o = (_dot_nt(p, buf[GROUP_W:2 * GROUP_W, :]) + p_self * v_new) / l
        lse = m + jnp.log(l)
        outs.append(jnp.where(head0, o[0:1], o[1:2]))
        lses.append(jnp.where(head0, jnp.broadcast_to(lse[0:1], (1, GROUP_W)),
                              jnp.broadcast_to(lse[1:2], (1, GROUP_W))))
    m = jnp.maximum(jnp.maximum(lses[0], lses[1]), lses[2])
    e = [jnp.exp(x - m) for x in lses]
    ya_ref[0] = (e[0] * outs[0] + e[1] * outs[1] + e[2] * outs[2]) / (e[0] + e[1] + e[2])

    zrow = zrow_ref[0]
    q_col, log_f, k_col = (_as_column(zrow[:, j * C_WIDTH:(j + 1) * C_WIDTH])[:, 0:64] for j in range(3))
    v_row = zrow[:, 3 * C_WIDTH:4 * C_WIDTH]
    gate_row = zrow[:, 4 * C_WIDTH:5 * C_WIDTH]
    v_sel = jnp.concatenate([jnp.broadcast_to(v_row[:, h * 64:(h + 1) * 64], (64, 64)) for h in range(C_HEADS)],
                            axis=0)
    s_new = jnp.exp(log_f) * s_ref[...] + k_col * v_sel
    s_out_ref[...] = s_new
    qs = q_col * s_new
    o4 = jnp.concatenate([jnp.sum(qs[h * 64:(h + 1) * 64, :], axis=0, keepdims=True) for h in range(C_HEADS)],
                         axis=0)
    gate4 = jnp.concatenate([gate_row[:, h * 64:(h + 1) * 64] for h in range(C_HEADS)], axis=0)
    o4 = o4 * lax.rsqrt(jnp.mean(o4 * o4, axis=-1, keepdims=True) + EPS) * gate4
    yc_ref[0] = o4


def _sample_mix(q, k, v, caches, zc, state, layer):
    n = q.shape[1]
    row3 = lambda w: pl.BlockSpec((1, 1, w), lambda b: (b, 0, 0))
    grp = pl.BlockSpec((3, None, 1, GROUP_W), lambda b: (0, b, 0, 0))
    cache_specs = [pl.BlockSpec((None, None, 2 * GROUP_W, win), lambda b: (layer, b, 0, 0))
                   for win, _ in DSWA_GROUPS]
    return pl.pallas_call(
        _sample_mix_body,
        grid=(n,),
        in_specs=([grp] * 3 + cache_specs
                  + [row3(5 * C_WIDTH), pl.BlockSpec((None, None, C_WIDTH, 64), lambda b: (layer, b, 0, 0))]),
        out_specs=[pl.BlockSpec((1, 1, GROUP_W), lambda b: (b, 0, 0)),
                   pl.BlockSpec((1, C_HEADS, 64), lambda b: (b, 0, 0)),
                   pl.BlockSpec((None, C_WIDTH, 64), lambda b: (b, 0, 0))],
        out_shape=[jax.ShapeDtypeStruct((n, 1, GROUP_W), F32), jax.ShapeDtypeStruct((n, C_HEADS, 64), F32),
                   jax.ShapeDtypeStruct((n, C_WIDTH, 64), F32)],
        compiler_params=_params(1),
        name="sample_mix",
    )(q.reshape(3, n, 1, GROUP_W), k.reshape(3, n, 1, GROUP_W), v.reshape(3, n, 1, GROUP_W),
      *caches, zc.reshape(n, 1, 5 * C_WIDTH), state)


def kernel(x_prompt, x_sample, cache_kv_w128, cache_kv_w512, cache_kv_w2048, state_conv, state_hgrn, w_in, b_gate,
           norm1, conv_w, hgrn_lb, hgrn_norm, w_pa, w_pb, w_pc, w_o, norm2, w_up, w_down, norm_f):
    bp, tp, d = x_prompt.shape
    bs, ts, _ = x_sample.shape
    depth = w_in.shape[0]
    assert d == D_MODEL and ts == 1 and depth == 2
    assert PAST_LEN >= max(win for win, _ in DSWA_GROUPS)

    sm = jax.nn.softmax(hgrn_lb.astype(F32), axis=0)
    lower = jnp.cumsum(sm, axis=0) - sm[0:1]

    cos_p, sin_p = _rope_tables(jnp.arange(tp, dtype=jnp.int32))
    cos_s, sin_s = _rope_tables(jnp.full((bs,), PAST_LEN, dtype=jnp.int32))

    caches = [jnp.transpose(c, (0, 1, 3, 4, 5, 2)).reshape(depth, bs, 2 * GROUP_W, win)
              for c, (win, _) in zip((cache_kv_w128, cache_kv_w512, cache_kv_w2048), DSWA_GROUPS)]
    state = state_hgrn.reshape(depth, bs, C_WIDTH, 64)

    xp = x_prompt.reshape(bp * tp, d)
    xs = x_sample.reshape(bs, d)
    kv_p, kv_s = None, None
    conv_p, hgrn_p, conv_s, hgrn_s = [], [], [], []
    for l in range(depth):
        final = l == depth - 1

        q, k_s, v_s, zc, zb, zg, w_in_bf = _sample_inproj(xs, norm1[l], w_in, l, cos_s, sin_s, lower[l],
                                                          hgrn_norm[l])
        ya, yc, s_new = _sample_mix(q, k_s, v_s, caches, zc, state, l)
        x1, u, wpa, wpb, wpc, wo = _merge_sample(xs, ya.reshape(bs, GROUP_W), yc.reshape(bs, C_WIDTH), zb, zg,
                                                 state_conv[l, :, 0], state_conv[l, :, 1], b_gate[l], conv_w[l],
                                                 l, w_pa, w_pb, w_pc, w_o)
        xs, wup, wdn = _mlp_sample(x1, norm2[l], l, w_up, w_down, norm_f, final)
        conv_s.append(jnp.stack([state_conv[l, :, 1], u], axis=1))
        hgrn_s.append(s_new.reshape(bs, C_HEADS, 64, 64))

        q, k, v, zc = _inproj(xp, norm1[l], w_in_bf, cos_p, sin_p, lower[l], hgrn_norm[l], tm=2 * ROW_TILE)
        ya, *kv_p = _attn_prompt(q.reshape(3, bp, tp, -1), k.reshape(3, bp, tp, -1), v.reshape(3, bp, tp, -1),
                                 l, kv_p)
        yc, st = _hgrn_prompt(zc.reshape(bp, tp, -1))
        x1, nc = _merge_prompt(xp.reshape(bp, tp, d), ya, yc, norm1[l], w_in_bf,
                               b_gate[l], conv_w[l], wpa, wpb, wpc, wo)
        xp, *kv_s = _mlp(x1.reshape(bp * tp, d), norm2[l], wup, wdn, norm_f, final, ROW_TILE,
                         k_s, v_s, caches, l, kv_s)
        conv_p.append(nc)
        hgrn_p.append(st)

    def kv6(c, n):
        return jnp.transpose(c.reshape(depth, n, 2, 2, HEAD_DIM, c.shape[3]), (0, 1, 5, 2, 3, 4))

    return (xp.reshape(bp, tp, d), xs.reshape(bs, ts, d),
            kv6(kv_p[0], bp), kv6(kv_p[1], bp), kv6(kv_p[2], bp), jnp.stack(conv_p), jnp.stack(hgrn_p),
            kv6(kv_s[0], bs), kv6(kv_s[1], bs), kv6(kv_s[2], bs), jnp.stack(conv_s), jnp.stack(hgrn_s))
```

```python
import functools

import jax
import jax.numpy as jnp
from jax import lax
from jax.experimental import pallas as pl
from jax.experimental.pallas import tpu as pltpu

F32 = jnp.float32
BF16 = jnp.bfloat16

D_MODEL = 1024
HEAD_DIM = 64
DSWA_GROUPS = ((128, 1), (512, 4), (2048, 16))
BAND = 128
GROUP_W = 2 * HEAD_DIM
A_WIDTH = 3 * GROUP_W
CONV_WIDTH = 384
CONV_K = 3
C_HEADS = 4
C_WIDTH = C_HEADS * 64
N_BRANCH = 3
D_FF = 4 * D_MODEL
ROPE_THETA = 10000.0
EPS = 1e-6
NEG_BIG = -1e30
LB_FLOOR = 1e-30
PAST_LEN = 8192
IN_COLS = 3 * A_WIDTH + 3 * CONV_WIDTH + 4 * C_WIDTH + N_BRANCH * D_MODEL
COL_B = 3 * A_WIDTH
COL_C = COL_B + 3 * CONV_WIDTH
COL_G = COL_C + 4 * C_WIDTH
CHUNK = 64
SUB = 8
LEVELS = (32, 16, 8)
VMEM_LIMIT = 56 * 1024 * 1024
ROW_TILE = 512
SUB_ROWS = 256


def _params(n_grid):
    return pltpu.CompilerParams(dimension_semantics=("arbitrary",) * n_grid,
                                vmem_limit_bytes=VMEM_LIMIT)


def _const_spec(shape):
    nd = len(shape)
    return pl.BlockSpec(shape, lambda *_: (0,) * nd, pipeline_mode=pl.Buffered(1))


def _layer_spec(stacked, layer):
    zeros = (0,) * (stacked.ndim - 1)
    return pl.BlockSpec((None,) + stacked.shape[1:], lambda *_: (layer,) + zeros, pipeline_mode=pl.Buffered(1))


def _dot(a, b):
    return jnp.dot(a.astype(BF16), b.astype(BF16), preferred_element_type=F32)


def _dot_nt(a, b):
    return lax.dot_general(a.astype(BF16), b.astype(BF16), (((1,), (1,)), ((), ())),
                           preferred_element_type=F32)


def _dot_tn(a, b):
    return lax.dot_general(a.astype(BF16), b.astype(BF16), (((0,), (0,)), ((), ())),
                           preferred_element_type=F32)


def _rmsnorm(x, g):
    return x * lax.rsqrt(jnp.mean(x * x, axis=-1, keepdims=True) + EPS) * g


def _store_qkv_and_hgrn(za, zc, cos, sin, lb, gnorm, q_ref, k_ref, v_ref, zc_ref, rows, zl_ref=None):
    chunked = zl_ref is not None
    lane = lax.broadcasted_iota(jnp.int32, cos.shape, 1)
    first_half = (lane % HEAD_DIM) < HEAD_DIM // 2
    for out_ref, c0 in ((q_ref, 0), (k_ref, A_WIDTH)):
        for g in range(3):
            z = za[:, c0 + g * GROUP_W:c0 + (g + 1) * GROUP_W]
            partner = jnp.where(first_half, pltpu.roll(z, GROUP_W - HEAD_DIM // 2, 1),
                                pltpu.roll(z, HEAD_DIM // 2, 1))
            out_ref[g, rows, :] = z * cos + partner * sin
    for g in range(3):
        v_ref[g, rows, :] = za[:, 2 * A_WIDTH + g * GROUP_W:2 * A_WIDTH + (g + 1) * GROUP_W]
    parts = _hgrn_inputs(zc, lb, gnorm, chunked)
    for i, part in enumerate(parts):
        zc_ref[rows, i * C_WIDTH:(i + 1) * C_WIDTH] = part
    if chunked:
        q, b, k = parts[0:3]
        for i, part in enumerate(_hgrn_level_operands(q, b, k)):
            zl_ref[rows, i * C_WIDTH:(i + 1) * C_WIDTH] = part


def _inproj_body(x_ref, g_ref, w_ref, cos_ref, sin_ref, lb_ref, gn_ref, q_ref, k_ref, v_ref, zc_ref, zl_ref):
    tm = x_ref.shape[0]
    sub = min(tm, SUB_ROWS)
    for r0 in range(0, tm, sub):
        rows = slice(r0, r0 + sub)
        h = _rmsnorm(x_ref[rows, :], g_ref[...]).astype(BF16)
        za = jnp.dot(h, w_ref[:, 0:COL_B], preferred_element_type=F32)
        zc = jnp.dot(h, w_ref[:, COL_C:COL_G], preferred_element_type=F32)
        _store_qkv_and_hgrn(za, zc, cos_ref[rows, :], sin_ref[rows, :], lb_ref[...], gn_ref[...],
                            q_ref, k_ref, v_ref, zc_ref, rows, zl_ref)


def _inproj(x2d, norm, w_in, cos, sin, lb, gnorm, tm):
    m = x2d.shape[0]
    assert tm % CHUNK == 0
    n_pos = cos.shape[0] // tm
    row = lambda w: pl.BlockSpec((tm, w), lambda i: (i, 0))
    tab = pl.BlockSpec((tm, GROUP_W), lambda i: (i % n_pos, 0))
    grp = pl.BlockSpec((3, tm, GROUP_W), lambda i: (0, i, 0))
    return pl.pallas_call(
        _inproj_body,
        grid=(m // tm,),
        in_specs=[row(D_MODEL), _const_spec((1, D_MODEL)), _const_spec(w_in.shape),
                  tab, tab, _const_spec((1, C_WIDTH)), _const_spec((1, C_WIDTH))],
        out_specs=[grp] * 3 + [row(5 * C_WIDTH), row(2 * len(LEVELS) * C_WIDTH)],
        out_shape=[jax.ShapeDtypeStruct((3, m, GROUP_W), F32)] * 3
        + [jax.ShapeDtypeStruct((m, 5 * C_WIDTH), F32), jax.ShapeDtypeStruct((m, 2 * len(LEVELS) * C_WIDTH), BF16)],
        compiler_params=_params(1),
        name="inproj",
    )(x2d, norm.reshape(1, D_MODEL), w_in, cos, sin, lb.reshape(1, C_WIDTH), gnorm.reshape(1, C_WIDTH))


def _sample_inproj_body(x_ref, g_ref, w_ref, cos_ref, sin_ref, lb_ref, gn_ref,
                        q_ref, k_ref, v_ref, zc_ref, zb_ref, zg_ref, wbf_ref, z_scr, *, n_blocks):
    j = pl.program_id(0)
    w = w_ref[...].astype(BF16)
    wbf_ref[...] = w
    h = _rmsnorm(x_ref[...], g_ref[...]).astype(BF16)
    z_scr[j] = jnp.dot(h, w, preferred_element_type=F32)

    @pl.when(j == n_blocks - 1)
    def _():
        z = jnp.concatenate([z_scr[i] for i in range(n_blocks)], axis=1)
        _store_qkv_and_hgrn(z[:, 0:COL_B], z[:, COL_C:COL_G], cos_ref[...], sin_ref[...], lb_ref[...],
                            gn_ref[...], q_ref, k_ref, v_ref, zc_ref, slice(None))
        zb_ref[...] = z[:, COL_B:COL_C]
        zg_ref[...] = z[:, COL_G:IN_COLS]


def _sample_inproj(x2d, norm, w_in_f32, layer, cos, sin, lb, gnorm, n_blocks=5):
    m = x2d.shape[0]
    bw = IN_COLS // n_blocks
    assert bw * n_blocks == IN_COLS and bw % 128 == 0
    widths = (5 * C_WIDTH, 3 * CONV_WIDTH, N_BRANCH * D_MODEL)
    return pl.pallas_call(
        functools.partial(_sample_inproj_body, n_blocks=n_blocks),
        grid=(n_blocks,),
        in_specs=[_const_spec((m, D_MODEL)), _const_spec((1, D_MODEL)),
                  pl.BlockSpec((None, D_MODEL, bw), lambda j: (layer, 0, j)),
                  _const_spec((m, GROUP_W)), _const_spec((m, GROUP_W)),
                  _const_spec((1, C_WIDTH)), _const_spec((1, C_WIDTH))],
        out_specs=[_const_spec((3, m, GROUP_W))] * 3 + [_const_spec((m, w)) for w in widths]
        + [pl.BlockSpec((D_MODEL, bw), lambda j: (0, j))],
        out_shape=[jax.ShapeDtypeStruct((3, m, GROUP_W), F32)] * 3
        + [jax.ShapeDtypeStruct((m, w), F32) for w in widths]
        + [jax.ShapeDtypeStruct((D_MODEL, IN_COLS), BF16)],
        scratch_shapes=[pltpu.VMEM((n_blocks, m, bw), F32)],
        compiler_params=_params(1),
        name="sample_inproj",
    )(x2d, norm.reshape(1, D_MODEL), w_in_f32, cos, sin, lb.reshape(1, C_WIDTH), gnorm.reshape(1, C_WIDTH))


def _rope_tables(pos):
    half = HEAD_DIM // 2
    inv = ROPE_THETA ** (-jnp.arange(half, dtype=F32) / half)
    ang = pos.astype(F32)[:, None] * inv[None, :]
    cos, sin = jnp.cos(ang), jnp.sin(ang)
    return jnp.tile(cos, (1, 4)), jnp.tile(jnp.concatenate([-sin, sin], axis=1), (1, 2))


def _attend(tiles):
    head0 = lax.broadcasted_iota(jnp.int32, (BAND, GROUP_W), 1) < HEAD_DIM
    scale = HEAD_DIM ** -0.5
    q2 = [jnp.concatenate([jnp.where(head0, q, 0.0), jnp.where(head0, 0.0, q)], axis=0).astype(BF16)
          for q, _ in tiles]
    keys = [jnp.concatenate([k for k, _, _ in blocks], axis=0) for _, blocks in tiles]
    vals = [jnp.concatenate([v1 for _, v1, _ in blocks], axis=0) for _, blocks in tiles]
    valid = [jnp.concatenate([ok for _, _, ok in blocks], axis=1) for _, blocks in tiles]
    s = [jnp.where(ok, _dot_nt(qq, kk) * scale, NEG_BIG) for qq, kk, ok in zip(q2, keys, valid)]
    m = [jnp.max(x, axis=-1, keepdims=True) for x in s]
    p = [jnp.exp(x - mm).astype(BF16) for x, mm in zip(s, m)]
    ext = [jnp.dot(pp, vv, preferred_element_type=F32) for pp, vv in zip(p, vals)]
    results = []
    for e, mm in zip(ext, m):
        den = e[:, GROUP_W:]
        o2 = e[:, 0:GROUP_W] / den
        lse2 = mm + jnp.log(den)
        results.append((jnp.where(head0, o2[0:BAND], o2[BAND:]), jnp.where(head0, lse2[0:BAND], lse2[BAND:])))
    return results


def _attn_prompt_body(q_ref, k_ref, v_ref, ya_ref, c0_ref, c1_ref, c2_ref, o_scr, lse_scr, *, seq):
    qi = lax.broadcasted_iota(jnp.int32, (2 * BAND, BAND), 0) % BAND
    kj = lax.broadcasted_iota(jnp.int32, (2 * BAND, BAND), 1)
    cur_valid = kj <= qi
    prev_valid = kj >= qi
    ones = jnp.ones((BAND, GROUP_W), BF16)

    def rows(start, dil):
        return pl.ds(start, BAND, stride=dil) if dil > 1 else pl.ds(start, BAND)

    def key_block(g, r):
        return k_ref[g, r, :].astype(BF16), jnp.concatenate([v_ref[g, r, :].astype(BF16), ones], axis=1)

    def run(g, dil, chains):
        tiles, slices = [], []
        for starts, first_prev in chains:
            chain = [rows(s, dil) for s in starts]
            keys = [key_block(g, r) for r in chain]
            for j, r in enumerate(chain):
                blocks = [(*keys[j], cur_valid)]
                if j > 0:
                    blocks.append((*keys[j - 1], prev_valid))
                elif first_prev is not None:
                    blocks.append((*key_block(g, rows(first_prev[0], dil)), prev_valid & first_prev[1]))
                tiles.append((q_ref[g, r, :], blocks))
            slices += chain
        for r, (o, lse) in zip(slices, _attend(tiles)):
            o_scr[g, r, :] = o
            lse_scr[g, r, :] = lse

    for g, (_, dil) in enumerate(DSWA_GROUPS):
        n_blocks = seq // dil // BAND
        if dil == 1:
            def dense(i, carry):
                s0 = pl.multiple_of(4 * i * BAND, BAND)
                prev = pl.multiple_of(jnp.maximum(4 * i - 1, 0) * BAND, BAND)
                run(g, dil, [([s0 + u * BAND for u in range(4)], (prev, i > 0))])
                return carry

            lax.fori_loop(0, n_blocks // 4, dense, 0)
        elif n_blocks > 1:
            def residue(r, carry, g=g, dil=dil, n_blocks=n_blocks):
                run(g, dil, [([r + i * BAND * dil for i in range(n_blocks)], None)])
                return carry

            lax.fori_loop(0, dil, residue, 0)
        else:
            def residues(j, carry, g=g, dil=dil):
                run(g, dil, [([4 * j + u], None) for u in range(4)])
                return carry

            lax.fori_loop(0, dil // 4, residues, 0)

    step = 256

    def merge(i, carry):
        sl = pl.ds(pl.multiple_of(i * step, step), step)
        lse = [lse_scr[g, sl, :] for g in range(3)]
        m = jnp.maximum(jnp.maximum(lse[0], lse[1]), lse[2])
        e = [jnp.exp(x - m) for x in lse]
        acc = e[0] * o_scr[0, sl, :] + e[1] * o_scr[1, sl, :] + e[2] * o_scr[2, sl, :]
        ya_ref[0, sl, :] = acc / (e[0] + e[1] + e[2])
        return carry

    lax.fori_loop(0, seq // step, merge, 0)

    for g, (c_ref, (win, _)) in enumerate(zip((c0_ref, c1_ref, c2_ref), DSWA_GROUPS)):
        for j in range(0, win, 512):
            w = min(512, win - j)
            c_ref[0:GROUP_W, j:j + w] = k_ref[g, seq - win + j:seq - win + j + w, :].T
            c_ref[GROUP_W:2 * GROUP_W, j:j + w] = v_ref[g, seq - win + j:seq - win + j + w, :].T


def _attn_prompt(q, k, v, layer, caches):
    _, bsz, seq, _ = q.shape
    depth = 2
    assert seq % (BAND * 16) == 0 and seq >= 2048
    qkv_spec = pl.BlockSpec((3, None, seq, GROUP_W), lambda b: (0, b, 0, 0))
    cache_specs = [pl.BlockSpec((None, None, 2 * GROUP_W, win), lambda b: (layer, b, 0, 0))
                   for win, _ in DSWA_GROUPS]
    cache_shapes = [jax.ShapeDtypeStruct((depth, bsz, 2 * GROUP_W, win), F32) for win, _ in DSWA_GROUPS]
    in_specs = [qkv_spec] * 3
    args = [q, k, v]
    aliases = {}
    if caches is not None:
        in_specs = in_specs + [pl.BlockSpec(memory_space=pl.ANY)] * 3
        args = args + list(caches)
        aliases = {3: 1, 4: 2, 5: 3}

    def body(*refs):
        ins, rest = refs[:3], refs[3 + (3 if caches is not None else 0):]
        _attn_prompt_body(*ins, *rest, seq=seq)

    return pl.pallas_call(
        body,
        grid=(bsz,),
        in_specs=in_specs,
        out_specs=[pl.BlockSpec((1, seq, GROUP_W), lambda b: (b, 0, 0))] + cache_specs,
        out_shape=[jax.ShapeDtypeStruct((bsz, seq, GROUP_W), F32)] + cache_shapes,
        scratch_shapes=[pltpu.VMEM((3, seq, GROUP_W), F32), pltpu.VMEM((3, seq, GROUP_W), F32)],
        input_output_aliases=aliases,
        compiler_params=_params(1),
        name="attn_prompt",
    )(*args)


def _log_forget_and_kgate(zf, lb):
    log_sig = -(jnp.maximum(-zf, 0.0) + jnp.log(1.0 + jnp.exp(-jnp.abs(zf))))
    a = jnp.log(jnp.maximum(lb, LB_FLOOR))
    c = jnp.log1p(-lb) + log_sig
    log_f = jnp.maximum(a, c) + jnp.log(1.0 + jnp.exp(-jnp.abs(a - c)))
    kgate = (1.0 - lb) * jax.nn.sigmoid(-zf)
    return log_f, kgate


def _silu(x):
    return x * jax.nn.sigmoid(x)


def _split3(x):
    hi = x.astype(BF16)
    r1 = x - hi.astype(F32)
    mid = r1.astype(BF16)
    lo = (r1 - mid.astype(F32)).astype(BF16)
    return hi, mid, lo


def _hgrn_inputs(zc, lb, gnorm, chunked):
    q, zf, v, og = (zc[:, i * C_WIDTH:(i + 1) * C_WIDTH] for i in range(4))
    log_f, k = _log_forget_and_kgate(zf, lb)
    if chunked:
        n = zc.shape[0]
        r = lax.broadcasted_iota(jnp.int32, (n, n), 0)
        c = lax.broadcasted_iota(jnp.int32, (n, n), 1)
        tri = jnp.where((c <= r) & ((r // CHUNK) == (c // CHUNK)), 1.0, 0.0).astype(BF16)
        b = functools.reduce(jnp.add, [jnp.dot(tri, p, preferred_element_type=F32) for p in _split3(log_f)])
    else:
        b = log_f
    return q, b, k, v, gnorm * _silu(og)


def _hgrn_level_operands(q, b, k):
    n = q.shape[0]
    row = lax.broadcasted_iota(jnp.int32, (n, C_WIDTH), 0)
    out = []
    for m in LEVELS:
        ref_q, ref_k = [], []
        for j in range(n // m):
            own = b[j * m:(j + 1) * m, :]
            if j % 2 == 1:
                ref_q.append(jnp.broadcast_to(b[j * m - 1:j * m, :], (m, C_WIDTH)))
                ref_k.append(own)
            else:
                ref_q.append(own)
                ref_k.append(jnp.broadcast_to(b[(j + 1) * m - 1:(j + 1) * m, :], (m, C_WIDTH)))
        upper = (row // m) % 2 == 1
        qt = jnp.where(upper, q * jnp.exp(jnp.minimum(b - jnp.concatenate(ref_q, axis=0), 0.0)), 0.0)
        kt = jnp.where(upper, 0.0, k * jnp.exp(jnp.minimum(jnp.concatenate(ref_k, axis=0) - b, 0.0)))
        out += [qt.astype(BF16), kt.astype(BF16)]
    return out


def _hgrn_chunks(seqs, cst):
    n = CHUNK
    ns = len(seqs)
    ones_bd = cst["ones_bd"]

    def head_sums(x):
        return jnp.dot(x.astype(BF16), ones_bd, preferred_element_type=F32)

    tiles = (n // SUB, SUB, C_WIDTH)
    ys = []
    for q, b, k, *_ in seqs:
        q3, k3, b3 = (x.reshape(tiles) for x in (q, k, b))
        f3 = jnp.exp(jnp.minimum(b3 - pltpu.roll(b3, 1, 1), 0.0))
        ys.append((q * k).astype(BF16))
        decay = None
        for d in range(1, SUB):
            decay = f3 if d == 1 else decay * pltpu.roll(f3, d - 1, 1)
            ys.append((q3 * pltpu.roll(k3, d, 1) * decay).reshape(n, C_WIDTH).astype(BF16))
    att_d = head_sums(jnp.concatenate(ys, axis=0))
    att = [functools.reduce(jnp.add, [att_d[(i * SUB + d) * n:(i * SUB + d + 1) * n, :] * cst["place"][d]
                                      for d in range(SUB)]) for i in range(ns)]

    for lvl in range(len(LEVELS)):
        k_stack = [jnp.concatenate([s[5][2 * lvl + 1]] * C_HEADS, axis=0) * cst["stack"] for s in seqs]
        a = [_dot_nt(s[5][2 * lvl], ks) for s, ks in zip(seqs, k_stack)]
        att = [x + y * cst["same_block"][lvl] for x, y in zip(att, a)]
    v_stack = [jnp.concatenate([s[3].astype(BF16)] * C_HEADS, axis=0) * cst["stack"] for s in seqs]
    q_hat = [s[0] * jnp.exp(s[1]) for s in seqs]
    b_last = [s[1][n - 1:n, :] for s in seqs]
    k_hat = [s[2] * jnp.exp(bl - s[1]) for s, bl in zip(seqs, b_last)]

    o = [jnp.dot(x.astype(BF16), vs, preferred_element_type=F32) for x, vs in zip(att, v_stack)]
    o_st = [_dot_nt(qh, s[6]) for qh, s in zip(q_hat, seqs)]
    upd = [_dot_tn(s[3], kh) for s, kh in zip(seqs, k_hat)]
    o = [x + y for x, y in zip(o, o_st)]
    st = [s[6] * jnp.exp(bl) + u * cst["same_head"] for s, bl, u in zip(seqs, b_last, upd)]

    sq = [x * x for x in o]
    sq_hi = [x.astype(BF16) for x in sq]
    sq_lo = [(x - h.astype(F32)).astype(BF16) for x, h in zip(sq, sq_hi)]
    ms = head_sums(jnp.concatenate(sq_hi + sq_lo, axis=0))
    out = []
    for i, (x, s) in enumerate(zip(o, seqs)):
        m2 = ms[i * n:(i + 1) * n, :] + ms[(ns + i) * n:(ns + i + 1) * n, :]
        out.append((x * lax.rsqrt(m2 * (1.0 / 64) + EPS) * s[4], st[i]))
    return out


def _hgrn_constants():
    n = CHUNK
    hr = lax.broadcasted_iota(jnp.int32, (C_WIDTH, C_WIDTH), 0) // 64
    hc = lax.broadcasted_iota(jnp.int32, (C_WIDTH, C_WIDTH), 1) // 64
    same_head = jnp.where(hr == hc, 1.0, 0.0)
    srow = lax.broadcasted_iota(jnp.int32, (n, C_HEADS * n), 0)
    scol = lax.broadcasted_iota(jnp.int32, (n, C_HEADS * n), 1) % n
    return {
        "ones_bd": same_head.astype(BF16),
        "same_head": same_head,
        "stack": same_head.astype(BF16),
        "place": [jnp.where((scol == srow - d) & (srow % SUB >= d), 1.0, 0.0) for d in range(SUB)],
        "same_block": [jnp.where(srow // (2 * m) == scol // (2 * m), 1.0, 0.0) for m in LEVELS],
    }


def _hgrn_prompt_body(zc_ref, zl_ref, yc_ref, st_ref, st_scr, *, rows, n_seq):
    t = pl.program_id(1)

    @pl.when(t == 0)
    def _():
        st_scr[...] = jnp.zeros_like(st_scr)

    cst = _hgrn_constants()

    def chunk(c, carry):
        sl = pl.ds(pl.multiple_of(c * CHUNK, CHUNK), CHUNK)
        seqs = [(*(zc_ref[i, sl, j * C_WIDTH:(j + 1) * C_WIDTH] for j in range(5)),
                 [zl_ref[i, sl, j * C_WIDTH:(j + 1) * C_WIDTH] for j in range(2 * len(LEVELS))], st_scr[i])
                for i in range(n_seq)]
        for i, (o, st) in enumerate(_hgrn_chunks(seqs, cst)):
            st_scr[i] = st
            yc_ref[i, sl, :] = o
        return carry

    lax.fori_loop(0, rows // CHUNK, chunk, 0)

    @pl.when(t == pl.num_programs(1) - 1)
    def _():
        for i in range(n_seq):
            for h in range(C_HEADS):
                st_ref[i, h] = st_scr[i, h * 64:(h + 1) * 64, h * 64:(h + 1) * 64]


def _hgrn_prompt(zc, zl, rows=ROW_TILE // 2):
    bsz, seq, _ = zc.shape
    n_seq = max(n for n in (8, 4, 2, 1) if bsz % n == 0)
    assert CHUNK == 64
    yc, st = pl.pallas_call(
        functools.partial(_hgrn_prompt_body, rows=rows, n_seq=n_seq),
        grid=(bsz // n_seq, seq // rows),
        in_specs=[pl.BlockSpec((n_seq, rows, 5 * C_WIDTH), lambda b, t: (b, t, 0)),
                  pl.BlockSpec((n_seq, rows, 2 * len(LEVELS) * C_WIDTH), lambda b, t: (b, t, 0))],
        out_specs=[pl.BlockSpec((n_seq, rows, C_WIDTH), lambda b, t: (b, t, 0)),
                   pl.BlockSpec((n_seq, C_HEADS, 64, 64), lambda b, t: (b, 0, 0, 0))],
        out_shape=[jax.ShapeDtypeStruct((bsz, seq, C_WIDTH), F32),
                   jax.ShapeDtypeStruct((bsz, C_HEADS, 64, 64), F32)],
        scratch_shapes=[pltpu.VMEM((n_seq, C_WIDTH, C_WIDTH), F32)],
        compiler_params=_params(2),
        name="hgrn_prompt",
    )(zc, zl)
    return yc, jnp.swapaxes(st, -1, -2)


def _gated_merge(x, ya, yb, yc, zg, bg, wpa, wpb, wpc, wo):
    g = jax.nn.sigmoid(zg + bg)
    mix = (g[:, 0:D_MODEL] * _dot(ya, wpa) + g[:, D_MODEL:2 * D_MODEL] * _dot(yb, wpb)
           + g[:, 2 * D_MODEL:3 * D_MODEL] * _dot(yc, wpc))
    return x + _dot(mix, wo)


def _project_conv_and_gates(x, g1, w_ref):
    h = _rmsnorm(x, g1).astype(BF16)
    return (jnp.dot(h, w_ref[:, COL_B:COL_C], preferred_element_type=F32),
            jnp.dot(h, w_ref[:, COL_G:IN_COLS], preferred_element_type=F32))


def _merge_prompt_body(x_ref, ya_ref, yc_ref, g1_ref, w_ref, bg_ref, cw_ref, wpa_ref, wpb_ref, wpc_ref,
                       wo_ref, x1_ref, nc_ref, u_scr, *, rows):
    @pl.when(pl.program_id(1) == 0)
    def _():
        u_scr[0:8, :] = jnp.zeros((8, CONV_WIDTH), F32)

    sub = min(rows, ROW_TILE)
    for r0 in range(0, rows, sub):
        sl = slice(r0, r0 + sub)
        x = x_ref[0, sl, :]
        zb, zg = _project_conv_and_gates(x, g1_ref[...], w_ref)
        u = zb[:, 2 * CONV_WIDTH:3 * CONV_WIDTH] * zb[:, 0:CONV_WIDTH]
        u_scr[8 + r0:8 + r0 + sub, :] = u
        conv = (cw_ref[0:1, :] * u_scr[pl.ds(6 + r0, sub), :] + cw_ref[1:2, :] * u_scr[pl.ds(7 + r0, sub), :]
                + cw_ref[2:3, :] * u)
        yb = zb[:, CONV_WIDTH:2 * CONV_WIDTH] * conv
        x1_ref[0, sl, :] = _gated_merge(x, ya_ref[0, sl, :], yb, yc_ref[0, sl, :], zg, bg_ref[...],
                                        wpa_ref[...], wpb_ref[...], wpc_ref[...], wo_ref[...])
    nc_ref[0] = u_scr[8 + rows - (CONV_K - 1):8 + rows, :]
    u_scr[0:8, :] = u_scr[rows:rows + 8, :]


def _merge_prompt(x, ya, yc, norm1, w_in, b_gate, conv_w, wpa, wpb, wpc, wo, rows=2 * ROW_TILE):
    bsz, seq, _ = x.shape
    blk = lambda w: pl.BlockSpec((1, rows, w), lambda b, t: (b, t, 0))
    return pl.pallas_call(
        functools.partial(_merge_prompt_body, rows=rows),
        grid=(bsz, seq // rows),
        in_specs=[blk(D_MODEL), blk(GROUP_W), blk(C_WIDTH),
                  _const_spec((1, D_MODEL)), _const_spec(w_in.shape),
                  _const_spec((1, N_BRANCH * D_MODEL)), _const_spec((CONV_K, CONV_WIDTH)),
                  _const_spec(wpa.shape), _const_spec(wpb.shape), _const_spec(wpc.shape), _const_spec(wo.shape)],
        out_specs=[blk(D_MODEL), pl.BlockSpec((1, CONV_K - 1, CONV_WIDTH), lambda b, t: (b, 0, 0))],
        out_shape=[jax.ShapeDtypeStruct((bsz, seq, D_MODEL), F32),
                   jax.ShapeDtypeStruct((bsz, CONV_K - 1, CONV_WIDTH), F32)],
        scratch_shapes=[pltpu.VMEM((rows + 8, CONV_WIDTH), F32)],
        compiler_params=_params(2),
        name="merge_prompt",
    )(x, ya, yc, norm1.reshape(1, D_MODEL), w_in, b_gate.reshape(1, -1), conv_w, wpa, wpb, wpc, wo)


def _merge_sample_body(x_ref, ya_ref, yc_ref, zb_ref, zg_ref, p0_ref, p1_ref, bg_ref, cw_ref, wpa_ref,
                       wpb_ref, wpc_ref, wo_ref, x1_ref, u_ref, wpa_bf_ref, wpb_bf_ref, wpc_bf_ref, wo_bf_ref):
    bf = []
    for w_ref, bf_ref in ((wpa_ref, wpa_bf_ref), (wpb_ref, wpb_bf_ref), (wpc_ref, wpc_bf_ref),
                          (wo_ref, wo_bf_ref)):
        bf.append(w_ref[...].astype(BF16))
        bf_ref[...] = bf[-1]
    zb = zb_ref[...]
    u = zb[:, 2 * CONV_WIDTH:3 * CONV_WIDTH] * zb[:, 0:CONV_WIDTH]
    conv = cw_ref[0:1, :] * p0_ref[...] + cw_ref[1:2, :] * p1_ref[...] + cw_ref[2:3, :] * u
    yb = zb[:, CONV_WIDTH:2 * CONV_WIDTH] * conv
    u_ref[...] = u
    x1_ref[...] = _gated_merge(x_ref[...], ya_ref[...], yb, yc_ref[...], zg_ref[...], bg_ref[...], *bf)


def _merge_sample(x, ya, yc, zb, zg, prev0, prev1, b_gate, conv_w, layer, w_pa, w_pb, w_pc, w_o):
    m = x.shape[0]
    small = (x, ya, yc, zb, zg, prev0, prev1, b_gate.reshape(1, -1), conv_w)
    stacks = (w_pa, w_pb, w_pc, w_o)
    return pl.pallas_call(
        _merge_sample_body,
        grid=(1,),
        in_specs=[_const_spec(a.shape) for a in small] + [_layer_spec(w, layer) for w in stacks],
        out_specs=[_const_spec((m, D_MODEL)), _const_spec((m, CONV_WIDTH))]
        + [_const_spec(w.shape[1:]) for w in stacks],
        out_shape=[jax.ShapeDtypeStruct((m, D_MODEL), F32), jax.ShapeDtypeStruct((m, CONV_WIDTH), F32)]
        + [jax.ShapeDtypeStruct(w.shape[1:], BF16) for w in stacks],
        compiler_params=_params(1),
        name="merge_sample",
    )(*small, *stacks)


def _as_column(row):
    return jnp.concatenate([jnp.broadcast_to(row[:, c:c + 128], (128, 128)).T for c in range(0, row.shape[1], 128)],
                           axis=0)


def _shift_window_buffers(k_ref, v_ref, cache_refs, new_refs):
    for g, (c_ref, n_ref, (win, _)) in enumerate(zip(cache_refs, new_refs, DSWA_GROUPS)):
        rolled = pltpu.roll(c_ref[...], win - 1, 1)
        new_col = jnp.concatenate([_as_column(k_ref[g]), _as_column(v_ref[g])], axis=0)
        last = lax.broadcasted_iota(jnp.int32, new_col.shape, 1) == 127
        if win > 128:
            n_ref[:, 0:win - 128] = rolled[:, 0:win - 128]
        n_ref[:, win - 128:win] = jnp.where(last, new_col, rolled[:, win - 128:win])


def _mlp_body(x_ref, g2_ref, wup_ref, wdn_ref, gf_ref, k_ref, v_ref, c0_ref, c1_ref, c2_ref,
              out_ref, n0_ref, n1_ref, n2_ref, *, final):
    _shift_window_buffers(k_ref, v_ref, (c0_ref, c1_ref, c2_ref), (n0_ref, n1_ref, n2_ref))
    tm = x_ref.shape[0]
    sub = min(tm, ROW_TILE)
    for r0 in range(0, tm, sub):
        rows = slice(r0, r0 + sub)
        x = x_ref[rows, :]
        h = _rmsnorm(x, g2_ref[...]).astype(BF16)
        acc = x
        for c in range(D_FF // D_MODEL):
            cols = slice(c * D_MODEL, (c + 1) * D_MODEL)
            a = jnp.maximum(jnp.dot(h, wup_ref[:, cols], preferred_element_type=F32), 0.0)
            acc = acc + jnp.dot((a * a).astype(BF16), wdn_ref[cols, :], preferred_element_type=F32)
        out_ref[rows, :] = _rmsnorm(acc, gf_ref[...]) if final else acc


def _mlp(x2d, norm2, wup, wdn, norm_f, final, tm, k_new, v_new, caches, layer, new_caches):
    m = x2d.shape[0]
    depth, n = caches[0].shape[:2]
    steps = m // tm
    assert n <= steps
    row = pl.BlockSpec((tm, D_MODEL), lambda i: (i, 0))
    seq = lambda i: jnp.minimum(i, n - 1)
    grp = pl.BlockSpec((3, None, 1, GROUP_W), lambda i: (0, seq(i), 0, 0))
    cache_specs = [pl.BlockSpec((None, None, 2 * GROUP_W, win), lambda i: (layer, seq(i), 0, 0))
                   for win, _ in DSWA_GROUPS]
    in_specs = [row, _const_spec((1, D_MODEL)), _const_spec(wup.shape), _const_spec(wdn.shape),
                _const_spec((1, D_MODEL)), grp, grp] + cache_specs
    args = [x2d, norm2.reshape(1, D_MODEL), wup, wdn, norm_f.reshape(1, D_MODEL),
            k_new.reshape(3, n, 1, GROUP_W), v_new.reshape(3, n, 1, GROUP_W), *caches]
    aliases = {}
    if new_caches is not None:
        aliases = {len(args) + i: 1 + i for i in range(3)}
        in_specs = in_specs + [pl.BlockSpec(memory_space=pl.ANY)] * 3
        args = args + list(new_caches)
    n_in = len(cache_specs) + 7

    def body(*refs):
        _mlp_body(*refs[:n_in], *refs[len(args):], final=final)

    return pl.pallas_call(
        body,
        grid=(steps,),
        in_specs=in_specs,
        out_specs=[row] + cache_specs,
        out_shape=[jax.ShapeDtypeStruct((m, D_MODEL), F32)]
        + [jax.ShapeDtypeStruct((depth, n, 2 * GROUP_W, win), F32) for win, _ in DSWA_GROUPS],
        input_output_aliases=aliases,
        compiler_params=_params(1),
        name="mlp",
    )(*args)


def _mlp_sample_body(x_ref, g2_ref, wup_ref, wdn_ref, gf_ref, out_ref, wup_bf_ref, wdn_bf_ref, acc_scr, *, final):
    c = pl.program_id(0)
    wu = wup_ref[...].astype(BF16)
    wd = wdn_ref[...].astype(BF16)
    wup_bf_ref[...] = wu
    wdn_bf_ref[...] = wd
    x = x_ref[...]
    h = _rmsnorm(x, g2_ref[...]).astype(BF16)
    a = jnp.maximum(jnp.dot(h, wu, preferred_element_type=F32), 0.0)
    part = jnp.dot((a * a).astype(BF16), wd, preferred_element_type=F32)

    @pl.when(c == 0)
    def _():
        acc_scr[...] = x + part

    @pl.when(c > 0)
    def _():
        acc_scr[...] = acc_scr[...] + part

    @pl.when(c == pl.num_programs(0) - 1)
    def _():
        acc = acc_scr[...]
        out_ref[...] = _rmsnorm(acc, gf_ref[...]) if final else acc


def _mlp_sample(x2d, norm2, layer, w_up, w_down, norm_f, final):
    m = x2d.shape[0]
    n_chunks = D_FF // D_MODEL
    return pl.pallas_call(
        functools.partial(_mlp_sample_body, final=final),
        grid=(n_chunks,),
        in_specs=[_const_spec((m, D_MODEL)), _const_spec((1, D_MODEL)),
                  pl.BlockSpec((None, D_MODEL, D_MODEL), lambda c: (layer, 0, c)),
                  pl.BlockSpec((None, D_MODEL, D_MODEL), lambda c: (layer, c, 0)),
                  _const_spec((1, D_MODEL))],
        out_specs=[_const_spec((m, D_MODEL)), pl.BlockSpec((D_MODEL, D_MODEL), lambda c: (0, c)),
                   pl.BlockSpec((D_MODEL, D_MODEL), lambda c: (c, 0))],
        out_shape=[jax.ShapeDtypeStruct((m, D_MODEL), F32), jax.ShapeDtypeStruct((D_MODEL, D_FF), BF16),
                   jax.ShapeDtypeStruct((D_FF, D_MODEL), BF16)],
        scratch_shapes=[pltpu.VMEM((m, D_MODEL), F32)],
        compiler_params=_params(1),
        name="mlp_sample",
    )(x2d, norm2.reshape(1, D_MODEL), w_up, w_down, norm_f.reshape(1, D_MODEL))


def _sample_mix_body(q_ref, k_ref, v_ref, c0_ref, c1_ref, c2_ref, zrow_ref, s_ref, ya_ref, yc_ref, s_out_ref):
    for i in range(q_ref.shape[1]):
        _sample_mix_one(q_ref.at[:, i], k_ref.at[:, i], v_ref.at[:, i], c0_ref.at[i], c1_ref.at[i], c2_ref.at[i],
                        zrow_ref.at[i], s_ref.at[i], ya_ref.at[i], yc_ref.at[i], s_out_ref.at[i])


def _sample_mix_one(q_ref, k_ref, v_ref, c0_ref, c1_ref, c2_ref, zrow_ref, s_ref, ya_ref, yc_ref, s_out_ref):
    lane = lax.broadcasted_iota(jnp.int32, (8, GROUP_W), 1)
    rowi = lax.broadcasted_iota(jnp.int32, (8, GROUP_W), 0)
    head_of_row = (lane // HEAD_DIM) == rowi
    head0 = lax.broadcasted_iota(jnp.int32, (1, GROUP_W), 1) < HEAD_DIM

    outs, lses = [], []
    for g, (c_ref, (win, dil)) in enumerate(zip((c0_ref, c1_ref, c2_ref), DSWA_GROUPS)):
        q, k_new, v_new = q_ref[g], k_ref[g], v_ref[g]
        buf = c_ref[...]
        pos = lax.broadcasted_iota(jnp.int32, (8, win), 1)
        q2 = jnp.where(head_of_row, jnp.broadcast_to(q, (8, GROUP_W)), 0.0)
        scale = HEAD_DIM ** -0.5
        s = jnp.where(pos % dil == 0, _dot(q2, buf[0:GROUP_W, :]) * scale, NEG_BIG)
        s_self = jnp.sum(q2 * k_new, axis=-1, keepdims=True) * scale
        m = jnp.maximum(jnp.max(s, axis=-1, keepdims=True), s_self)
        p, p_self = jnp.exp(s - m), jnp.exp(s_self - m)
        l = jnp.sum(p, axis=-1, keepdims=True) + p_self
        o = (_dot_nt(p, buf[GROUP_W:2 * GROUP_W, :]) + p_self * v_new) / l
        lse = m + jnp.log(l)
        outs.append(jnp.where(head0, o[0:1], o[1:2]))
        lses.append(jnp.where(head0, jnp.broadcast_to(lse[0:1], (1, GROUP_W)),
                              jnp.broadcast_to(lse[1:2], (1, GROUP_W))))
    m = jnp.maximum(jnp.maximum(lses[0], lses[1]), lses[2])
    e = [jnp.exp(x - m) for x in lses]
    ya_ref[...] = (e[0] * outs[0] + e[1] * outs[1] + e[2] * outs[2]) / (e[0] + e[1] + e[2])

    zrow = zrow_ref[...]
    q_col, log_f, k_col = (_as_column(zrow[:, j * C_WIDTH:(j + 1) * C_WIDTH])[:, 0:64] for j in range(3))
    v_row = zrow[:, 3 * C_WIDTH:4 * C_WIDTH]
    gate_row = zrow[:, 4 * C_WIDTH:5 * C_WIDTH]
    v_sel = jnp.concatenate([jnp.broadcast_to(v_row[:, h * 64:(h + 1) * 64], (64, 64)) for h in range(C_HEADS)],
                            axis=0)
    s_new = jnp.exp(log_f) * s_ref[...] + k_col * v_sel
    s_out_ref[...] = s_new
    qs = q_col * s_new
    o4 = jnp.concatenate([jnp.sum(qs[h * 64:(h + 1) * 64, :], axis=0, keepdims=True) for h in range(C_HEADS)],
                         axis=0)
    gate4 = jnp.concatenate([gate_row[:, h * 64:(h + 1) * 64] for h in range(C_HEADS)], axis=0)
    o4 = o4 * lax.rsqrt(jnp.mean(o4 * o4, axis=-1, keepdims=True) + EPS) * gate4
    yc_ref[...] = o4


def _sample_mix(q, k, v, caches, zc, state, layer):
    n = q.shape[1]
    per = max(p for p in (4, 2, 1) if n % p == 0)
    row3 = lambda w: pl.BlockSpec((per, 1, w), lambda b: (b, 0, 0))
    grp = pl.BlockSpec((3, per, 1, GROUP_W), lambda b: (0, b, 0, 0))
    cache_specs = [pl.BlockSpec((None, per, 2 * GROUP_W, win), lambda b: (layer, b, 0, 0))
                   for win, _ in DSWA_GROUPS]
    return pl.pallas_call(
        _sample_mix_body,
        grid=(n // per,),
        in_specs=([grp] * 3 + cache_specs
                  + [row3(5 * C_WIDTH), pl.BlockSpec((None, per, C_WIDTH, 64), lambda b: (layer, b, 0, 0))]),
        out_specs=[pl.BlockSpec((per, 1, GROUP_W), lambda b: (b, 0, 0)),
                   pl.BlockSpec((per, C_HEADS, 64), lambda b: (b, 0, 0)),
                   pl.BlockSpec((per, C_WIDTH, 64), lambda b: (b, 0, 0))],
        out_shape=[jax.ShapeDtypeStruct((n, 1, GROUP_W), F32), jax.ShapeDtypeStruct((n, C_HEADS, 64), F32),
                   jax.ShapeDtypeStruct((n, C_WIDTH, 64), F32)],
        compiler_params=_params(1),
        name="sample_mix",
    )(q.reshape(3, n, 1, GROUP_W), k.reshape(3, n, 1, GROUP_W), v.reshape(3, n, 1, GROUP_W),
      *caches, zc.reshape(n, 1, 5 * C_WIDTH), state)


def kernel(x_prompt, x_sample, cache_kv_w128, cache_kv_w512, cache_kv_w2048, state_conv, state_hgrn, w_in, b_gate,
           norm1, conv_w, hgrn_lb, hgrn_norm, w_pa, w_pb, w_pc, w_o, norm2, w_up, w_down, norm_f):
    bp, tp, d = x_prompt.shape
    bs, ts, _ = x_sample.shape
    depth = w_in.shape[0]
    assert d == D_MODEL and ts == 1 and depth == 2
    assert PAST_LEN >= max(win for win, _ in DSWA_GROUPS)

    sm = jax.nn.softmax(hgrn_lb.astype(F32), axis=0)
    lower = jnp.cumsum(sm, axis=0) - sm[0:1]

    cos_p, sin_p = _rope_tables(jnp.arange(tp, dtype=jnp.int32))
    cos_s, sin_s = _rope_tables(jnp.full((bs,), PAST_LEN, dtype=jnp.int32))

    caches = [jnp.transpose(c, (0, 1, 3, 4, 5, 2)).reshape(depth, bs, 2 * GROUP_W, win)
              for c, (win, _) in zip((cache_kv_w128, cache_kv_w512, cache_kv_w2048), DSWA_GROUPS)]
    state = state_hgrn.reshape(depth, bs, C_WIDTH, 64)

    xp = x_prompt.reshape(bp * tp, d)
    xs = x_sample.reshape(bs, d)
    kv_p, kv_s = None, None
    conv_p, hgrn_p, conv_s, hgrn_s = [], [], [], []
    for l in range(depth):
        final = l == depth - 1

        q, k_s, v_s, zc, zb, zg, w_in_bf = _sample_inproj(xs, norm1[l], w_in, l, cos_s, sin_s, lower[l],
                                                          hgrn_norm[l])
        ya, yc, s_new = _sample_mix(q, k_s, v_s, caches, zc, state, l)
        x1, u, wpa, wpb, wpc, wo = _merge_sample(xs, ya.reshape(bs, GROUP_W), yc.reshape(bs, C_WIDTH), zb, zg,
                                                 state_conv[l, :, 0], state_conv[l, :, 1], b_gate[l], conv_w[l],
                                                 l, w_pa, w_pb, w_pc, w_o)
        xs, wup, wdn = _mlp_sample(x1, norm2[l], l, w_up, w_down, norm_f, final)
        conv_s.append(jnp.stack([state_conv[l, :, 1], u], axis=1))
        hgrn_s.append(s_new.reshape(bs, C_HEADS, 64, 64))

        q, k, v, zc, zl = _inproj(xp, norm1[l], w_in_bf, cos_p, sin_p, lower[l], hgrn_norm[l], tm=2 * ROW_TILE)
        ya, *kv_p = _attn_prompt(q.reshape(3, bp, tp, -1), k.reshape(3, bp, tp, -1), v.reshape(3, bp, tp, -1),
                                 l, kv_p)
        yc, st = _hgrn_prompt(zc.reshape(bp, tp, -1), zl.reshape(bp, tp, -1))
        x1, nc = _merge_prompt(xp.reshape(bp, tp, d), ya, yc, norm1[l], w_in_bf,
                               b_gate[l], conv_w[l], wpa, wpb, wpc, wo)
        xp, *kv_s = _mlp(x1.reshape(bp * tp, d), norm2[l], wup, wdn, norm_f, final, ROW_TILE,
                         k_s, v_s, caches, l, kv_s)
        conv_p.append(nc)
        hgrn_p.append(st)

    def kv6(c, n):
        return jnp.transpose(c.reshape(depth, n, 2, 2, HEAD_DIM, c.shape[3]), (0, 1, 5, 2, 3, 4))

    return (xp.reshape(bp, tp, d), xs.reshape(bs, ts, d),
            kv6(kv_p[0], bp), kv6(kv_p[1], bp), kv6(kv_p[2], bp), jnp.stack(conv_p), jnp.stack(hgrn_p),
            kv6(kv_s[0], bs), kv6(kv_s[1], bs), kv6(kv_s[2], bs), jnp.stack(conv_s), jnp.stack(hgrn_s))
```

```python
import functools

import jax
import jax.numpy as jnp
from jax import lax
from jax.experimental import pallas as pl
from jax.experimental.pallas import tpu as pltpu

F32 = jnp.float32
BF16 = jnp.bfloat16

D_MODEL = 1024
HEAD_DIM = 64
DSWA_GROUPS = ((128, 1), (512, 4), (2048, 16))
BAND = 128
GROUP_W = 2 * HEAD_DIM
A_WIDTH = 3 * GROUP_W
CONV_WIDTH = 384
CONV_K = 3
C_HEADS = 4
C_WIDTH = C_HEADS * 64
N_BRANCH = 3
D_FF = 4 * D_MODEL
ROPE_THETA = 10000.0
EPS = 1e-6
NEG_BIG = -1e30
LB_FLOOR = 1e-30
PAST_LEN = 8192
IN_COLS = 3 * A_WIDTH + 3 * CONV_WIDTH + 4 * C_WIDTH + N_BRANCH * D_MODEL
COL_B = 3 * A_WIDTH
COL_C = COL_B + 3 * CONV_WIDTH
COL_G = COL_C + 4 * C_WIDTH
CHUNK = 64
SUB = 8
LEVELS = (32, 16, 8)
VMEM_LIMIT = 56 * 1024 * 1024
ROW_TILE = 512
SUB_ROWS = 256


def _params(n_grid):
    return pltpu.CompilerParams(dimension_semantics=("arbitrary",) * n_grid,
                                vmem_limit_bytes=VMEM_LIMIT)


def _const_spec(shape):
    nd = len(shape)
    return pl.BlockSpec(shape, lambda *_: (0,) * nd, pipeline_mode=pl.Buffered(1))


def _layer_spec(stacked, layer):
    zeros = (0,) * (stacked.ndim - 1)
    return pl.BlockSpec((None,) + stacked.shape[1:], lambda *_: (layer,) + zeros, pipeline_mode=pl.Buffered(1))


def _dot(a, b):
    return jnp.dot(a.astype(BF16), b.astype(BF16), preferred_element_type=F32)


def _dot_nt(a, b):
    return lax.dot_general(a.astype(BF16), b.astype(BF16), (((1,), (1,)), ((), ())),
                           preferred_element_type=F32)


def _dot_tn(a, b):
    return lax.dot_general(a.astype(BF16), b.astype(BF16), (((0,), (0,)), ((), ())),
                           preferred_element_type=F32)


def _rmsnorm(x, g):
    return x * lax.rsqrt(jnp.mean(x * x, axis=-1, keepdims=True) + EPS) * g


def _store_qkv_and_hgrn(za, zc, cos, sin, lb, gnorm, q_ref, k_ref, v_ref, zc_ref, rows, zl_ref=None):
    chunked = zl_ref is not None
    lane = lax.broadcasted_iota(jnp.int32, cos.shape, 1)
    first_half = (lane % HEAD_DIM) < HEAD_DIM // 2
    for out_ref, c0 in ((q_ref, 0), (k_ref, A_WIDTH)):
        for g in range(3):
            z = za[:, c0 + g * GROUP_W:c0 + (g + 1) * GROUP_W]
            partner = jnp.where(first_half, pltpu.roll(z, GROUP_W - HEAD_DIM // 2, 1),
                                pltpu.roll(z, HEAD_DIM // 2, 1))
            out_ref[g, rows, :] = z * cos + partner * sin
    for g in range(3):
        v_ref[g, rows, :] = za[:, 2 * A_WIDTH + g * GROUP_W:2 * A_WIDTH + (g + 1) * GROUP_W]
    parts = _hgrn_inputs(zc, lb, gnorm, chunked)
    for i, part in enumerate(parts):
        zc_ref[rows, i * C_WIDTH:(i + 1) * C_WIDTH] = part
    if chunked:
        q, b, k = parts[0:3]
        for i, part in enumerate(_hgrn_level_operands(q, b, k)):
            zl_ref[rows, i * C_WIDTH:(i + 1) * C_WIDTH] = part


def _inproj_body(x_ref, g_ref, w_ref, cos_ref, sin_ref, lb_ref, gn_ref, q_ref, k_ref, v_ref, zc_ref, zl_ref):
    tm = x_ref.shape[0]
    sub = min(tm, SUB_ROWS)
    for r0 in range(0, tm, sub):
        rows = slice(r0, r0 + sub)
        h = _rmsnorm(x_ref[rows, :], g_ref[...]).astype(BF16)
        za = jnp.dot(h, w_ref[:, 0:COL_B], preferred_element_type=F32)
        zc = jnp.dot(h, w_ref[:, COL_C:COL_G], preferred_element_type=F32)
        _store_qkv_and_hgrn(za, zc, cos_ref[rows, :], sin_ref[rows, :], lb_ref[...], gn_ref[...],
                            q_ref, k_ref, v_ref, zc_ref, rows, zl_ref)


def _inproj(x2d, norm, w_in, cos, sin, lb, gnorm, tm):
    m = x2d.shape[0]
    assert tm % CHUNK == 0
    n_pos = cos.shape[0] // tm
    row = lambda w: pl.BlockSpec((tm, w), lambda i: (i, 0))
    tab = pl.BlockSpec((tm, GROUP_W), lambda i: (i % n_pos, 0))
    grp = pl.BlockSpec((3, tm, GROUP_W), lambda i: (0, i, 0))
    return pl.pallas_call(
        _inproj_body,
        grid=(m // tm,),
        in_specs=[row(D_MODEL), _const_spec((1, D_MODEL)), _const_spec(w_in.shape),
                  tab, tab, _const_spec((1, C_WIDTH)), _const_spec((1, C_WIDTH))],
        out_specs=[grp] * 3 + [row(5 * C_WIDTH), row(2 * len(LEVELS) * C_WIDTH)],
        out_shape=[jax.ShapeDtypeStruct((3, m, GROUP_W), F32)] * 3
        + [jax.ShapeDtypeStruct((m, 5 * C_WIDTH), F32), jax.ShapeDtypeStruct((m, 2 * len(LEVELS) * C_WIDTH), BF16)],
        compiler_params=_params(1),
        name="inproj",
    )(x2d, norm.reshape(1, D_MODEL), w_in, cos, sin, lb.reshape(1, C_WIDTH), gnorm.reshape(1, C_WIDTH))


def _sample_inproj_body(x_ref, g_ref, w_ref, cos_ref, sin_ref, lb_ref, gn_ref,
                        q_ref, k_ref, v_ref, zc_ref, zb_ref, zg_ref, wbf_ref, z_scr, *, n_blocks):
    j = pl.program_id(0)
    w = w_ref[...].astype(BF16)
    wbf_ref[...] = w
    h = _rmsnorm(x_ref[...], g_ref[...]).astype(BF16)
    z_scr[j] = jnp.dot(h, w, preferred_element_type=F32)

    @pl.when(j == n_blocks - 1)
    def _():
        z = jnp.concatenate([z_scr[i] for i in range(n_blocks)], axis=1)
        _store_qkv_and_hgrn(z[:, 0:COL_B], z[:, COL_C:COL_G], cos_ref[...], sin_ref[...], lb_ref[...],
                            gn_ref[...], q_ref, k_ref, v_ref, zc_ref, slice(None))
        zb_ref[...] = z[:, COL_B:COL_C]
        zg_ref[...] = z[:, COL_G:IN_COLS]


def _sample_inproj(x2d, norm, w_in_f32, layer, cos, sin, lb, gnorm, n_blocks=5):
    m = x2d.shape[0]
    bw = IN_COLS // n_blocks
    assert bw * n_blocks == IN_COLS and bw % 128 == 0
    widths = (5 * C_WIDTH, 3 * CONV_WIDTH, N_BRANCH * D_MODEL)
    return pl.pallas_call(
        functools.partial(_sample_inproj_body, n_blocks=n_blocks),
        grid=(n_blocks,),
        in_specs=[_const_spec((m, D_MODEL)), _const_spec((1, D_MODEL)),
                  pl.BlockSpec((None, D_MODEL, bw), lambda j: (layer, 0, j)),
                  _const_spec((m, GROUP_W)), _const_spec((m, GROUP_W)),
                  _const_spec((1, C_WIDTH)), _const_spec((1, C_WIDTH))],
        out_specs=[_const_spec((3, m, GROUP_W))] * 3 + [_const_spec((m, w)) for w in widths]
        + [pl.BlockSpec((D_MODEL, bw), lambda j: (0, j))],
        out_shape=[jax.ShapeDtypeStruct((3, m, GROUP_W), F32)] * 3
        + [jax.ShapeDtypeStruct((m, w), F32) for w in widths]
        + [jax.ShapeDtypeStruct((D_MODEL, IN_COLS), BF16)],
        scratch_shapes=[pltpu.VMEM((n_blocks, m, bw), F32)],
        compiler_params=_params(1),
        name="sample_inproj",
    )(x2d, norm.reshape(1, D_MODEL), w_in_f32, cos, sin, lb.reshape(1, C_WIDTH), gnorm.reshape(1, C_WIDTH))


def _rope_tables(pos):
    half = HEAD_DIM // 2
    inv = ROPE_THETA ** (-jnp.arange(half, dtype=F32) / half)
    ang = pos.astype(F32)[:, None] * inv[None, :]
    cos, sin = jnp.cos(ang), jnp.sin(ang)
    return jnp.tile(cos, (1, 4)), jnp.tile(jnp.concatenate([-sin, sin], axis=1), (1, 2))


def _attend(tiles):
    head0 = lax.broadcasted_iota(jnp.int32, (BAND, GROUP_W), 1) < HEAD_DIM
    scale = HEAD_DIM ** -0.5
    q2 = [jnp.concatenate([jnp.where(head0, q, 0.0), jnp.where(head0, 0.0, q)], axis=0).astype(BF16)
          for q, _ in tiles]
    keys = [jnp.concatenate([k for k, _, _ in blocks], axis=0) for _, blocks in tiles]
    vals = [jnp.concatenate([v1 for _, v1, _ in blocks], axis=0) for _, blocks in tiles]
    valid = [jnp.concatenate([ok for _, _, ok in blocks], axis=1) for _, blocks in tiles]
    s = [jnp.where(ok, _dot_nt(qq, kk) * scale, NEG_BIG) for qq, kk, ok in zip(q2, keys, valid)]
    m = [jnp.max(x, axis=-1, keepdims=True) for x in s]
    p = [jnp.exp(x - mm).astype(BF16) for x, mm in zip(s, m)]
    ext = [jnp.dot(pp, vv, preferred_element_type=F32) for pp, vv in zip(p, vals)]
    results = []
    for e, mm in zip(ext, m):
        den = e[:, GROUP_W:]
        o2 = e[:, 0:GROUP_W] / den
        lse2 = mm + jnp.log(den)
        results.append((jnp.where(head0, o2[0:BAND], o2[BAND:]), jnp.where(head0, lse2[0:BAND], lse2[BAND:])))
    return results


def _attn_prompt_body(q_ref, k_ref, v_ref, ya_ref, c0_ref, c1_ref, c2_ref, o_scr, lse_scr, *, seq):
    qi = lax.broadcasted_iota(jnp.int32, (2 * BAND, BAND), 0) % BAND
    kj = lax.broadcasted_iota(jnp.int32, (2 * BAND, BAND), 1)
    cur_valid = kj <= qi
    prev_valid = kj >= qi
    ones = jnp.ones((BAND, GROUP_W), BF16)

    def rows(start, dil):
        return pl.ds(start, BAND, stride=dil) if dil > 1 else pl.ds(start, BAND)

    def key_block(g, r):
        return k_ref[g, r, :].astype(BF16), jnp.concatenate([v_ref[g, r, :].astype(BF16), ones], axis=1)

    def run(g, dil, chains):
        tiles, slices = [], []
        for starts, first_prev in chains:
            chain = [rows(s, dil) for s in starts]
            keys = [key_block(g, r) for r in chain]
            for j, r in enumerate(chain):
                blocks = [(*keys[j], cur_valid)]
                if j > 0:
                    blocks.append((*keys[j - 1], prev_valid))
                elif first_prev is not None:
                    blocks.append((*key_block(g, rows(first_prev[0], dil)), prev_valid & first_prev[1]))
                tiles.append((q_ref[g, r, :], blocks))
            slices += chain
        for r, (o, lse) in zip(slices, _attend(tiles)):
            o_scr[g, r, :] = o
            lse_scr[g, r, :] = lse

    for g, (_, dil) in enumerate(DSWA_GROUPS):
        n_blocks = seq // dil // BAND
        if dil == 1:
            def dense(i, carry):
                s0 = pl.multiple_of(8 * i * BAND, BAND)
                prev = pl.multiple_of(jnp.maximum(8 * i - 1, 0) * BAND, BAND)
                run(g, dil, [([s0 + u * BAND for u in range(8)], (prev, i > 0))])
                return carry

            lax.fori_loop(0, n_blocks // 8, dense, 0)
        elif n_blocks > 1:
            def residue(r, carry, g=g, dil=dil, n_blocks=n_blocks):
                run(g, dil, [([2 * r + u + i * BAND * dil for i in range(n_blocks)], None) for u in range(2)])
                return carry

            lax.fori_loop(0, dil // 2, residue, 0)
        else:
            def residues(j, carry, g=g, dil=dil):
                run(g, dil, [([8 * j + u], None) for u in range(8)])
                return carry

            lax.fori_loop(0, dil // 8, residues, 0)

    step = 256

    def merge(i, carry):
        sl = pl.ds(pl.multiple_of(i * step, step), step)
        lse = [lse_scr[g, sl, :] for g in range(3)]
        m = jnp.maximum(jnp.maximum(lse[0], lse[1]), lse[2])
        e = [jnp.exp(x - m) for x in lse]
        acc = e[0] * o_scr[0, sl, :] + e[1] * o_scr[1, sl, :] + e[2] * o_scr[2, sl, :]
        ya_ref[0, sl, :] = acc / (e[0] + e[1] + e[2])
        return carry

    lax.fori_loop(0, seq // step, merge, 0)

    for g, (c_ref, (win, _)) in enumerate(zip((c0_ref, c1_ref, c2_ref), DSWA_GROUPS)):
        for j in range(0, win, 512):
            w = min(512, win - j)
            c_ref[0:GROUP_W, j:j + w] = k_ref[g, seq - win + j:seq - win + j + w, :].T
            c_ref[GROUP_W:2 * GROUP_W, j:j + w] = v_ref[g, seq - win + j:seq - win + j + w, :].T


def _attn_prompt(q, k, v, layer, caches):
    _, bsz, seq, _ = q.shape
    depth = 2
    assert seq % (BAND * 16) == 0 and seq >= 2048
    qkv_spec = pl.BlockSpec((3, None, seq, GROUP_W), lambda b: (0, b, 0, 0))
    cache_specs = [pl.BlockSpec((None, None, 2 * GROUP_W, win), lambda b: (layer, b, 0, 0))
                   for win, _ in DSWA_GROUPS]
    cache_shapes = [jax.ShapeDtypeStruct((depth, bsz, 2 * GROUP_W, win), F32) for win, _ in DSWA_GROUPS]
    in_specs = [qkv_spec] * 3
    args = [q, k, v]
    aliases = {}
    if caches is not None:
        in_specs = in_specs + [pl.BlockSpec(memory_space=pl.ANY)] * 3
        args = args + list(caches)
        aliases = {3: 1, 4: 2, 5: 3}

    def body(*refs):
        ins, rest = refs[:3], refs[3 + (3 if caches is not None else 0):]
        _attn_prompt_body(*ins, *rest, seq=seq)

    return pl.pallas_call(
        body,
        grid=(bsz,),
        in_specs=in_specs,
        out_specs=[pl.BlockSpec((1, seq, GROUP_W), lambda b: (b, 0, 0))] + cache_specs,
        out_shape=[jax.ShapeDtypeStruct((bsz, seq, GROUP_W), F32)] + cache_shapes,
        scratch_shapes=[pltpu.VMEM((3, seq, GROUP_W), F32), pltpu.VMEM((3, seq, GROUP_W), F32)],
        input_output_aliases=aliases,
        compiler_params=_params(1),
        name="attn_prompt",
    )(*args)


def _log_forget_and_kgate(zf, lb):
    log_sig = -(jnp.maximum(-zf, 0.0) + jnp.log(1.0 + jnp.exp(-jnp.abs(zf))))
    a = jnp.log(jnp.maximum(lb, LB_FLOOR))
    c = jnp.log1p(-lb) + log_sig
    log_f = jnp.maximum(a, c) + jnp.log(1.0 + jnp.exp(-jnp.abs(a - c)))
    kgate = (1.0 - lb) * jax.nn.sigmoid(-zf)
    return log_f, kgate


def _silu(x):
    return x * jax.nn.sigmoid(x)


def _split3(x):
    hi = x.astype(BF16)
    r1 = x - hi.astype(F32)
    mid = r1.astype(BF16)
    lo = (r1 - mid.astype(F32)).astype(BF16)
    return hi, mid, lo


def _hgrn_inputs(zc, lb, gnorm, chunked):
    q, zf, v, og = (zc[:, i * C_WIDTH:(i + 1) * C_WIDTH] for i in range(4))
    log_f, k = _log_forget_and_kgate(zf, lb)
    if chunked:
        n = zc.shape[0]
        r = lax.broadcasted_iota(jnp.int32, (n, n), 0)
        c = lax.broadcasted_iota(jnp.int32, (n, n), 1)
        tri = jnp.where((c <= r) & ((r // CHUNK) == (c // CHUNK)), 1.0, 0.0).astype(BF16)
        b = functools.reduce(jnp.add, [jnp.dot(tri, p, preferred_element_type=F32) for p in _split3(log_f)])
    else:
        b = log_f
    return q, b, k, v, gnorm * _silu(og)


def _hgrn_level_operands(q, b, k):
    n = q.shape[0]
    row = lax.broadcasted_iota(jnp.int32, (n, C_WIDTH), 0)
    out = []
    for m in LEVELS:
        ref_q, ref_k = [], []
        for j in range(n // m):
            own = b[j * m:(j + 1) * m, :]
            if j % 2 == 1:
                ref_q.append(jnp.broadcast_to(b[j * m - 1:j * m, :], (m, C_WIDTH)))
                ref_k.append(own)
            else:
                ref_q.append(own)
                ref_k.append(jnp.broadcast_to(b[(j + 1) * m - 1:(j + 1) * m, :], (m, C_WIDTH)))
        upper = (row // m) % 2 == 1
        qt = jnp.where(upper, q * jnp.exp(jnp.minimum(b - jnp.concatenate(ref_q, axis=0), 0.0)), 0.0)
        kt = jnp.where(upper, 0.0, k * jnp.exp(jnp.minimum(jnp.concatenate(ref_k, axis=0) - b, 0.0)))
        out += [qt.astype(BF16), kt.astype(BF16)]
    return out


def _hgrn_chunks(seqs, cst):
    n = CHUNK
    ns = len(seqs)
    ones_bd = cst["ones_bd"]

    def head_sums(x):
        return jnp.dot(x.astype(BF16), ones_bd, preferred_element_type=F32)

    tiles = (n // SUB, SUB, C_WIDTH)
    ys = []
    for q, b, k, *_ in seqs:
        q3, k3, b3 = (x.reshape(tiles) for x in (q, k, b))
        f3 = jnp.exp(jnp.minimum(b3 - pltpu.roll(b3, 1, 1), 0.0))
        ys.append((q * k).astype(BF16))
        decay = None
        for d in range(1, SUB):
            decay = f3 if d == 1 else decay * pltpu.roll(f3, d - 1, 1)
            ys.append((q3 * pltpu.roll(k3, d, 1) * decay).reshape(n, C_WIDTH).astype(BF16))
    att_d = head_sums(jnp.concatenate(ys, axis=0))
    att = [functools.reduce(jnp.add, [att_d[(i * SUB + d) * n:(i * SUB + d + 1) * n, :] * cst["place"][d]
                                      for d in range(SUB)]) for i in range(ns)]

    for lvl in range(len(LEVELS)):
        k_stack = [jnp.concatenate([s[5][2 * lvl + 1]] * C_HEADS, axis=0) * cst["stack"] for s in seqs]
        a = [_dot_nt(s[5][2 * lvl], ks) for s, ks in zip(seqs, k_stack)]
        att = [x + y * cst["same_block"][lvl] for x, y in zip(att, a)]
    v_stack = [jnp.concatenate([s[3].astype(BF16)] * C_HEADS, axis=0) * cst["stack"] for s in seqs]
    q_hat = [s[0] * jnp.exp(s[1]) for s in seqs]
    b_last = [s[1][n - 1:n, :] for s in seqs]
    k_hat = [s[2] * jnp.exp(bl - s[1]) for s, bl in zip(seqs, b_last)]

    o = [jnp.dot(x.astype(BF16), vs, preferred_element_type=F32) for x, vs in zip(att, v_stack)]
    o_st = [_dot_nt(qh, s[6]) for qh, s in zip(q_hat, seqs)]
    upd = [_dot_tn(s[3], kh) for s, kh in zip(seqs, k_hat)]
    o = [x + y for x, y in zip(o, o_st)]
    st = [s[6] * jnp.exp(bl) + u * cst["same_head"] for s, bl, u in zip(seqs, b_last, upd)]

    sq = [x * x for x in o]
    sq_hi = [x.astype(BF16) for x in sq]
    sq_lo = [(x - h.astype(F32)).astype(BF16) for x, h in zip(sq, sq_hi)]
    ms = head_sums(jnp.concatenate(sq_hi + sq_lo, axis=0))
    out = []
    for i, (x, s) in enumerate(zip(o, seqs)):
        m2 = ms[i * n:(i + 1) * n, :] + ms[(ns + i) * n:(ns + i + 1) * n, :]
        out.append((x * lax.rsqrt(m2 * (1.0 / 64) + EPS) * s[4], st[i]))
    return out


def _hgrn_constants():
    n = CHUNK
    hr = lax.broadcasted_iota(jnp.int32, (C_WIDTH, C_WIDTH), 0) // 64
    hc = lax.broadcasted_iota(jnp.int32, (C_WIDTH, C_WIDTH), 1) // 64
    same_head = jnp.where(hr == hc, 1.0, 0.0)
    srow = lax.broadcasted_iota(jnp.int32, (n, C_HEADS * n), 0)
    scol = lax.broadcasted_iota(jnp.int32, (n, C_HEADS * n), 1) % n
    return {
        "ones_bd": same_head.astype(BF16),
        "same_head": same_head,
        "stack": same_head.astype(BF16),
        "place": [jnp.where((scol == srow - d) & (srow % SUB >= d), 1.0, 0.0) for d in range(SUB)],
        "same_block": [jnp.where(srow // (2 * m) == scol // (2 * m), 1.0, 0.0) for m in LEVELS],
    }


def _hgrn_prompt_body(zc_ref, zl_ref, yc_ref, st_ref, st_scr, *, rows, n_seq):
    t = pl.program_id(1)

    @pl.when(t == 0)
    def _():
        st_scr[...] = jnp.zeros_like(st_scr)

    cst = _hgrn_constants()

    def chunk(c, carry):
        sl = pl.ds(pl.multiple_of(c * CHUNK, CHUNK), CHUNK)
        seqs = [(*(zc_ref[i, sl, j * C_WIDTH:(j + 1) * C_WIDTH] for j in range(5)),
                 [zl_ref[i, sl, j * C_WIDTH:(j + 1) * C_WIDTH] for j in range(2 * len(LEVELS))], st_scr[i])
                for i in range(n_seq)]
        for i, (o, st) in enumerate(_hgrn_chunks(seqs, cst)):
            st_scr[i] = st
            yc_ref[i, sl, :] = o
        return carry

    lax.fori_loop(0, rows // CHUNK, chunk, 0)

    @pl.when(t == pl.num_programs(1) - 1)
    def _():
        for i in range(n_seq):
            for h in range(C_HEADS):
                st_ref[i, h] = st_scr[i, h * 64:(h + 1) * 64, h * 64:(h + 1) * 64]


def _hgrn_prompt(zc, zl, rows=ROW_TILE // 2):
    bsz, seq, _ = zc.shape
    n_seq = max(n for n in (8, 4, 2, 1) if bsz % n == 0)
    assert CHUNK == 64
    yc, st = pl.pallas_call(
        functools.partial(_hgrn_prompt_body, rows=rows, n_seq=n_seq),
        grid=(bsz // n_seq, seq // rows),
        in_specs=[pl.BlockSpec((n_seq, rows, 5 * C_WIDTH), lambda b, t: (b, t, 0)),
                  pl.BlockSpec((n_seq, rows, 2 * len(LEVELS) * C_WIDTH), lambda b, t: (b, t, 0))],
        out_specs=[pl.BlockSpec((n_seq, rows, C_WIDTH), lambda b, t: (b, t, 0)),
                   pl.BlockSpec((n_seq, C_HEADS, 64, 64), lambda b, t: (b, 0, 0, 0))],
        out_shape=[jax.ShapeDtypeStruct((bsz, seq, C_WIDTH), F32),
                   jax.ShapeDtypeStruct((bsz, C_HEADS, 64, 64), F32)],
        scratch_shapes=[pltpu.VMEM((n_seq, C_WIDTH, C_WIDTH), F32)],
        compiler_params=_params(2),
        name="hgrn_prompt",
    )(zc, zl)
    return yc, jnp.swapaxes(st, -1, -2)


def _gated_merge(x, ya, yb, yc, zg, bg, wpa, wpb, wpc, wo):
    g = jax.nn.sigmoid(zg + bg)
    mix = (g[:, 0:D_MODEL] * _dot(ya, wpa) + g[:, D_MODEL:2 * D_MODEL] * _dot(yb, wpb)
           + g[:, 2 * D_MODEL:3 * D_MODEL] * _dot(yc, wpc))
    return x + _dot(mix, wo)


def _project_conv_and_gates(x, g1, w_ref):
    h = _rmsnorm(x, g1).astype(BF16)
    return (jnp.dot(h, w_ref[:, COL_B:COL_C], preferred_element_type=F32),
            jnp.dot(h, w_ref[:, COL_G:IN_COLS], preferred_element_type=F32))


def _merge_prompt_body(x_ref, ya_ref, yc_ref, g1_ref, w_ref, bg_ref, cw_ref, wpa_ref, wpb_ref, wpc_ref,
                       wo_ref, x1_ref, nc_ref, u_scr, *, rows):
    @pl.when(pl.program_id(1) == 0)
    def _():
        u_scr[0:8, :] = jnp.zeros((8, CONV_WIDTH), F32)

    sub = min(rows, ROW_TILE)
    for r0 in range(0, rows, sub):
        sl = slice(r0, r0 + sub)
        x = x_ref[0, sl, :]
        zb, zg = _project_conv_and_gates(x, g1_ref[...], w_ref)
        u = zb[:, 2 * CONV_WIDTH:3 * CONV_WIDTH] * zb[:, 0:CONV_WIDTH]
        u_scr[8 + r0:8 + r0 + sub, :] = u
        conv = (cw_ref[0:1, :] * u_scr[pl.ds(6 + r0, sub), :] + cw_ref[1:2, :] * u_scr[pl.ds(7 + r0, sub), :]
                + cw_ref[2:3, :] * u)
        yb = zb[:, CONV_WIDTH:2 * CONV_WIDTH] * conv
        x1_ref[0, sl, :] = _gated_merge(x, ya_ref[0, sl, :], yb, yc_ref[0, sl, :], zg, bg_ref[...],
                                        wpa_ref[...], wpb_ref[...], wpc_ref[...], wo_ref[...])
    nc_ref[0] = u_scr[8 + rows - (CONV_K - 1):8 + rows, :]
    u_scr[0:8, :] = u_scr[rows:rows + 8, :]


def _merge_prompt(x, ya, yc, norm1, w_in, b_gate, conv_w, wpa, wpb, wpc, wo, rows=2 * ROW_TILE):
    bsz, seq, _ = x.shape
    blk = lambda w: pl.BlockSpec((1, rows, w), lambda b, t: (b, t, 0))
    return pl.pallas_call(
        functools.partial(_merge_prompt_body, rows=rows),
        grid=(bsz, seq // rows),
        in_specs=[blk(D_MODEL), blk(GROUP_W), blk(C_WIDTH),
                  _const_spec((1, D_MODEL)), _const_spec(w_in.shape),
                  _const_spec((1, N_BRANCH * D_MODEL)), _const_spec((CONV_K, CONV_WIDTH)),
                  _const_spec(wpa.shape), _const_spec(wpb.shape), _const_spec(wpc.shape), _const_spec(wo.shape)],
        out_specs=[blk(D_MODEL), pl.BlockSpec((1, CONV_K - 1, CONV_WIDTH), lambda b, t: (b, 0, 0))],
        out_shape=[jax.ShapeDtypeStruct((bsz, seq, D_MODEL), F32),
                   jax.ShapeDtypeStruct((bsz, CONV_K - 1, CONV_WIDTH), F32)],
        scratch_shapes=[pltpu.VMEM((rows + 8, CONV_WIDTH), F32)],
        compiler_params=_params(2),
        name="merge_prompt",
    )(x, ya, yc, norm1.reshape(1, D_MODEL), w_in, b_gate.reshape(1, -1), conv_w, wpa, wpb, wpc, wo)


def _merge_sample_body(x_ref, ya_ref, yc_ref, zb_ref, zg_ref, p0_ref, p1_ref, bg_ref, cw_ref, wpa_ref,
                       wpb_ref, wpc_ref, wo_ref, x1_ref, u_ref, wpa_bf_ref, wpb_bf_ref, wpc_bf_ref, wo_bf_ref):
    bf = []
    for w_ref, bf_ref in ((wpa_ref, wpa_bf_ref), (wpb_ref, wpb_bf_ref), (wpc_ref, wpc_bf_ref),
                          (wo_ref, wo_bf_ref)):
        bf.append(w_ref[...].astype(BF16))
        bf_ref[...] = bf[-1]
    zb = zb_ref[...]
    u = zb[:, 2 * CONV_WIDTH:3 * CONV_WIDTH] * zb[:, 0:CONV_WIDTH]
    conv = cw_ref[0:1, :] * p0_ref[...] + cw_ref[1:2, :] * p1_ref[...] + cw_ref[2:3, :] * u
    yb = zb[:, CONV_WIDTH:2 * CONV_WIDTH] * conv
    u_ref[...] = u
    x1_ref[...] = _gated_merge(x_ref[...], ya_ref[...], yb, yc_ref[...], zg_ref[...], bg_ref[...], *bf)


def _merge_sample(x, ya, yc, zb, zg, prev0, prev1, b_gate, conv_w, layer, w_pa, w_pb, w_pc, w_o):
    m = x.shape[0]
    small = (x, ya, yc, zb, zg, prev0, prev1, b_gate.reshape(1, -1), conv_w)
    stacks = (w_pa, w_pb, w_pc, w_o)
    return pl.pallas_call(
        _merge_sample_body,
        grid=(1,),
        in_specs=[_const_spec(a.shape) for a in small] + [_layer_spec(w, layer) for w in stacks],
        out_specs=[_const_spec((m, D_MODEL)), _const_spec((m, CONV_WIDTH))]
        + [_const_spec(w.shape[1:]) for w in stacks],
        out_shape=[jax.ShapeDtypeStruct((m, D_MODEL), F32), jax.ShapeDtypeStruct((m, CONV_WIDTH), F32)]
        + [jax.ShapeDtypeStruct(w.shape[1:], BF16) for w in stacks],
        compiler_params=_params(1),
        name="merge_sample",
    )(*small, *stacks)


def _as_column(row):
    return jnp.concatenate([jnp.broadcast_to(row[:, c:c + 128], (128, 128)).T for c in range(0, row.shape[1], 128)],
                           axis=0)


def _shift_window_buffers(k_ref, v_ref, cache_refs, new_refs):
    for g, (c_ref, n_ref, (win, _)) in enumerate(zip(cache_refs, new_refs, DSWA_GROUPS)):
        rolled = pltpu.roll(c_ref[...], win - 1, 1)
        new_col = jnp.concatenate([_as_column(k_ref[g]), _as_column(v_ref[g])], axis=0)
        last = lax.broadcasted_iota(jnp.int32, new_col.shape, 1) == 127
        if win > 128:
            n_ref[:, 0:win - 128] = rolled[:, 0:win - 128]
        n_ref[:, win - 128:win] = jnp.where(last, new_col, rolled[:, win - 128:win])


def _mlp_body(x_ref, g2_ref, wup_ref, wdn_ref, gf_ref, k_ref, v_ref, c0_ref, c1_ref, c2_ref,
              out_ref, n0_ref, n1_ref, n2_ref, *, final):
    _shift_window_buffers(k_ref, v_ref, (c0_ref, c1_ref, c2_ref), (n0_ref, n1_ref, n2_ref))
    tm = x_ref.shape[0]
    sub = min(tm, ROW_TILE)
    for r0 in range(0, tm, sub):
        rows = slice(r0, r0 + sub)
        x = x_ref[rows, :]
        h = _rmsnorm(x, g2_ref[...]).astype(BF16)
        acc = x
        for c in range(D_FF // D_MODEL):
            cols = slice(c * D_MODEL, (c + 1) * D_MODEL)
            a = jnp.maximum(jnp.dot(h, wup_ref[:, cols], preferred_element_type=F32), 0.0)
            acc = acc + jnp.dot((a * a).astype(BF16), wdn_ref[cols, :], preferred_element_type=F32)
        out_ref[rows, :] = _rmsnorm(acc, gf_ref[...]) if final else acc


def _mlp(x2d, norm2, wup, wdn, norm_f, final, tm, k_new, v_new, caches, layer, new_caches):
    m = x2d.shape[0]
    depth, n = caches[0].shape[:2]
    steps = m // tm
    assert n <= steps
    row = pl.BlockSpec((tm, D_MODEL), lambda i: (i, 0))
    seq = lambda i: jnp.minimum(i, n - 1)
    grp = pl.BlockSpec((3, None, 1, GROUP_W), lambda i: (0, seq(i), 0, 0))
    cache_specs = [pl.BlockSpec((None, None, 2 * GROUP_W, win), lambda i: (layer, seq(i), 0, 0))
                   for win, _ in DSWA_GROUPS]
    in_specs = [row, _const_spec((1, D_MODEL)), _const_spec(wup.shape), _const_spec(wdn.shape),
                _const_spec((1, D_MODEL)), grp, grp] + cache_specs
    args = [x2d, norm2.reshape(1, D_MODEL), wup, wdn, norm_f.reshape(1, D_MODEL),
            k_new.reshape(3, n, 1, GROUP_W), v_new.reshape(3, n, 1, GROUP_W), *caches]
    aliases = {}
    if new_caches is not None:
        aliases = {len(args) + i: 1 + i for i in range(3)}
        in_specs = in_specs + [pl.BlockSpec(memory_space=pl.ANY)] * 3
        args = args + list(new_caches)
    n_in = len(cache_specs) + 7

    def body(*refs):
        _mlp_body(*refs[:n_in], *refs[len(args):], final=final)

    return pl.pallas_call(
        body,
        grid=(steps,),
        in_specs=in_specs,
        out_specs=[row] + cache_specs,
        out_shape=[jax.ShapeDtypeStruct((m, D_MODEL), F32)]
        + [jax.ShapeDtypeStruct((depth, n, 2 * GROUP_W, win), F32) for win, _ in DSWA_GROUPS],
        input_output_aliases=aliases,
        compiler_params=_params(1),
        name="mlp",
    )(*args)


def _mlp_sample_body(x_ref, g2_ref, wup_ref, wdn_ref, gf_ref, out_ref, wup_bf_ref, wdn_bf_ref, acc_scr, *, final):
    c = pl.program_id(0)
    wu = wup_ref[...].astype(BF16)
    wd = wdn_ref[...].astype(BF16)
    wup_bf_ref[...] = wu
    wdn_bf_ref[...] = wd
    x = x_ref[...]
    h = _rmsnorm(x, g2_ref[...]).astype(BF16)
    a = jnp.maximum(jnp.dot(h, wu, preferred_element_type=F32), 0.0)
    part = jnp.dot((a * a).astype(BF16), wd, preferred_element_type=F32)

    @pl.when(c == 0)
    def _():
        acc_scr[...] = x + part

    @pl.when(c > 0)
    def _():
        acc_scr[...] = acc_scr[...] + part

    @pl.when(c == pl.num_programs(0) - 1)
    def _():
        acc = acc_scr[...]
        out_ref[...] = _rmsnorm(acc, gf_ref[...]) if final else acc


def _mlp_sample(x2d, norm2, layer, w_up, w_down, norm_f, final):
    m = x2d.shape[0]
    n_chunks = D_FF // D_MODEL
    return pl.pallas_call(
        functools.partial(_mlp_sample_body, final=final),
        grid=(n_chunks,),
        in_specs=[_const_spec((m, D_MODEL)), _const_spec((1, D_MODEL)),
                  pl.BlockSpec((None, D_MODEL, D_MODEL), lambda c: (layer, 0, c)),
                  pl.BlockSpec((None, D_MODEL, D_MODEL), lambda c: (layer, c, 0)),
                  _const_spec((1, D_MODEL))],
        out_specs=[_const_spec((m, D_MODEL)), pl.BlockSpec((D_MODEL, D_MODEL), lambda c: (0, c)),
                   pl.BlockSpec((D_MODEL, D_MODEL), lambda c: (c, 0))],
        out_shape=[jax.ShapeDtypeStruct((m, D_MODEL), F32), jax.ShapeDtypeStruct((D_MODEL, D_FF), BF16),
                   jax.ShapeDtypeStruct((D_FF, D_MODEL), BF16)],
        scratch_shapes=[pltpu.VMEM((m, D_MODEL), F32)],
        compiler_params=_params(1),
        name="mlp_sample",
    )(x2d, norm2.reshape(1, D_MODEL), w_up, w_down, norm_f.reshape(1, D_MODEL))


def _sample_mix_body(q_ref, k_ref, v_ref, c0_ref, c1_ref, c2_ref, zrow_ref, s_ref, ya_ref, yc_ref, s_out_ref):
    for i in range(q_ref.shape[1]):
        _sample_mix_one(q_ref.at[:, i], k_ref.at[:, i], v_ref.at[:, i], c0_ref.at[i], c1_ref.at[i], c2_ref.at[i],
                        zrow_ref.at[i], s_ref.at[i], ya_ref.at[i], yc_ref.at[i], s_out_ref.at[i])


def _sample_mix_one(q_ref, k_ref, v_ref, c0_ref, c1_ref, c2_ref, zrow_ref, s_ref, ya_ref, yc_ref, s_out_ref):
    lane = lax.broadcasted_iota(jnp.int32, (8, GROUP_W), 1)
    rowi = lax.broadcasted_iota(jnp.int32, (8, GROUP_W), 0)
    head_of_row = (lane // HEAD_DIM) == rowi
    head0 = lax.broadcasted_iota(jnp.int32, (1, GROUP_W), 1) < HEAD_DIM

    outs, lses = [], []
    for g, (c_ref, (win, dil)) in enumerate(zip((c0_ref, c1_ref, c2_ref), DSWA_GROUPS)):
        q, k_new, v_new = q_ref[g], k_ref[g], v_ref[g]
        buf = c_ref[...]
        pos = lax.broadcasted_iota(jnp.int32, (8, win), 1)
        q2 = jnp.where(head_of_row, jnp.broadcast_to(q, (8, GROUP_W)), 0.0)
        scale = HEAD_DIM ** -0.5
        s = jnp.where(pos % dil == 0, _dot(q2, buf[0:GROUP_W, :]) * scale, NEG_BIG)
        s_self = jnp.sum(q2 * k_new, axis=-1, keepdims=True) * scale
        m = jnp.maximum(jnp.max(s, axis=-1, keepdims=True), s_self)
        p, p_self = jnp.exp(s - m), jnp.exp(s_self - m)
        l = jnp.sum(p, axis=-1, keepdims=True) + p_self
        o = (_dot_nt(p, buf[GROUP_W:2 * GROUP_W, :]) + p_self * v_new) / l
        lse = m + jnp.log(l)
        outs.append(jnp.where(head0, o[0:1], o[1:2]))
        lses.append(jnp.where(head0, jnp.broadcast_to(lse[0:1], (1, GROUP_W)),
                              jnp.broadcast_to(lse[1:2], (1, GROUP_W))))
    m = jnp.maximum(jnp.maximum(lses[0], lses[1]), lses[2])
    e = [jnp.exp(x - m) for x in lses]
    ya_ref[...] = (e[0] * outs[0] + e[1] * outs[1] + e[2] * outs[2]) / (e[0] + e[1] + e[2])

    zrow = zrow_ref[...]
    q_col, log_f, k_col = (_as_column(zrow[:, j * C_WIDTH:(j + 1) * C_WIDTH])[:, 0:64] for j in range(3))
    v_row = zrow[:, 3 * C_WIDTH:4 * C_WIDTH]
    gate_row = zrow[:, 4 * C_WIDTH:5 * C_WIDTH]
    v_sel = jnp.concatenate([jnp.broadcast_to(v_row[:, h * 64:(h + 1) * 64], (64, 64)) for h in range(C_HEADS)],
                            axis=0)
    s_new = jnp.exp(log_f) * s_ref[...] + k_col * v_sel
    s_out_ref[...] = s_new
    qs = q_col * s_new
    o4 = jnp.concatenate([jnp.sum(qs[h * 64:(h + 1) * 64, :], axis=0, keepdims=True) for h in range(C_HEADS)],
                         axis=0)
    gate4 = jnp.concatenate([gate_row[:, h * 64:(h + 1) * 64] for h in range(C_HEADS)], axis=0)
    o4 = o4 * lax.rsqrt(jnp.mean(o4 * o4, axis=-1, keepdims=True) + EPS) * gate4
    yc_ref[...] = o4


def _sample_mix(q, k, v, caches, zc, state, layer, new_state):
    n = q.shape[1]
    per = max(p for p in (4, 2, 1) if n % p == 0)
    row3 = lambda w: pl.BlockSpec((per, 1, w), lambda b: (b, 0, 0))
    grp = pl.BlockSpec((3, per, 1, GROUP_W), lambda b: (0, b, 0, 0))
    cache_specs = [pl.BlockSpec((None, per, 2 * GROUP_W, win), lambda b: (layer, b, 0, 0))
                   for win, _ in DSWA_GROUPS]
    state_spec = pl.BlockSpec((None, per, C_WIDTH, 64), lambda b: (layer, b, 0, 0))
    in_specs = [grp] * 3 + cache_specs + [row3(5 * C_WIDTH), state_spec]
    args = [q.reshape(3, n, 1, GROUP_W), k.reshape(3, n, 1, GROUP_W), v.reshape(3, n, 1, GROUP_W),
            *caches, zc.reshape(n, 1, 5 * C_WIDTH), state]
    n_in = len(args)
    aliases = {}
    if new_state is not None:
        aliases = {n_in: 2}
        in_specs = in_specs + [pl.BlockSpec(memory_space=pl.ANY)]
        args = args + [new_state]

    def body(*refs):
        _sample_mix_body(*refs[:n_in], *refs[len(args):])

    return pl.pallas_call(
        body,
        grid=(n // per,),
        in_specs=in_specs,
        out_specs=[pl.BlockSpec((per, 1, GROUP_W), lambda b: (b, 0, 0)),
                   pl.BlockSpec((per, C_HEADS, 64), lambda b: (b, 0, 0)), state_spec],
        out_shape=[jax.ShapeDtypeStruct((n, 1, GROUP_W), F32), jax.ShapeDtypeStruct((n, C_HEADS, 64), F32),
                   jax.ShapeDtypeStruct(state.shape, F32)],
        input_output_aliases=aliases,
        compiler_params=_params(1),
        name="sample_mix",
    )(*args)


def kernel(x_prompt, x_sample, cache_kv_w128, cache_kv_w512, cache_kv_w2048, state_conv, state_hgrn, w_in, b_gate,
           norm1, conv_w, hgrn_lb, hgrn_norm, w_pa, w_pb, w_pc, w_o, norm2, w_up, w_down, norm_f):
    bp, tp, d = x_prompt.shape
    bs, ts, _ = x_sample.shape
    depth = w_in.shape[0]
    assert d == D_MODEL and ts == 1 and depth == 2
    assert PAST_LEN >= max(win for win, _ in DSWA_GROUPS)

    sm = jax.nn.softmax(hgrn_lb.astype(F32), axis=0)
    lower = jnp.cumsum(sm, axis=0) - sm[0:1]

    cos_p, sin_p = _rope_tables(jnp.arange(tp, dtype=jnp.int32))
    cos_s, sin_s = _rope_tables(jnp.full((bs,), PAST_LEN, dtype=jnp.int32))

    caches = [jnp.transpose(c, (0, 1, 3, 4, 5, 2)).reshape(depth, bs, 2 * GROUP_W, win)
              for c, (win, _) in zip((cache_kv_w128, cache_kv_w512, cache_kv_w2048), DSWA_GROUPS)]
    state = state_hgrn.reshape(depth, bs, C_WIDTH, 64)

    xp = x_prompt.reshape(bp * tp, d)
    xs = x_sample.reshape(bs, d)
    kv_p, kv_s = None, None
    conv_p, hgrn_p, conv_s, hgrn_s = [], [], [], None
    for l in range(depth):
        final = l == depth - 1

        q, k_s, v_s, zc, zb, zg, w_in_bf = _sample_inproj(xs, norm1[l], w_in, l, cos_s, sin_s, lower[l],
                                                          hgrn_norm[l])
        ya, yc, hgrn_s = _sample_mix(q, k_s, v_s, caches, zc, state, l, hgrn_s)
        x1, u, wpa, wpb, wpc, wo = _merge_sample(xs, ya.reshape(bs, GROUP_W), yc.reshape(bs, C_WIDTH), zb, zg,
                                                 state_conv[l, :, 0], state_conv[l, :, 1], b_gate[l], conv_w[l],
                                                 l, w_pa, w_pb, w_pc, w_o)
        xs, wup, wdn = _mlp_sample(x1, norm2[l], l, w_up, w_down, norm_f, final)
        conv_s.append(jnp.stack([state_conv[l, :, 1], u], axis=1))

        q, k, v, zc, zl = _inproj(xp, norm1[l], w_in_bf, cos_p, sin_p, lower[l], hgrn_norm[l], tm=2 * ROW_TILE)
        ya, *kv_p = _attn_prompt(q.reshape(3, bp, tp, -1), k.reshape(3, bp, tp, -1), v.reshape(3, bp, tp, -1),
                                 l, kv_p)
        yc, st = _hgrn_prompt(zc.reshape(bp, tp, -1), zl.reshape(bp, tp, -1))
        x1, nc = _merge_prompt(xp.reshape(bp, tp, d), ya, yc, norm1[l], w_in_bf,
                               b_gate[l], conv_w[l], wpa, wpb, wpc, wo)
        xp, *kv_s = _mlp(x1.reshape(bp * tp, d), norm2[l], wup, wdn, norm_f, final, ROW_TILE,
                         k_s, v_s, caches, l, kv_s)
        conv_p.append(nc)
        hgrn_p.append(st)

    def kv6(c, n):
        return jnp.transpose(c.reshape(depth, n, 2, 2, HEAD_DIM, c.shape[3]), (0, 1, 5, 2, 3, 4))

    return (xp.reshape(bp, tp, d), xs.reshape(bs, ts, d),
            kv6(kv_p[0], bp), kv6(kv_p[1], bp), kv6(kv_p[2], bp), jnp.stack(conv_p), jnp.stack(hgrn_p),
            kv6(kv_s[0], bs), kv6(kv_s[1], bs), kv6(kv_s[2], bs), jnp.stack(conv_s),
            hgrn_s.reshape(depth, bs, C_HEADS, 64, 64))
```

```python
import functools

import jax
import jax.numpy as jnp
from jax import lax
from jax.experimental import pallas as pl
from jax.experimental.pallas import tpu as pltpu

F32 = jnp.float32
BF16 = jnp.bfloat16

D_MODEL = 1024
HEAD_DIM = 64
DSWA_GROUPS = ((128, 1), (512, 4), (2048, 16))
BAND = 128
GROUP_W = 2 * HEAD_DIM
A_WIDTH = 3 * GROUP_W
CONV_WIDTH = 384
CONV_K = 3
C_HEADS = 4
C_WIDTH = C_HEADS * 64
N_BRANCH = 3
D_FF = 4 * D_MODEL
ROPE_THETA = 10000.0
EPS = 1e-6
NEG_BIG = -1e30
LB_FLOOR = 1e-30
PAST_LEN = 8192
IN_COLS = 3 * A_WIDTH + 3 * CONV_WIDTH + 4 * C_WIDTH + N_BRANCH * D_MODEL
COL_B = 3 * A_WIDTH
COL_C = COL_B + 3 * CONV_WIDTH
COL_G = COL_C + 4 * C_WIDTH
CHUNK = 64
SUBLANES = 8
SUB = SUBLANES
LEVELS = (32, 16, 8)
VMEM_LIMIT = 56 * 1024 * 1024
ROW_TILE = 512
SUB_ROWS = 256


def _params(n_grid):
    return pltpu.CompilerParams(dimension_semantics=("arbitrary",) * n_grid,
                                vmem_limit_bytes=VMEM_LIMIT)


def _const_spec(shape):
    nd = len(shape)
    return pl.BlockSpec(shape, lambda *_: (0,) * nd, pipeline_mode=pl.Buffered(1))


def _layer_spec(stacked, layer):
    zeros = (0,) * (stacked.ndim - 1)
    return pl.BlockSpec((None,) + stacked.shape[1:], lambda *_: (layer,) + zeros, pipeline_mode=pl.Buffered(1))


def _dot(a, b):
    return jnp.dot(a.astype(BF16), b.astype(BF16), preferred_element_type=F32)


def _dot_nt(a, b):
    return lax.dot_general(a.astype(BF16), b.astype(BF16), (((1,), (1,)), ((), ())),
                           preferred_element_type=F32)


def _dot_tn(a, b):
    return lax.dot_general(a.astype(BF16), b.astype(BF16), (((0,), (0,)), ((), ())),
                           preferred_element_type=F32)


def _rmsnorm(x, g):
    return x * lax.rsqrt(jnp.mean(x * x, axis=-1, keepdims=True) + EPS) * g


def _store_qkv_and_hgrn(za, zc, cos, sin, lb, gnorm, q_ref, k_ref, v_ref, zc_ref, rows, zl_ref=None):
    chunked = zl_ref is not None
    lane = lax.broadcasted_iota(jnp.int32, cos.shape, 1)
    first_half = (lane % HEAD_DIM) < HEAD_DIM // 2
    for out_ref, c0 in ((q_ref, 0), (k_ref, A_WIDTH)):
        for g in range(3):
            z = za[:, c0 + g * GROUP_W:c0 + (g + 1) * GROUP_W]
            partner = jnp.where(first_half, pltpu.roll(z, GROUP_W - HEAD_DIM // 2, 1),
                                pltpu.roll(z, HEAD_DIM // 2, 1))
            out_ref[g, rows, :] = z * cos + partner * sin
    for g in range(3):
        v_ref[g, rows, :] = za[:, 2 * A_WIDTH + g * GROUP_W:2 * A_WIDTH + (g + 1) * GROUP_W]
    parts = _hgrn_inputs(zc, lb, gnorm, chunked)
    for i, part in enumerate(parts):
        zc_ref[rows, i * C_WIDTH:(i + 1) * C_WIDTH] = part
    if chunked:
        q, b, k = parts[0:3]
        for i, part in enumerate(_hgrn_level_operands(q, b, k)):
            zl_ref[rows, i * C_WIDTH:(i + 1) * C_WIDTH] = part


def _inproj_body(x_ref, g_ref, w_ref, cos_ref, sin_ref, lb_ref, gn_ref, q_ref, k_ref, v_ref, zc_ref, zl_ref):
    tm = x_ref.shape[0]
    sub = min(tm, SUB_ROWS)
    for r0 in range(0, tm, sub):
        rows = slice(r0, r0 + sub)
        h = _rmsnorm(x_ref[rows, :], g_ref[...]).astype(BF16)
        za = jnp.dot(h, w_ref[:, 0:COL_B], preferred_element_type=F32)
        zc = jnp.dot(h, w_ref[:, COL_C:COL_G], preferred_element_type=F32)
        _store_qkv_and_hgrn(za, zc, cos_ref[rows, :], sin_ref[rows, :], lb_ref[...], gn_ref[...],
                            q_ref, k_ref, v_ref, zc_ref, rows, zl_ref)


def _inproj(x2d, norm, w_in, cos, sin, lb, gnorm, tm):
    m = x2d.shape[0]
    assert tm % CHUNK == 0
    n_pos = cos.shape[0] // tm
    row = lambda w: pl.BlockSpec((tm, w), lambda i: (i, 0))
    tab = pl.BlockSpec((tm, GROUP_W), lambda i: (i % n_pos, 0))
    grp = pl.BlockSpec((3, tm, GROUP_W), lambda i: (0, i, 0))
    return pl.pallas_call(
        _inproj_body,
        grid=(m // tm,),
        in_specs=[row(D_MODEL), _const_spec((1, D_MODEL)), _const_spec(w_in.shape),
                  tab, tab, _const_spec((1, C_WIDTH)), _const_spec((1, C_WIDTH))],
        out_specs=[grp] * 3 + [row(5 * C_WIDTH), row(2 * len(LEVELS) * C_WIDTH)],
        out_shape=[jax.ShapeDtypeStruct((3, m, GROUP_W), F32)] * 3
        + [jax.ShapeDtypeStruct((m, 5 * C_WIDTH), F32), jax.ShapeDtypeStruct((m, 2 * len(LEVELS) * C_WIDTH), BF16)],
        compiler_params=_params(1),
        name="inproj",
    )(x2d, norm.reshape(1, D_MODEL), w_in, cos, sin, lb.reshape(1, C_WIDTH), gnorm.reshape(1, C_WIDTH))


def _sample_inproj_body(x_ref, g_ref, w_ref, cos_ref, sin_ref, lb_ref, gn_ref,
                        q_ref, k_ref, v_ref, zc_ref, zb_ref, zg_ref, wbf_ref, z_scr, *, n_blocks):
    j = pl.program_id(0)
    w = w_ref[...].astype(BF16)
    wbf_ref[...] = w
    h = _rmsnorm(x_ref[...], g_ref[...]).astype(BF16)
    z_scr[j] = jnp.dot(h, w, preferred_element_type=F32)

    @pl.when(j == n_blocks - 1)
    def _():
        z = jnp.concatenate([z_scr[i] for i in range(n_blocks)], axis=1)
        _store_qkv_and_hgrn(z[:, 0:COL_B], z[:, COL_C:COL_G], cos_ref[...], sin_ref[...], lb_ref[...],
                            gn_ref[...], q_ref, k_ref, v_ref, zc_ref, slice(None))
        zb_ref[...] = z[:, COL_B:COL_C]
        zg_ref[...] = z[:, COL_G:IN_COLS]


def _sample_inproj(x2d, norm, w_in_f32, layer, cos, sin, lb, gnorm, n_blocks=5):
    m = x2d.shape[0]
    bw = IN_COLS // n_blocks
    assert bw * n_blocks == IN_COLS and bw % 128 == 0
    widths = (5 * C_WIDTH, 3 * CONV_WIDTH, N_BRANCH * D_MODEL)
    return pl.pallas_call(
        functools.partial(_sample_inproj_body, n_blocks=n_blocks),
        grid=(n_blocks,),
        in_specs=[_const_spec((m, D_MODEL)), _const_spec((1, D_MODEL)),
                  pl.BlockSpec((None, D_MODEL, bw), lambda j: (layer, 0, j)),
                  _const_spec((m, GROUP_W)), _const_spec((m, GROUP_W)),
                  _const_spec((1, C_WIDTH)), _const_spec((1, C_WIDTH))],
        out_specs=[_const_spec((3, m, GROUP_W))] * 3 + [_const_spec((m, w)) for w in widths]
        + [pl.BlockSpec((D_MODEL, bw), lambda j: (0, j))],
        out_shape=[jax.ShapeDtypeStruct((3, m, GROUP_W), F32)] * 3
        + [jax.ShapeDtypeStruct((m, w), F32) for w in widths]
        + [jax.ShapeDtypeStruct((D_MODEL, IN_COLS), BF16)],
        scratch_shapes=[pltpu.VMEM((n_blocks, m, bw), F32)],
        compiler_params=_params(1),
        name="sample_inproj",
    )(x2d, norm.reshape(1, D_MODEL), w_in_f32, cos, sin, lb.reshape(1, C_WIDTH), gnorm.reshape(1, C_WIDTH))


def _rope_tables(pos):
    half = HEAD_DIM // 2
    inv = ROPE_THETA ** (-jnp.arange(half, dtype=F32) / half)
    ang = pos.astype(F32)[:, None] * inv[None, :]
    cos, sin = jnp.cos(ang), jnp.sin(ang)
    return jnp.tile(cos, (1, 4)), jnp.tile(jnp.concatenate([-sin, sin], axis=1), (1, 2))


def _attend(tiles):
    head0 = lax.broadcasted_iota(jnp.int32, (BAND, GROUP_W), 1) < HEAD_DIM
    scale = HEAD_DIM ** -0.5
    q2 = [jnp.concatenate([jnp.where(head0, q * scale, 0.0), jnp.where(head0, 0.0, q * scale)],
                          axis=0).astype(BF16) for q, _ in tiles]
    keys = [jnp.concatenate([k for k, _, _ in blocks], axis=0) for _, blocks in tiles]
    vals = [jnp.concatenate([v1 for _, v1, _ in blocks], axis=0) for _, blocks in tiles]
    valid = [jnp.concatenate([ok for _, _, ok in blocks], axis=1) for _, blocks in tiles]
    s = [jnp.where(ok, _dot_nt(qq, kk), NEG_BIG) for qq, kk, ok in zip(q2, keys, valid)]
    m = [jnp.max(x, axis=-1, keepdims=True) for x in s]
    p = [jnp.exp(x - mm).astype(BF16) for x, mm in zip(s, m)]
    ext = [jnp.dot(pp, vv, preferred_element_type=F32) for pp, vv in zip(p, vals)]
    results = []
    for e, mm in zip(ext, m):
        den = e[:, GROUP_W:]
        o2 = e[:, 0:GROUP_W] / den
        lse2 = mm + jnp.log(den)
        results.append((jnp.where(head0, o2[0:BAND], o2[BAND:]), jnp.where(head0, lse2[0:BAND], lse2[BAND:])))
    return results


def _attn_prompt_body(q_ref, k_ref, v_ref, ya_ref, c0_ref, c1_ref, c2_ref, o_scr, lse_scr, *, seq):
    qi = lax.broadcasted_iota(jnp.int32, (2 * BAND, BAND), 0) % BAND
    kj = lax.broadcasted_iota(jnp.int32, (2 * BAND, BAND), 1)
    cur_valid = kj <= qi
    prev_valid = kj >= qi
    ones = jnp.ones((BAND, GROUP_W), BF16)

    def rows(start, dil):
        return pl.ds(start, BAND, stride=dil) if dil > 1 else pl.ds(start, BAND)

    def key_block(g, r):
        return k_ref[g, r, :].astype(BF16), jnp.concatenate([v_ref[g, r, :].astype(BF16), ones], axis=1)

    def run(g, dil, chains):
        tiles, slices = [], []
        for starts, first_prev in chains:
            chain = [rows(s, dil) for s in starts]
            keys = [key_block(g, r) for r in chain]
            for j, r in enumerate(chain):
                blocks = [(*keys[j], cur_valid)]
                if j > 0:
                    blocks.append((*keys[j - 1], prev_valid))
                elif first_prev is not None:
                    blocks.append((*key_block(g, rows(first_prev[0], dil)), prev_valid & first_prev[1]))
                tiles.append((q_ref[g, r, :], blocks))
            slices += chain
        for r, (o, lse) in zip(slices, _attend(tiles)):
            o_scr[g, r, :] = o
            lse_scr[g, r, :] = lse

    for g, (_, dil) in enumerate(DSWA_GROUPS):
        n_blocks = seq // dil // BAND
        if dil == 1:
            def dense(i, carry):
                s0 = pl.multiple_of(8 * i * BAND, BAND)
                prev = pl.multiple_of(jnp.maximum(8 * i - 1, 0) * BAND, BAND)
                run(g, dil, [([s0 + u * BAND for u in range(8)], (prev, i > 0))])
                return carry

            lax.fori_loop(0, n_blocks // 8, dense, 0)
        elif n_blocks > 1:
            def residue(r, carry, g=g, dil=dil, n_blocks=n_blocks):
                run(g, dil, [([2 * r + u + i * BAND * dil for i in range(n_blocks)], None) for u in range(2)])
                return carry

            lax.fori_loop(0, dil // 2, residue, 0)
        else:
            def residues(j, carry, g=g, dil=dil):
                run(g, dil, [([8 * j + u], None) for u in range(8)])
                return carry

            lax.fori_loop(0, dil // 8, residues, 0)

    step = 256

    def merge(i, carry):
        sl = pl.ds(pl.multiple_of(i * step, step), step)
        lse = [lse_scr[g, sl, :] for g in range(3)]
        m = jnp.maximum(jnp.maximum(lse[0], lse[1]), lse[2])
        e = [jnp.exp(x - m) for x in lse]
        acc = e[0] * o_scr[0, sl, :] + e[1] * o_scr[1, sl, :] + e[2] * o_scr[2, sl, :]
        ya_ref[0, sl, :] = acc / (e[0] + e[1] + e[2])
        return carry

    lax.fori_loop(0, seq // step, merge, 0)

    for g, (c_ref, (win, _)) in enumerate(zip((c0_ref, c1_ref, c2_ref), DSWA_GROUPS)):
        for j in range(0, win, 512):
            w = min(512, win - j)
            c_ref[0:GROUP_W, j:j + w] = k_ref[g, seq - win + j:seq - win + j + w, :].T
            c_ref[GROUP_W:2 * GROUP_W, j:j + w] = v_ref[g, seq - win + j:seq - win + j + w, :].T


def _attn_prompt(q, k, v, layer, caches):
    _, bsz, seq, _ = q.shape
    depth = 2
    assert seq % (BAND * 16) == 0 and seq >= 2048
    qkv_spec = pl.BlockSpec((3, None, seq, GROUP_W), lambda b: (0, b, 0, 0))
    cache_specs = [pl.BlockSpec((None, None, 2 * GROUP_W, win), lambda b: (layer, b, 0, 0))
                   for win, _ in DSWA_GROUPS]
    cache_shapes = [jax.ShapeDtypeStruct((depth, bsz, 2 * GROUP_W, win), F32) for win, _ in DSWA_GROUPS]
    in_specs = [qkv_spec] * 3
    args = [q, k, v]
    aliases = {}
    if caches is not None:
        in_specs = in_specs + [pl.BlockSpec(memory_space=pl.ANY)] * 3
        args = args + list(caches)
        aliases = {3: 1, 4: 2, 5: 3}

    def body(*refs):
        ins, rest = refs[:3], refs[3 + (3 if caches is not None else 0):]
        _attn_prompt_body(*ins, *rest, seq=seq)

    return pl.pallas_call(
        body,
        grid=(bsz,),
        in_specs=in_specs,
        out_specs=[pl.BlockSpec((1, seq, GROUP_W), lambda b: (b, 0, 0))] + cache_specs,
        out_shape=[jax.ShapeDtypeStruct((bsz, seq, GROUP_W), F32)] + cache_shapes,
        scratch_shapes=[pltpu.VMEM((3, seq, GROUP_W), F32), pltpu.VMEM((3, seq, GROUP_W), F32)],
        input_output_aliases=aliases,
        compiler_params=_params(1),
        name="attn_prompt",
    )(*args)


def _log_forget_and_kgate(zf, lb):
    e = jnp.exp(-jnp.abs(zf))
    log_sig = -(jnp.maximum(-zf, 0.0) + jnp.log(1.0 + e))
    a = jnp.log(jnp.maximum(lb, LB_FLOOR))
    c = jnp.log1p(-lb) + log_sig
    log_f = jnp.maximum(a, c) + jnp.log(1.0 + jnp.exp(-jnp.abs(a - c)))
    kgate = (1.0 - lb) * (jnp.where(zf >= 0.0, e, 1.0) / (1.0 + e))
    return log_f, kgate


def _silu(x):
    return x * jax.nn.sigmoid(x)


def _split3(x):
    hi = x.astype(BF16)
    r1 = x - hi.astype(F32)
    mid = r1.astype(BF16)
    lo = (r1 - mid.astype(F32)).astype(BF16)
    return hi, mid, lo


def _hgrn_inputs(zc, lb, gnorm, chunked):
    q, zf, v, og = (zc[:, i * C_WIDTH:(i + 1) * C_WIDTH] for i in range(4))
    log_f, k = _log_forget_and_kgate(zf, lb)
    if chunked:
        n = zc.shape[0]
        r = lax.broadcasted_iota(jnp.int32, (n, n), 0)
        c = lax.broadcasted_iota(jnp.int32, (n, n), 1)
        tri = jnp.where((c <= r) & ((r // CHUNK) == (c // CHUNK)), 1.0, 0.0).astype(BF16)
        b = functools.reduce(jnp.add, [jnp.dot(tri, p, preferred_element_type=F32) for p in _split3(log_f)])
    else:
        b = log_f
    return q, b, k, v, gnorm * _silu(og)


def _hgrn_level_operands(q, b, k):
    n = q.shape[0]
    row = lax.broadcasted_iota(jnp.int32, (n, C_WIDTH), 0)
    out = []
    for m in LEVELS:
        ref_q, ref_k = [], []
        for j in range(n // m):
            own = b[j * m:(j + 1) * m, :]
            if j % 2 == 1:
                ref_q.append(jnp.broadcast_to(b[j * m - 1:j * m, :], (m, C_WIDTH)))
                ref_k.append(own)
            else:
                ref_q.append(own)
                ref_k.append(jnp.broadcast_to(b[(j + 1) * m - 1:(j + 1) * m, :], (m, C_WIDTH)))
        upper = (row // m) % 2 == 1
        qt = jnp.where(upper, q * jnp.exp(jnp.minimum(b - jnp.concatenate(ref_q, axis=0), 0.0)), 0.0)
        kt = jnp.where(upper, 0.0, k * jnp.exp(jnp.minimum(jnp.concatenate(ref_k, axis=0) - b, 0.0)))
        out += [qt.astype(BF16), kt.astype(BF16)]
    return out


def _hgrn_chunks(seqs, cst):
    n = CHUNK
    ns = len(seqs)
    ones_bd = cst["ones_bd"]

    def head_sums(x):
        return jnp.dot(x.astype(BF16), ones_bd, preferred_element_type=F32)

    tiles = (n // SUBLANES, SUBLANES, C_WIDTH)
    ys = []
    for q, b, k, *_ in seqs:
        q3, k3, b3 = (x.reshape(tiles) for x in (q, k, b))
        f3 = jnp.exp(jnp.minimum(b3 - pltpu.roll(b3, 1, 1), 0.0))
        ys.append((q * k).astype(BF16))
        decay = None
        for d in range(1, SUB):
            decay = f3 if d == 1 else decay * pltpu.roll(f3, d - 1, 1)
            ys.append((q3 * pltpu.roll(k3, d, 1) * decay).reshape(n, C_WIDTH).astype(BF16))
    att_d = head_sums(jnp.concatenate(ys, axis=0)).astype(BF16)
    lvl_scores = []
    for lvl in range(len(LEVELS)):
        k_stack = [jnp.concatenate([s[5][2 * lvl + 1]] * C_HEADS, axis=0) * cst["stack"] for s in seqs]
        lvl_scores.append([_dot_nt(s[5][2 * lvl], ks).astype(BF16) for s, ks in zip(seqs, k_stack)])
    q_hat = [s[0] * jnp.exp(s[1]) for s in seqs]
    b_last = [s[1][n - 1:n, :] for s in seqs]
    k_hat = [s[2] * jnp.exp(bl - s[1]) for s, bl in zip(seqs, b_last)]
    o_st = [_dot_nt(qh, s[6]) for qh, s in zip(q_hat, seqs)]
    upd = [_dot_tn(s[3], kh) for s, kh in zip(seqs, k_hat)]

    att = [functools.reduce(jnp.add, [att_d[(i * SUB + d) * n:(i * SUB + d + 1) * n, :] * cst["place"][d]
                                      for d in range(SUB)]) for i in range(ns)]
    for lvl, scores in enumerate(lvl_scores):
        att = [x + y * cst["same_block"][lvl] for x, y in zip(att, scores)]

    v_stack = [jnp.concatenate([s[3].astype(BF16)] * C_HEADS, axis=0) * cst["stack"] for s in seqs]
    o = [jnp.dot(x, vs, preferred_element_type=F32) + y for x, vs, y in zip(att, v_stack, o_st)]
    st = [s[6] * jnp.exp(bl) + u * cst["same_head"] for s, bl, u in zip(seqs, b_last, upd)]

    sq = [x * x for x in o]
    sq_hi = [x.astype(BF16) for x in sq]
    sq_lo = [(x - h.astype(F32)).astype(BF16) for x, h in zip(sq, sq_hi)]
    ms = head_sums(jnp.concatenate(sq_hi + sq_lo, axis=0))
    out = []
    for i, (x, s) in enumerate(zip(o, seqs)):
        m2 = ms[i * n:(i + 1) * n, :] + ms[(ns + i) * n:(ns + i + 1) * n, :]
        out.append((x * lax.rsqrt(m2 * (1.0 / 64) + EPS) * s[4], st[i]))
    return out


def _hgrn_constants():
    n = CHUNK
    hr = lax.broadcasted_iota(jnp.int32, (C_WIDTH, C_WIDTH), 0) // 64
    hc = lax.broadcasted_iota(jnp.int32, (C_WIDTH, C_WIDTH), 1) // 64
    same_head = jnp.where(hr == hc, 1.0, 0.0)
    srow = lax.broadcasted_iota(jnp.int32, (n, C_HEADS * n), 0)
    scol = lax.broadcasted_iota(jnp.int32, (n, C_HEADS * n), 1) % n
    return {
        "ones_bd": same_head.astype(BF16),
        "same_head": same_head,
        "stack": same_head.astype(BF16),
        "place": [jnp.where((scol == srow - d) & (srow % SUB >= d), 1.0, 0.0).astype(BF16) for d in range(SUB)],
        "same_block": [jnp.where(srow // (2 * m) == scol // (2 * m), 1.0, 0.0).astype(BF16) for m in LEVELS],
    }


def _hgrn_prompt_body(zc_ref, zl_ref, yc_ref, st_ref, st_scr, *, rows, n_seq):
    t = pl.program_id(1)

    @pl.when(t == 0)
    def _():
        st_scr[...] = jnp.zeros_like(st_scr)

    cst = _hgrn_constants()

    def chunk(c, carry):
        sl = pl.ds(pl.multiple_of(c * CHUNK, CHUNK), CHUNK)
        seqs = [(*(zc_ref[i, sl, j * C_WIDTH:(j + 1) * C_WIDTH] for j in range(5)),
                 [zl_ref[i, sl, j * C_WIDTH:(j + 1) * C_WIDTH] for j in range(2 * len(LEVELS))], st_scr[i])
                for i in range(n_seq)]
        for i, (o, st) in enumerate(_hgrn_chunks(seqs, cst)):
            st_scr[i] = st
            yc_ref[i, sl, :] = o
        return carry

    lax.fori_loop(0, rows // CHUNK, chunk, 0)

    @pl.when(t == pl.num_programs(1) - 1)
    def _():
        for i in range(n_seq):
            for h in range(C_HEADS):
                st_ref[i, h] = st_scr[i, h * 64:(h + 1) * 64, h * 64:(h + 1) * 64]


def _hgrn_prompt(zc, zl, rows=ROW_TILE // 2):
    bsz, seq, _ = zc.shape
    n_seq = max(n for n in (8, 4, 2, 1) if bsz % n == 0)
    assert CHUNK == 64
    yc, st = pl.pallas_call(
        functools.partial(_hgrn_prompt_body, rows=rows, n_seq=n_seq),
        grid=(bsz // n_seq, seq // rows),
        in_specs=[pl.BlockSpec((n_seq, rows, 5 * C_WIDTH), lambda b, t: (b, t, 0)),
                  pl.BlockSpec((n_seq, rows, 2 * len(LEVELS) * C_WIDTH), lambda b, t: (b, t, 0))],
        out_specs=[pl.BlockSpec((n_seq, rows, C_WIDTH), lambda b, t: (b, t, 0)),
                   pl.BlockSpec((n_seq, C_HEADS, 64, 64), lambda b, t: (b, 0, 0, 0))],
        out_shape=[jax.ShapeDtypeStruct((bsz, seq, C_WIDTH), F32),
                   jax.ShapeDtypeStruct((bsz, C_HEADS, 64, 64), F32)],
        scratch_shapes=[pltpu.VMEM((n_seq, C_WIDTH, C_WIDTH), F32)],
        compiler_params=_params(2),
        name="hgrn_prompt",
    )(zc, zl)
    return yc, jnp.swapaxes(st, -1, -2)


def _gated_merge(x, ya, yb, yc, zg, bg, wpa, wpb, wpc, wo):
    g = jax.nn.sigmoid(zg + bg)
    mix = (g[:, 0:D_MODEL] * _dot(ya, wpa) + g[:, D_MODEL:2 * D_MODEL] * _dot(yb, wpb)
           + g[:, 2 * D_MODEL:3 * D_MODEL] * _dot(yc, wpc))
    return x + _dot(mix, wo)


def _project_conv_and_gates(x, g1, w_ref):
    h = _rmsnorm(x, g1).astype(BF16)
    return (jnp.dot(h, w_ref[:, COL_B:COL_C], preferred_element_type=F32),
            jnp.dot(h, w_ref[:, COL_G:IN_COLS], preferred_element_type=F32))


def _merge_prompt_body(x_ref, ya_ref, yc_ref, g1_ref, w_ref, bg_ref, cw_ref, wpa_ref, wpb_ref, wpc_ref,
                       wo_ref, x1_ref, nc_ref, u_scr, *, rows):
    @pl.when(pl.program_id(1) == 0)
    def _():
        u_scr[0:8, :] = jnp.zeros((8, CONV_WIDTH), F32)

    sub = min(rows, ROW_TILE)
    for r0 in range(0, rows, sub):
        sl = slice(r0, r0 + sub)
        x = x_ref[0, sl, :]
        zb, zg = _project_conv_and_gates(x, g1_ref[...], w_ref)
        u = zb[:, 2 * CONV_WIDTH:3 * CONV_WIDTH] * zb[:, 0:CONV_WIDTH]
        u_scr[8 + r0:8 + r0 + sub, :] = u
        conv = (cw_ref[0:1, :] * u_scr[pl.ds(6 + r0, sub), :] + cw_ref[1:2, :] * u_scr[pl.ds(7 + r0, sub), :]
                + cw_ref[2:3, :] * u)
        yb = zb[:, CONV_WIDTH:2 * CONV_WIDTH] * conv
        x1_ref[0, sl, :] = _gated_merge(x, ya_ref[0, sl, :], yb, yc_ref[0, sl, :], zg, bg_ref[...],
                                        wpa_ref[...], wpb_ref[...], wpc_ref[...], wo_ref[...])
    nc_ref[0] = u_scr[8 + rows - (CONV_K - 1):8 + rows, :]
    u_scr[0:8, :] = u_scr[rows:rows + 8, :]


def _merge_prompt(x, ya, yc, norm1, w_in, b_gate, conv_w, wpa, wpb, wpc, wo, rows=2 * ROW_TILE):
    bsz, seq, _ = x.shape
    blk = lambda w: pl.BlockSpec((1, rows, w), lambda b, t: (b, t, 0))
    return pl.pallas_call(
        functools.partial(_merge_prompt_body, rows=rows),
        grid=(bsz, seq // rows),
        in_specs=[blk(D_MODEL), blk(GROUP_W), blk(C_WIDTH),
                  _const_spec((1, D_MODEL)), _const_spec(w_in.shape),
                  _const_spec((1, N_BRANCH * D_MODEL)), _const_spec((CONV_K, CONV_WIDTH)),
                  _const_spec(wpa.shape), _const_spec(wpb.shape), _const_spec(wpc.shape), _const_spec(wo.shape)],
        out_specs=[blk(D_MODEL), pl.BlockSpec((1, CONV_K - 1, CONV_WIDTH), lambda b, t: (b, 0, 0))],
        out_shape=[jax.ShapeDtypeStruct((bsz, seq, D_MODEL), F32),
                   jax.ShapeDtypeStruct((bsz, CONV_K - 1, CONV_WIDTH), F32)],
        scratch_shapes=[pltpu.VMEM((rows + 8, CONV_WIDTH), F32)],
        compiler_params=_params(2),
        name="merge_prompt",
    )(x, ya, yc, norm1.reshape(1, D_MODEL), w_in, b_gate.reshape(1, -1), conv_w, wpa, wpb, wpc, wo)


def _merge_sample_body(x_ref, ya_ref, yc_ref, zb_ref, zg_ref, p0_ref, p1_ref, bg_ref, cw_ref, wpa_ref,
                       wpb_ref, wpc_ref, wo_ref, x1_ref, u_ref, wpa_bf_ref, wpb_bf_ref, wpc_bf_ref, wo_bf_ref):
    bf = []
    for w_ref, bf_ref in ((wpa_ref, wpa_bf_ref), (wpb_ref, wpb_bf_ref), (wpc_ref, wpc_bf_ref),
                          (wo_ref, wo_bf_ref)):
        bf.append(w_ref[...].astype(BF16))
        bf_ref[...] = bf[-1]
    zb = zb_ref[...]
    u = zb[:, 2 * CONV_WIDTH:3 * CONV_WIDTH] * zb[:, 0:CONV_WIDTH]
    conv = cw_ref[0:1, :] * p0_ref[...] + cw_ref[1:2, :] * p1_ref[...] + cw_ref[2:3, :] * u
    yb = zb[:, CONV_WIDTH:2 * CONV_WIDTH] * conv
    u_ref[...] = u
    x1_ref[...] = _gated_merge(x_ref[...], ya_ref[...], yb, yc_ref[...], zg_ref[...], bg_ref[...], *bf)


def _merge_sample(x, ya, yc, zb, zg, prev0, prev1, b_gate, conv_w, layer, w_pa, w_pb, w_pc, w_o):
    m = x.shape[0]
    small = (x, ya, yc, zb, zg, prev0, prev1, b_gate.reshape(1, -1), conv_w)
    stacks = (w_pa, w_pb, w_pc, w_o)
    return pl.pallas_call(
        _merge_sample_body,
        grid=(1,),
        in_specs=[_const_spec(a.shape) for a in small] + [_layer_spec(w, layer) for w in stacks],
        out_specs=[_const_spec((m, D_MODEL)), _const_spec((m, CONV_WIDTH))]
        + [_const_spec(w.shape[1:]) for w in stacks],
        out_shape=[jax.ShapeDtypeStruct((m, D_MODEL), F32), jax.ShapeDtypeStruct((m, CONV_WIDTH), F32)]
        + [jax.ShapeDtypeStruct(w.shape[1:], BF16) for w in stacks],
        compiler_params=_params(1),
        name="merge_sample",
    )(*small, *stacks)


def _as_column(row):
    return jnp.concatenate([jnp.broadcast_to(row[:, c:c + 128], (128, 128)).T for c in range(0, row.shape[1], 128)],
                           axis=0)


def _shift_window_buffers(k_ref, v_ref, cache_refs, new_refs):
    for g, (c_ref, n_ref, (win, _)) in enumerate(zip(cache_refs, new_refs, DSWA_GROUPS)):
        rolled = pltpu.roll(c_ref[...], win - 1, 1)
        new_col = jnp.concatenate([_as_column(k_ref[g]), _as_column(v_ref[g])], axis=0)
        last = lax.broadcasted_iota(jnp.int32, new_col.shape, 1) == 127
        if win > 128:
            n_ref[:, 0:win - 128] = rolled[:, 0:win - 128]
        n_ref[:, win - 128:win] = jnp.where(last, new_col, rolled[:, win - 128:win])


def _mlp_body(x_ref, g2_ref, wup_ref, wdn_ref, gf_ref, k_ref, v_ref, c0_ref, c1_ref, c2_ref,
              out_ref, n0_ref, n1_ref, n2_ref, *, final):
    _shift_window_buffers(k_ref, v_ref, (c0_ref, c1_ref, c2_ref), (n0_ref, n1_ref, n2_ref))
    tm = x_ref.shape[0]
    sub = min(tm, ROW_TILE)
    for r0 in range(0, tm, sub):
        rows = slice(r0, r0 + sub)
        x = x_ref[rows, :]
        h = _rmsnorm(x, g2_ref[...]).astype(BF16)
        acc = x
        for c in range(D_FF // D_MODEL):
            cols = slice(c * D_MODEL, (c + 1) * D_MODEL)
            a = jnp.maximum(jnp.dot(h, wup_ref[:, cols], preferred_element_type=F32), 0.0)
            acc = acc + jnp.dot((a * a).astype(BF16), wdn_ref[cols, :], preferred_element_type=F32)
        out_ref[rows, :] = _rmsnorm(acc, gf_ref[...]) if final else acc


def _mlp(x2d, norm2, wup, wdn, norm_f, final, tm, k_new, v_new, caches, layer, new_caches):
    m = x2d.shape[0]
    depth, n = caches[0].shape[:2]
    steps = m // tm
    assert n <= steps
    row = pl.BlockSpec((tm, D_MODEL), lambda i: (i, 0))
    seq = lambda i: jnp.minimum(i, n - 1)
    grp = pl.BlockSpec((3, None, 1, GROUP_W), lambda i: (0, seq(i), 0, 0))
    cache_specs = [pl.BlockSpec((None, None, 2 * GROUP_W, win), lambda i: (layer, seq(i), 0, 0))
                   for win, _ in DSWA_GROUPS]
    in_specs = [row, _const_spec((1, D_MODEL)), _const_spec(wup.shape), _const_spec(wdn.shape),
                _const_spec((1, D_MODEL)), grp, grp] + cache_specs
    args = [x2d, norm2.reshape(1, D_MODEL), wup, wdn, norm_f.reshape(1, D_MODEL),
            k_new.reshape(3, n, 1, GROUP_W), v_new.reshape(3, n, 1, GROUP_W), *caches]
    aliases = {}
    if new_caches is not None:
        aliases = {len(args) + i: 1 + i for i in range(3)}
        in_specs = in_specs + [pl.BlockSpec(memory_space=pl.ANY)] * 3
        args = args + list(new_caches)
    n_in = len(cache_specs) + 7

    def body(*refs):
        _mlp_body(*refs[:n_in], *refs[len(args):], final=final)

    return pl.pallas_call(
        body,
        grid=(steps,),
        in_specs=in_specs,
        out_specs=[row] + cache_specs,
        out_shape=[jax.ShapeDtypeStruct((m, D_MODEL), F32)]
        + [jax.ShapeDtypeStruct((depth, n, 2 * GROUP_W, win), F32) for win, _ in DSWA_GROUPS],
        input_output_aliases=aliases,
        compiler_params=_params(1),
        name="mlp",
    )(*args)


def _mlp_sample_body(x_ref, g2_ref, wup_ref, wdn_ref, gf_ref, out_ref, wup_bf_ref, wdn_bf_ref, acc_scr, *, final):
    c = pl.program_id(0)
    wu = wup_ref[...].astype(BF16)
    wd = wdn_ref[...].astype(BF16)
    wup_bf_ref[...] = wu
    wdn_bf_ref[...] = wd
    x = x_ref[...]
    h = _rmsnorm(x, g2_ref[...]).astype(BF16)
    a = jnp.maximum(jnp.dot(h, wu, preferred_element_type=F32), 0.0)
    part = jnp.dot((a * a).astype(BF16), wd, preferred_element_type=F32)

    @pl.when(c == 0)
    def _():
        acc_scr[...] = x + part

    @pl.when(c > 0)
    def _():
        acc_scr[...] = acc_scr[...] + part

    @pl.when(c == pl.num_programs(0) - 1)
    def _():
        acc = acc_scr[...]
        out_ref[...] = _rmsnorm(acc, gf_ref[...]) if final else acc


def _mlp_sample(x2d, norm2, layer, w_up, w_down, norm_f, final):
    m = x2d.shape[0]
    n_chunks = D_FF // D_MODEL
    return pl.pallas_call(
        functools.partial(_mlp_sample_body, final=final),
        grid=(n_chunks,),
        in_specs=[_const_spec((m, D_MODEL)), _const_spec((1, D_MODEL)),
                  pl.BlockSpec((None, D_MODEL, D_MODEL), lambda c: (layer, 0, c)),
                  pl.BlockSpec((None, D_MODEL, D_MODEL), lambda c: (layer, c, 0)),
                  _const_spec((1, D_MODEL))],
        out_specs=[_const_spec((m, D_MODEL)), pl.BlockSpec((D_MODEL, D_MODEL), lambda c: (0, c)),
                   pl.BlockSpec((D_MODEL, D_MODEL), lambda c: (c, 0))],
        out_shape=[jax.ShapeDtypeStruct((m, D_MODEL), F32), jax.ShapeDtypeStruct((D_MODEL, D_FF), BF16),
                   jax.ShapeDtypeStruct((D_FF, D_MODEL), BF16)],
        scratch_shapes=[pltpu.VMEM((m, D_MODEL), F32)],
        compiler_params=_params(1),
        name="mlp_sample",
    )(x2d, norm2.reshape(1, D_MODEL), w_up, w_down, norm_f.reshape(1, D_MODEL))


def _sample_mix_body(q_ref, k_ref, v_ref, c0_ref, c1_ref, c2_ref, zrow_ref, s_ref, ya_ref, yc_ref, s_out_ref):
    for i in range(q_ref.shape[1]):
        _sample_mix_one(q_ref.at[:, i], k_ref.at[:, i], v_ref.at[:, i], c0_ref.at[i], c1_ref.at[i], c2_ref.at[i],
                        zrow_ref.at[i], s_ref.at[i], ya_ref.at[i], yc_ref.at[i], s_out_ref.at[i])


def _sample_mix_one(q_ref, k_ref, v_ref, c0_ref, c1_ref, c2_ref, zrow_ref, s_ref, ya_ref, yc_ref, s_out_ref):
    lane = lax.broadcasted_iota(jnp.int32, (8, GROUP_W), 1)
    rowi = lax.broadcasted_iota(jnp.int32, (8, GROUP_W), 0)
    head_of_row = (lane // HEAD_DIM) == rowi
    head0 = lax.broadcasted_iota(jnp.int32, (1, GROUP_W), 1) < HEAD_DIM

    outs, lses = [], []
    for g, (c_ref, (win, dil)) in enumerate(zip((c0_ref, c1_ref, c2_ref), DSWA_GROUPS)):
        q, k_new, v_new = q_ref[g], k_ref[g], v_ref[g]
        buf = c_ref[...]
        pos = lax.broadcasted_iota(jnp.int32, (8, win), 1)
        q2 = jnp.where(head_of_row, jnp.broadcast_to(q, (8, GROUP_W)), 0.0)
        scale = HEAD_DIM ** -0.5
        s = jnp.where(pos % dil == 0, _dot(q2, buf[0:GROUP_W, :]) * scale, NEG_BIG)
        s_self = jnp.sum(q2 * k_new, axis=-1, keepdims=True) * scale
        m = jnp.maximum(jnp.max(s, axis=-1, keepdims=True), s_self)
        p, p_self = jnp.exp(s - m), jnp.exp(s_self - m)
        l = jnp.sum(p, axis=-1, keepdims=True) + p_self
        o = (_dot_nt(p, buf[GROUP_W:2 * GROUP_W, :]) + p_self * v_new) / l
        lse = m + jnp.log(l)
        outs.append(jnp.where(head0, o[0:1], o[1:2]))
        lses.append(jnp.where(head0, jnp.broadcast_to(lse[0:1], (1, GROUP_W)),
                              jnp.broadcast_to(lse[1:2], (1, GROUP_W))))
    m = jnp.maximum(jnp.maximum(lses[0], lses[1]), lses[2])
    e = [jnp.exp(x - m) for x in lses]
    ya_ref[...] = (e[0] * outs[0] + e[1] * outs[1] + e[2] * outs[2]) / (e[0] + e[1] + e[2])

    zrow = zrow_ref[...]
    q_col, log_f, k_col = (_as_column(zrow[:, j * C_WIDTH:(j + 1) * C_WIDTH])[:, 0:64] for j in range(3))
    v_row = zrow[:, 3 * C_WIDTH:4 * C_WIDTH]
    gate_row = zrow[:, 4 * C_WIDTH:5 * C_WIDTH]
    v_sel = jnp.concatenate([jnp.broadcast_to(v_row[:, h * 64:(h + 1) * 64], (64, 64)) for h in range(C_HEADS)],
                            axis=0)
    s_new = jnp.exp(log_f) * s_ref[...] + k_col * v_sel
    s_out_ref[...] = s_new
    qs = q_col * s_new
    o4 = jnp.concatenate([jnp.sum(qs[h * 64:(h + 1) * 64, :], axis=0, keepdims=True) for h in range(C_HEADS)],
                         axis=0)
    gate4 = jnp.concatenate([gate_row[:, h * 64:(h + 1) * 64] for h in range(C_HEADS)], axis=0)
    o4 = o4 * lax.rsqrt(jnp.mean(o4 * o4, axis=-1, keepdims=True) + EPS) * gate4
    yc_ref[...] = o4


def _sample_mix(q, k, v, caches, zc, state, layer, new_state):
    n = q.shape[1]
    per = max(p for p in (4, 2, 1) if n % p == 0)
    row3 = lambda w: pl.BlockSpec((per, 1, w), lambda b: (b, 0, 0))
    grp = pl.BlockSpec((3, per, 1, GROUP_W), lambda b: (0, b, 0, 0))
    cache_specs = [pl.BlockSpec((None, per, 2 * GROUP_W, win), lambda b: (layer, b, 0, 0))
                   for win, _ in DSWA_GROUPS]
    state_spec = pl.BlockSpec((None, per, C_WIDTH, 64), lambda b: (layer, b, 0, 0))
    in_specs = [grp] * 3 + cache_specs + [row3(5 * C_WIDTH), state_spec]
    args = [q.reshape(3, n, 1, GROUP_W), k.reshape(3, n, 1, GROUP_W), v.reshape(3, n, 1, GROUP_W),
            *caches, zc.reshape(n, 1, 5 * C_WIDTH), state]
    n_in = len(args)
    aliases = {}
    if new_state is not None:
        aliases = {n_in: 2}
        in_specs = in_specs + [pl.BlockSpec(memory_space=pl.ANY)]
        args = args + [new_state]

    def body(*refs):
        _sample_mix_body(*refs[:n_in], *refs[len(args):])

    return pl.pallas_call(
        body,
        grid=(n // per,),
        in_specs=in_specs,
        out_specs=[pl.BlockSpec((per, 1, GROUP_W), lambda b: (b, 0, 0)),
                   pl.BlockSpec((per, C_HEADS, 64), lambda b: (b, 0, 0)), state_spec],
        out_shape=[jax.ShapeDtypeStruct((n, 1, GROUP_W), F32), jax.ShapeDtypeStruct((n, C_HEADS, 64), F32),
                   jax.ShapeDtypeStruct(state.shape, F32)],
        input_output_aliases=aliases,
        compiler_params=_params(1),
        name="sample_mix",
    )(*args)


def kernel(x_prompt, x_sample, cache_kv_w128, cache_kv_w512, cache_kv_w2048, state_conv, state_hgrn, w_in, b_gate,
           norm1, conv_w, hgrn_lb, hgrn_norm, w_pa, w_pb, w_pc, w_o, norm2, w_up, w_down, norm_f):
    bp, tp, d = x_prompt.shape
    bs, ts, _ = x_sample.shape
    depth = w_in.shape[0]
    assert d == D_MODEL and ts == 1 and depth == 2
    assert PAST_LEN >= max(win for win, _ in DSWA_GROUPS)

    sm = jax.nn.softmax(hgrn_lb.astype(F32), axis=0)
    lower = jnp.cumsum(sm, axis=0) - sm[0:1]

    cos_p, sin_p = _rope_tables(jnp.arange(tp, dtype=jnp.int32))
    cos_s, sin_s = _rope_tables(jnp.full((bs,), PAST_LEN, dtype=jnp.int32))

    caches = [jnp.transpose(c, (0, 1, 3, 4, 5, 2)).reshape(depth, bs, 2 * GROUP_W, win)
              for c, (win, _) in zip((cache_kv_w128, cache_kv_w512, cache_kv_w2048), DSWA_GROUPS)]
    state = state_hgrn.reshape(depth, bs, C_WIDTH, 64)

    xp = x_prompt.reshape(bp * tp, d)
    xs = x_sample.reshape(bs, d)
    kv_p, kv_s = None, None
    conv_p, hgrn_p, conv_s, hgrn_s = [], [], [], None
    for l in range(depth):
        final = l == depth - 1

        q, k_s, v_s, zc, zb, zg, w_in_bf = _sample_inproj(xs, norm1[l], w_in, l, cos_s, sin_s, lower[l],
                                                          hgrn_norm[l])
        ya, yc, hgrn_s = _sample_mix(q, k_s, v_s, caches, zc, state, l, hgrn_s)
        x1, u, wpa, wpb, wpc, wo = _merge_sample(xs, ya.reshape(bs, GROUP_W), yc.reshape(bs, C_WIDTH), zb, zg,
                                                 state_conv[l, :, 0], state_conv[l, :, 1], b_gate[l], conv_w[l],
                                                 l, w_pa, w_pb, w_pc, w_o)
        xs, wup, wdn = _mlp_sample(x1, norm2[l], l, w_up, w_down, norm_f, final)
        conv_s.append(jnp.stack([state_conv[l, :, 1], u], axis=1))

        q, k, v, zc, zl = _inproj(xp, norm1[l], w_in_bf, cos_p, sin_p, lower[l], hgrn_norm[l], tm=2 * ROW_TILE)
        ya, *kv_p = _attn_prompt(q.reshape(3, bp, tp, -1), k.reshape(3, bp, tp, -1), v.reshape(3, bp, tp, -1),
                                 l, kv_p)
        yc, st = _hgrn_prompt(zc.reshape(bp, tp, -1), zl.reshape(bp, tp, -1))
        x1, nc = _merge_prompt(xp.reshape(bp, tp, d), ya, yc, norm1[l], w_in_bf,
                               b_gate[l], conv_w[l], wpa, wpb, wpc, wo)
        xp, *kv_s = _mlp(x1.reshape(bp * tp, d), norm2[l], wup, wdn, norm_f, final, ROW_TILE,
                         k_s, v_s, caches, l, kv_s)
        conv_p.append(nc)
        hgrn_p.append(st)

    def kv6(c, n):
        return jnp.transpose(c.reshape(depth, n, 2, 2, HEAD_DIM, c.shape[3]), (0, 1, 5, 2, 3, 4))

    return (xp.reshape(bp, tp, d), xs.reshape(bs, ts, d),
            kv6(kv_p[0], bp), kv6(kv_p[1], bp), kv6(kv_p[2], bp), jnp.stack(conv_p), jnp.stack(hgrn_p),
            kv6(kv_s[0], bs), kv6(kv_s[1], bs), kv6(kv_s[2], bs), jnp.stack(conv_s),
            hgrn_s.reshape(depth, bs, C_HEADS, 64, 64))
```

```python
import functools

import jax
import jax.numpy as jnp
from jax import lax
from jax.experimental import pallas as pl
from jax.experimental.pallas import tpu as pltpu

F32 = jnp.float32
BF16 = jnp.bfloat16

D_MODEL = 1024
HEAD_DIM = 64
DSWA_GROUPS = ((128, 1), (512, 4), (2048, 16))
BAND = 128
GROUP_W = 2 * HEAD_DIM
A_WIDTH = 3 * GROUP_W
CONV_WIDTH = 384
CONV_K = 3
C_HEADS = 4
C_DK = 64
C_WIDTH = C_HEADS * C_DK
N_BRANCH = 3
D_FF = 4 * D_MODEL
ROPE_THETA = 10000.0
EPS = 1e-6
NEG_BIG = -1e30
LB_FLOOR = 1e-30
PAST_LEN = 8192
IN_COLS = 3 * A_WIDTH + 3 * CONV_WIDTH + 4 * C_WIDTH + N_BRANCH * D_MODEL
COL_B = 3 * A_WIDTH
COL_C = COL_B + 3 * CONV_WIDTH
COL_G = COL_C + 4 * C_WIDTH
CHUNK = 64
SUBLANES = 8
SUB = SUBLANES
LEVELS = (32, 16, 8)
VMEM_LIMIT = 56 * 1024 * 1024
TILES_PER_TRIP = 8
LANES = 128
ROW_TILE = 512
SUB_ROWS = 256


def _params(n_grid):
    return pltpu.CompilerParams(dimension_semantics=("arbitrary",) * n_grid,
                                vmem_limit_bytes=VMEM_LIMIT)


def _const_spec(shape):
    nd = len(shape)
    return pl.BlockSpec(shape, lambda *_: (0,) * nd, pipeline_mode=pl.Buffered(1))


def _layer_spec(stacked, layer):
    zeros = (0,) * (stacked.ndim - 1)
    return pl.BlockSpec((None,) + stacked.shape[1:], lambda *_: (layer,) + zeros, pipeline_mode=pl.Buffered(1))


def _dot(a, b):
    return jnp.dot(a.astype(BF16), b.astype(BF16), preferred_element_type=F32)


def _dot_nt(a, b):
    return lax.dot_general(a.astype(BF16), b.astype(BF16), (((1,), (1,)), ((), ())),
                           preferred_element_type=F32)


def _dot_tn(a, b):
    return lax.dot_general(a.astype(BF16), b.astype(BF16), (((0,), (0,)), ((), ())),
                           preferred_element_type=F32)


def _rmsnorm(x, g):
    return x * lax.rsqrt(jnp.mean(x * x, axis=-1, keepdims=True) + EPS) * g


def _store_qkv_and_hgrn(za, zc, cos, sin, lb, gnorm, q_ref, k_ref, v_ref, zc_ref, rows, zl_ref=None):
    chunked = zl_ref is not None
    lane = lax.broadcasted_iota(jnp.int32, cos.shape, 1)
    first_half = (lane % HEAD_DIM) < HEAD_DIM // 2
    for out_ref, c0 in ((q_ref, 0), (k_ref, A_WIDTH)):
        for g in range(3):
            z = za[:, c0 + g * GROUP_W:c0 + (g + 1) * GROUP_W]
            partner = jnp.where(first_half, pltpu.roll(z, GROUP_W - HEAD_DIM // 2, 1),
                                pltpu.roll(z, HEAD_DIM // 2, 1))
            out_ref[g, rows, :] = z * cos + partner * sin
    for g in range(3):
        v_ref[g, rows, :] = za[:, 2 * A_WIDTH + g * GROUP_W:2 * A_WIDTH + (g + 1) * GROUP_W]
    parts = _hgrn_inputs(zc, lb, gnorm, chunked)
    for i, part in enumerate(parts):
        zc_ref[rows, i * C_WIDTH:(i + 1) * C_WIDTH] = part
    if chunked:
        q, b, k = parts[0:3]
        for i, part in enumerate(_hgrn_level_operands(q, b, k)):
            zl_ref[rows, i * C_WIDTH:(i + 1) * C_WIDTH] = part


def _inproj_body(x_ref, g_ref, w_ref, cos_ref, sin_ref, lb_ref, gn_ref, q_ref, k_ref, v_ref, zc_ref, zl_ref):
    tm = x_ref.shape[0]
    sub = min(tm, SUB_ROWS)
    for r0 in range(0, tm, sub):
        rows = slice(r0, r0 + sub)
        h = _rmsnorm(x_ref[rows, :], g_ref[...]).astype(BF16)
        za = jnp.dot(h, w_ref[:, 0:COL_B], preferred_element_type=F32)
        zc = jnp.dot(h, w_ref[:, COL_C:COL_G], preferred_element_type=F32)
        _store_qkv_and_hgrn(za, zc, cos_ref[rows, :], sin_ref[rows, :], lb_ref[...], gn_ref[...],
                            q_ref, k_ref, v_ref, zc_ref, rows, zl_ref)


def _inproj(x2d, norm, w_in, cos, sin, lb, gnorm, tm):
    m = x2d.shape[0]
    assert tm % CHUNK == 0
    n_pos = cos.shape[0] // tm
    row = lambda w: pl.BlockSpec((tm, w), lambda i: (i, 0))
    tab = pl.BlockSpec((tm, GROUP_W), lambda i: (i % n_pos, 0))
    grp = pl.BlockSpec((3, tm, GROUP_W), lambda i: (0, i, 0))
    return pl.pallas_call(
        _inproj_body,
        grid=(m // tm,),
        in_specs=[row(D_MODEL), _const_spec((1, D_MODEL)), _const_spec(w_in.shape),
                  tab, tab, _const_spec((1, C_WIDTH)), _const_spec((1, C_WIDTH))],
        out_specs=[grp] * 3 + [row(5 * C_WIDTH), row(2 * len(LEVELS) * C_WIDTH)],
        out_shape=[jax.ShapeDtypeStruct((3, m, GROUP_W), F32)] * 3
        + [jax.ShapeDtypeStruct((m, 5 * C_WIDTH), F32), jax.ShapeDtypeStruct((m, 2 * len(LEVELS) * C_WIDTH), BF16)],
        compiler_params=_params(1),
        name="inproj",
    )(x2d, norm.reshape(1, D_MODEL), w_in, cos, sin, lb.reshape(1, C_WIDTH), gnorm.reshape(1, C_WIDTH))


def _sample_inproj_body(x_ref, g_ref, w_ref, cos_ref, sin_ref, lb_ref, gn_ref,
                        q_ref, k_ref, v_ref, zc_ref, zb_ref, zg_ref, wbf_ref, z_scr, *, n_blocks):
    j = pl.program_id(0)
    w = w_ref[...].astype(BF16)
    wbf_ref[...] = w
    h = _rmsnorm(x_ref[...], g_ref[...]).astype(BF16)
    z_scr[j] = jnp.dot(h, w, preferred_element_type=F32)

    @pl.when(j == n_blocks - 1)
    def _():
        z = jnp.concatenate([z_scr[i] for i in range(n_blocks)], axis=1)
        _store_qkv_and_hgrn(z[:, 0:COL_B], z[:, COL_C:COL_G], cos_ref[...], sin_ref[...], lb_ref[...],
                            gn_ref[...], q_ref, k_ref, v_ref, zc_ref, slice(None))
        zb_ref[...] = z[:, COL_B:COL_C]
        zg_ref[...] = z[:, COL_G:IN_COLS]


def _sample_inproj(x2d, norm, w_in_f32, layer, cos, sin, lb, gnorm, n_blocks=5):
    m = x2d.shape[0]
    bw = IN_COLS // n_blocks
    assert bw * n_blocks == IN_COLS and bw % 128 == 0
    widths = (5 * C_WIDTH, 3 * CONV_WIDTH, N_BRANCH * D_MODEL)
    return pl.pallas_call(
        functools.partial(_sample_inproj_body, n_blocks=n_blocks),
        grid=(n_blocks,),
        in_specs=[_const_spec((m, D_MODEL)), _const_spec((1, D_MODEL)),
                  pl.BlockSpec((None, D_MODEL, bw), lambda j: (layer, 0, j)),
                  _const_spec((m, GROUP_W)), _const_spec((m, GROUP_W)),
                  _const_spec((1, C_WIDTH)), _const_spec((1, C_WIDTH))],
        out_specs=[_const_spec((3, m, GROUP_W))] * 3 + [_const_spec((m, w)) for w in widths]
        + [pl.BlockSpec((D_MODEL, bw), lambda j: (0, j))],
        out_shape=[jax.ShapeDtypeStruct((3, m, GROUP_W), F32)] * 3
        + [jax.ShapeDtypeStruct((m, w), F32) for w in widths]
        + [jax.ShapeDtypeStruct((D_MODEL, IN_COLS), BF16)],
        scratch_shapes=[pltpu.VMEM((n_blocks, m, bw), F32)],
        compiler_params=_params(1),
        name="sample_inproj",
    )(x2d, norm.reshape(1, D_MODEL), w_in_f32, cos, sin, lb.reshape(1, C_WIDTH), gnorm.reshape(1, C_WIDTH))


def _rope_tables(pos):
    half = HEAD_DIM // 2
    inv = ROPE_THETA ** (-jnp.arange(half, dtype=F32) / half)
    ang = pos.astype(F32)[:, None] * inv[None, :]
    cos, sin = jnp.cos(ang), jnp.sin(ang)
    return jnp.tile(cos, (1, 4)), jnp.tile(jnp.concatenate([-sin, sin], axis=1), (1, 2))


def _attend(tiles):
    head0 = lax.broadcasted_iota(jnp.int32, (BAND, GROUP_W), 1) < HEAD_DIM
    scale = HEAD_DIM ** -0.5
    q2 = [jnp.concatenate([jnp.where(head0, q * scale, 0.0), jnp.where(head0, 0.0, q * scale)],
                          axis=0).astype(BF16) for q, _ in tiles]
    keys = [jnp.concatenate([k for k, _, _ in blocks], axis=0) for _, blocks in tiles]
    vals = [jnp.concatenate([v1 for _, v1, _ in blocks], axis=0) for _, blocks in tiles]
    valid = [jnp.concatenate([ok for _, _, ok in blocks], axis=1) for _, blocks in tiles]
    s = [jnp.where(ok, _dot_nt(qq, kk), NEG_BIG) for qq, kk, ok in zip(q2, keys, valid)]
    m = [jnp.max(x, axis=-1, keepdims=True) for x in s]
    p = [jnp.exp(x - mm).astype(BF16) for x, mm in zip(s, m)]
    ext = [jnp.dot(pp, vv, preferred_element_type=F32) for pp, vv in zip(p, vals)]
    results = []
    for e, mm in zip(ext, m):
        den = e[:, GROUP_W:]
        o2 = e[:, 0:GROUP_W] / den
        lse2 = mm + jnp.log(den)
        results.append((jnp.where(head0, o2[0:BAND], o2[BAND:]), jnp.where(head0, lse2[0:BAND], lse2[BAND:])))
    return results


def _attn_prompt_body(q_ref, k_ref, v_ref, ya_ref, c0_ref, c1_ref, c2_ref, o_scr, lse_scr, *, seq):
    qi = lax.broadcasted_iota(jnp.int32, (2 * BAND, BAND), 0) % BAND
    kj = lax.broadcasted_iota(jnp.int32, (2 * BAND, BAND), 1)
    cur_valid = kj <= qi
    prev_valid = kj >= qi
    ones = jnp.ones((BAND, GROUP_W), BF16)

    def rows(start, dil):
        return pl.ds(start, BAND, stride=dil) if dil > 1 else pl.ds(start, BAND)

    def key_block(g, r):
        return k_ref[g, r, :].astype(BF16), jnp.concatenate([v_ref[g, r, :].astype(BF16), ones], axis=1)

    def run(g, dil, chains):
        tiles, slices = [], []
        for starts, first_prev in chains:
            chain = [rows(s, dil) for s in starts]
            keys = [key_block(g, r) for r in chain]
            for j, r in enumerate(chain):
                blocks = [(*keys[j], cur_valid)]
                if j > 0:
                    blocks.append((*keys[j - 1], prev_valid))
                elif first_prev is not None:
                    blocks.append((*key_block(g, rows(first_prev[0], dil)), prev_valid & first_prev[1]))
                tiles.append((q_ref[g, r, :], blocks))
            slices += chain
        for r, (o, lse) in zip(slices, _attend(tiles)):
            o_scr[g, r, :] = o
            lse_scr[g, r, :] = lse

    for g, (_, dil) in enumerate(DSWA_GROUPS):
        n_blocks = seq // dil // BAND
        if dil == 1:
            def dense(i, carry):
                s0 = pl.multiple_of(TILES_PER_TRIP * i * BAND, BAND)
                prev = pl.multiple_of(jnp.maximum(TILES_PER_TRIP * i - 1, 0) * BAND, BAND)
                run(g, dil, [([s0 + u * BAND for u in range(TILES_PER_TRIP)], (prev, i > 0))])
                return carry

            lax.fori_loop(0, n_blocks // TILES_PER_TRIP, dense, 0)
        else:
            per = max(1, TILES_PER_TRIP // n_blocks)
            assert dil % per == 0

            def residues(j, carry, g=g, dil=dil, n_blocks=n_blocks, per=per):
                run(g, dil, [([per * j + u + i * BAND * dil for i in range(n_blocks)], None) for u in range(per)])
                return carry

            lax.fori_loop(0, dil // per, residues, 0)

    step = 256

    def merge(i, carry):
        sl = pl.ds(pl.multiple_of(i * step, step), step)
        lse = [lse_scr[g, sl, :] for g in range(3)]
        m = jnp.maximum(jnp.maximum(lse[0], lse[1]), lse[2])
        e = [jnp.exp(x - m) for x in lse]
        acc = e[0] * o_scr[0, sl, :] + e[1] * o_scr[1, sl, :] + e[2] * o_scr[2, sl, :]
        ya_ref[0, sl, :] = acc / (e[0] + e[1] + e[2])
        return carry

    lax.fori_loop(0, seq // step, merge, 0)

    for g, (c_ref, (win, _)) in enumerate(zip((c0_ref, c1_ref, c2_ref), DSWA_GROUPS)):
        for j in range(0, win, ROW_TILE):
            w = min(ROW_TILE, win - j)
            c_ref[0:GROUP_W, j:j + w] = k_ref[g, seq - win + j:seq - win + j + w, :].T
            c_ref[GROUP_W:2 * GROUP_W, j:j + w] = v_ref[g, seq - win + j:seq - win + j + w, :].T


def _attn_prompt(q, k, v, layer, depth, caches):
    _, bsz, seq, _ = q.shape
    assert seq % (BAND * TILES_PER_TRIP) == 0
    assert all(seq >= win and seq % (dil * BAND) == 0 for win, dil in DSWA_GROUPS)
    qkv_spec = pl.BlockSpec((3, None, seq, GROUP_W), lambda b: (0, b, 0, 0))
    cache_specs = [pl.BlockSpec((None, None, 2 * GROUP_W, win), lambda b: (layer, b, 0, 0))
                   for win, _ in DSWA_GROUPS]
    cache_shapes = [jax.ShapeDtypeStruct((depth, bsz, 2 * GROUP_W, win), F32) for win, _ in DSWA_GROUPS]
    in_specs = [qkv_spec] * 3
    args = [q, k, v]
    aliases = {}
    if caches is not None:
        in_specs = in_specs + [pl.BlockSpec(memory_space=pl.ANY)] * 3
        args = args + list(caches)
        aliases = {3: 1, 4: 2, 5: 3}

    def body(*refs):
        ins, rest = refs[:3], refs[3 + (3 if caches is not None else 0):]
        _attn_prompt_body(*ins, *rest, seq=seq)

    return pl.pallas_call(
        body,
        grid=(bsz,),
        in_specs=in_specs,
        out_specs=[pl.BlockSpec((1, seq, GROUP_W), lambda b: (b, 0, 0))] + cache_specs,
        out_shape=[jax.ShapeDtypeStruct((bsz, seq, GROUP_W), F32)] + cache_shapes,
        scratch_shapes=[pltpu.VMEM((3, seq, GROUP_W), F32), pltpu.VMEM((3, seq, GROUP_W), F32)],
        input_output_aliases=aliases,
        compiler_params=_params(1),
        name="attn_prompt",
    )(*args)


def _log_forget_and_kgate(zf, lb):
    e = jnp.exp(-jnp.abs(zf))
    log_sig = -(jnp.maximum(-zf, 0.0) + jnp.log(1.0 + e))
    a = jnp.log(jnp.maximum(lb, LB_FLOOR))
    c = jnp.log1p(-lb) + log_sig
    log_f = jnp.maximum(a, c) + jnp.log(1.0 + jnp.exp(-jnp.abs(a - c)))
    kgate = (1.0 - lb) * (jnp.where(zf >= 0.0, e, 1.0) / (1.0 + e))
    return log_f, kgate


def _silu(x):
    return x * jax.nn.sigmoid(x)


def _split3(x):
    hi = x.astype(BF16)
    r1 = x - hi.astype(F32)
    mid = r1.astype(BF16)
    lo = (r1 - mid.astype(F32)).astype(BF16)
    return hi, mid, lo


def _hgrn_inputs(zc, lb, gnorm, chunked):
    q, zf, v, og = (zc[:, i * C_WIDTH:(i + 1) * C_WIDTH] for i in range(4))
    log_f, k = _log_forget_and_kgate(zf, lb)
    if chunked:
        n = zc.shape[0]
        r = lax.broadcasted_iota(jnp.int32, (n, n), 0)
        c = lax.broadcasted_iota(jnp.int32, (n, n), 1)
        tri = jnp.where((c <= r) & ((r // CHUNK) == (c // CHUNK)), 1.0, 0.0).astype(BF16)
        b = functools.reduce(jnp.add, [jnp.dot(tri, p, preferred_element_type=F32) for p in _split3(log_f)])
    else:
        b = log_f
    return q, b, k, v, gnorm * _silu(og)


def _hgrn_level_operands(q, b, k):
    n = q.shape[0]
    row = lax.broadcasted_iota(jnp.int32, (n, C_WIDTH), 0)
    out = []
    for m in LEVELS:
        ref_q, ref_k = [], []
        for j in range(n // m):
            own = b[j * m:(j + 1) * m, :]
            if j % 2 == 1:
                ref_q.append(jnp.broadcast_to(b[j * m - 1:j * m, :], (m, C_WIDTH)))
                ref_k.append(own)
            else:
                ref_q.append(own)
                ref_k.append(jnp.broadcast_to(b[(j + 1) * m - 1:(j + 1) * m, :], (m, C_WIDTH)))
        upper = (row // m) % 2 == 1
        qt = jnp.where(upper, q * jnp.exp(jnp.minimum(b - jnp.concatenate(ref_q, axis=0), 0.0)), 0.0)
        kt = jnp.where(upper, 0.0, k * jnp.exp(jnp.minimum(jnp.concatenate(ref_k, axis=0) - b, 0.0)))
        out += [qt.astype(BF16), kt.astype(BF16)]
    return out


def _hgrn_chunks(seqs, cst):
    n = CHUNK
    ns = len(seqs)
    ones_bd = cst["ones_bd"]

    def head_sums(x):
        return jnp.dot(x.astype(BF16), ones_bd, preferred_element_type=F32)

    tiles = (n // SUBLANES, SUBLANES, C_WIDTH)
    ys = []
    for q, b, k, *_ in seqs:
        q3, k3, b3 = (x.reshape(tiles) for x in (q, k, b))
        f3 = jnp.exp(jnp.minimum(b3 - pltpu.roll(b3, 1, 1), 0.0))
        ys.append((q * k).astype(BF16))
        decay = None
        for d in range(1, SUB):
            decay = f3 if d == 1 else decay * pltpu.roll(f3, d - 1, 1)
            ys.append((q3 * pltpu.roll(k3, d, 1) * decay).reshape(n, C_WIDTH).astype(BF16))
    att_d = head_sums(jnp.concatenate(ys, axis=0)).astype(BF16)
    lvl_scores = []
    for lvl in range(len(LEVELS)):
        k_stack = [jnp.concatenate([s[5][2 * lvl + 1]] * C_HEADS, axis=0) * cst["stack"] for s in seqs]
        lvl_scores.append([_dot_nt(s[5][2 * lvl], ks).astype(BF16) for s, ks in zip(seqs, k_stack)])
    q_hat = [s[0] * jnp.exp(s[1]) for s in seqs]
    b_last = [s[1][n - 1:n, :] for s in seqs]
    k_hat = [s[2] * jnp.exp(bl - s[1]) for s, bl in zip(seqs, b_last)]
    o_st = [_dot_nt(qh, s[6]) for qh, s in zip(q_hat, seqs)]
    upd = [_dot_tn(s[3], kh) for s, kh in zip(seqs, k_hat)]

    att = [functools.reduce(jnp.add, [att_d[(i * SUB + d) * n:(i * SUB + d + 1) * n, :] * cst["place"][d]
                                      for d in range(SUB)]) for i in range(ns)]
    for lvl, scores in enumerate(lvl_scores):
        att = [x + y * cst["same_block"][lvl] for x, y in zip(att, scores)]

    v_stack = [jnp.concatenate([s[3].astype(BF16)] * C_HEADS, axis=0) * cst["stack"] for s in seqs]
    o = [jnp.dot(x, vs, preferred_element_type=F32) + y for x, vs, y in zip(att, v_stack, o_st)]
    st = [s[6] * jnp.exp(bl) + u * cst["same_head"] for s, bl, u in zip(seqs, b_last, upd)]

    sq = [x * x for x in o]
    sq_hi = [x.astype(BF16) for x in sq]
    sq_lo = [(x - h.astype(F32)).astype(BF16) for x, h in zip(sq, sq_hi)]
    ms = head_sums(jnp.concatenate(sq_hi + sq_lo, axis=0))
    out = []
    for i, (x, s) in enumerate(zip(o, seqs)):
        m2 = ms[i * n:(i + 1) * n, :] + ms[(ns + i) * n:(ns + i + 1) * n, :]
        out.append((x * lax.rsqrt(m2 * (1.0 / C_DK) + EPS) * s[4], st[i]))
    return out


def _hgrn_constants():
    n = CHUNK
    hr = lax.broadcasted_iota(jnp.int32, (C_WIDTH, C_WIDTH), 0) // C_DK
    hc = lax.broadcasted_iota(jnp.int32, (C_WIDTH, C_WIDTH), 1) // C_DK
    same_head = jnp.where(hr == hc, 1.0, 0.0)
    srow = lax.broadcasted_iota(jnp.int32, (n, C_HEADS * n), 0)
    scol = lax.broadcasted_iota(jnp.int32, (n, C_HEADS * n), 1) % n
    return {
        "ones_bd": same_head.astype(BF16),
        "same_head": same_head,
        "stack": same_head.astype(BF16),
        "place": [jnp.where((scol == srow - d) & (srow % SUB >= d), 1.0, 0.0).astype(BF16) for d in range(SUB)],
        "same_block": [jnp.where(srow // (2 * m) == scol // (2 * m), 1.0, 0.0).astype(BF16) for m in LEVELS],
    }


def _hgrn_prompt_body(zc_ref, zl_ref, yc_ref, st_ref, st_scr, *, rows, n_seq):
    t = pl.program_id(1)

    @pl.when(t == 0)
    def _():
        st_scr[...] = jnp.zeros_like(st_scr)

    cst = _hgrn_constants()

    def chunk(c, carry):
        sl = pl.ds(pl.multiple_of(c * CHUNK, CHUNK), CHUNK)
        seqs = [(*(zc_ref[i, sl, j * C_WIDTH:(j + 1) * C_WIDTH] for j in range(5)),
                 [zl_ref[i, sl, j * C_WIDTH:(j + 1) * C_WIDTH] for j in range(2 * len(LEVELS))], st_scr[i])
                for i in range(n_seq)]
        for i, (o, st) in enumerate(_hgrn_chunks(seqs, cst)):
            st_scr[i] = st
            yc_ref[i, sl, :] = o
        return carry

    lax.fori_loop(0, rows // CHUNK, chunk, 0)

    @pl.when(t == pl.num_programs(1) - 1)
    def _():
        for i in range(n_seq):
            for h in range(C_HEADS):
                st_ref[i, h] = st_scr[i, h * C_DK:(h + 1) * C_DK, h * C_DK:(h + 1) * C_DK]


def _hgrn_prompt(zc, zl, rows=ROW_TILE // 2):
    bsz, seq, _ = zc.shape
    n_seq = max(n for n in (8, 4, 2, 1) if bsz % n == 0)
    assert CHUNK == C_DK
    yc, st = pl.pallas_call(
        functools.partial(_hgrn_prompt_body, rows=rows, n_seq=n_seq),
        grid=(bsz // n_seq, seq // rows),
        in_specs=[pl.BlockSpec((n_seq, rows, 5 * C_WIDTH), lambda b, t: (b, t, 0)),
                  pl.BlockSpec((n_seq, rows, 2 * len(LEVELS) * C_WIDTH), lambda b, t: (b, t, 0))],
        out_specs=[pl.BlockSpec((n_seq, rows, C_WIDTH), lambda b, t: (b, t, 0)),
                   pl.BlockSpec((n_seq, C_HEADS, C_DK, C_DK), lambda b, t: (b, 0, 0, 0))],
        out_shape=[jax.ShapeDtypeStruct((bsz, seq, C_WIDTH), F32),
                   jax.ShapeDtypeStruct((bsz, C_HEADS, C_DK, C_DK), F32)],
        scratch_shapes=[pltpu.VMEM((n_seq, C_WIDTH, C_WIDTH), F32)],
        compiler_params=_params(2),
        name="hgrn_prompt",
    )(zc, zl)
    return yc, jnp.swapaxes(st, -1, -2)


def _gated_merge(x, ya, yb, yc, zg, bg, wpa, wpb, wpc, wo):
    g = jax.nn.sigmoid(zg + bg)
    mix = (g[:, 0:D_MODEL] * _dot(ya, wpa) + g[:, D_MODEL:2 * D_MODEL] * _dot(yb, wpb)
           + g[:, 2 * D_MODEL:3 * D_MODEL] * _dot(yc, wpc))
    return x + _dot(mix, wo)


def _project_conv_and_gates(x, g1, w_ref):
    h = _rmsnorm(x, g1).astype(BF16)
    return (jnp.dot(h, w_ref[:, COL_B:COL_C], preferred_element_type=F32),
            jnp.dot(h, w_ref[:, COL_G:IN_COLS], preferred_element_type=F32))


def _merge_prompt_body(x_ref, ya_ref, yc_ref, g1_ref, w_ref, bg_ref, cw_ref, wpa_ref, wpb_ref, wpc_ref,
                       wo_ref, x1_ref, nc_ref, u_scr, *, rows):
    @pl.when(pl.program_id(1) == 0)
    def _():
        u_scr[0:8, :] = jnp.zeros((8, CONV_WIDTH), F32)

    sub = min(rows, ROW_TILE)
    for r0 in range(0, rows, sub):
        sl = slice(r0, r0 + sub)
        x = x_ref[0, sl, :]
        zb, zg = _project_conv_and_gates(x, g1_ref[...], w_ref)
        u = zb[:, 2 * CONV_WIDTH:3 * CONV_WIDTH] * zb[:, 0:CONV_WIDTH]
        u_scr[8 + r0:8 + r0 + sub, :] = u
        conv = (cw_ref[0:1, :] * u_scr[pl.ds(6 + r0, sub), :] + cw_ref[1:2, :] * u_scr[pl.ds(7 + r0, sub), :]
                + cw_ref[2:3, :] * u)
        yb = zb[:, CONV_WIDTH:2 * CONV_WIDTH] * conv
        x1_ref[0, sl, :] = _gated_merge(x, ya_ref[0, sl, :], yb, yc_ref[0, sl, :], zg, bg_ref[...],
                                        wpa_ref[...], wpb_ref[...], wpc_ref[...], wo_ref[...])
    nc_ref[0] = u_scr[8 + rows - (CONV_K - 1):8 + rows, :]
    u_scr[0:8, :] = u_scr[rows:rows + 8, :]


def _merge_prompt(x, ya, yc, norm1, w_in, b_gate, conv_w, wpa, wpb, wpc, wo, rows=2 * ROW_TILE):
    bsz, seq, _ = x.shape
    blk = lambda w: pl.BlockSpec((1, rows, w), lambda b, t: (b, t, 0))
    return pl.pallas_call(
        functools.partial(_merge_prompt_body, rows=rows),
        grid=(bsz, seq // rows),
        in_specs=[blk(D_MODEL), blk(GROUP_W), blk(C_WIDTH),
                  _const_spec((1, D_MODEL)), _const_spec(w_in.shape),
                  _const_spec((1, N_BRANCH * D_MODEL)), _const_spec((CONV_K, CONV_WIDTH)),
                  _const_spec(wpa.shape), _const_spec(wpb.shape), _const_spec(wpc.shape), _const_spec(wo.shape)],
        out_specs=[blk(D_MODEL), pl.BlockSpec((1, CONV_K - 1, CONV_WIDTH), lambda b, t: (b, 0, 0))],
        out_shape=[jax.ShapeDtypeStruct((bsz, seq, D_MODEL), F32),
                   jax.ShapeDtypeStruct((bsz, CONV_K - 1, CONV_WIDTH), F32)],
        scratch_shapes=[pltpu.VMEM((rows + 8, CONV_WIDTH), F32)],
        compiler_params=_params(2),
        name="merge_prompt",
    )(x, ya, yc, norm1.reshape(1, D_MODEL), w_in, b_gate.reshape(1, -1), conv_w, wpa, wpb, wpc, wo)


def _merge_sample_body(x_ref, ya_ref, yc_ref, zb_ref, zg_ref, p0_ref, p1_ref, bg_ref, cw_ref, wpa_ref,
                       wpb_ref, wpc_ref, wo_ref, x1_ref, u_ref, wpa_bf_ref, wpb_bf_ref, wpc_bf_ref, wo_bf_ref):
    bf = []
    for w_ref, bf_ref in ((wpa_ref, wpa_bf_ref), (wpb_ref, wpb_bf_ref), (wpc_ref, wpc_bf_ref),
                          (wo_ref, wo_bf_ref)):
        bf.append(w_ref[...].astype(BF16))
        bf_ref[...] = bf[-1]
    zb = zb_ref[...]
    u = zb[:, 2 * CONV_WIDTH:3 * CONV_WIDTH] * zb[:, 0:CONV_WIDTH]
    conv = cw_ref[0:1, :] * p0_ref[...] + cw_ref[1:2, :] * p1_ref[...] + cw_ref[2:3, :] * u
    yb = zb[:, CONV_WIDTH:2 * CONV_WIDTH] * conv
    u_ref[...] = u
    x1_ref[...] = _gated_merge(x_ref[...], ya_ref[...], yb, yc_ref[...], zg_ref[...], bg_ref[...], *bf)


def _merge_sample(x, ya, yc, zb, zg, prev0, prev1, b_gate, conv_w, layer, w_pa, w_pb, w_pc, w_o):
    m = x.shape[0]
    small = (x, ya, yc, zb, zg, prev0, prev1, b_gate.reshape(1, -1), conv_w)
    stacks = (w_pa, w_pb, w_pc, w_o)
    return pl.pallas_call(
        _merge_sample_body,
        grid=(1,),
        in_specs=[_const_spec(a.shape) for a in small] + [_layer_spec(w, layer) for w in stacks],
        out_specs=[_const_spec((m, D_MODEL)), _const_spec((m, CONV_WIDTH))]
        + [_const_spec(w.shape[1:]) for w in stacks],
        out_shape=[jax.ShapeDtypeStruct((m, D_MODEL), F32), jax.ShapeDtypeStruct((m, CONV_WIDTH), F32)]
        + [jax.ShapeDtypeStruct(w.shape[1:], BF16) for w in stacks],
        compiler_params=_params(1),
        name="merge_sample",
    )(*small, *stacks)


def _as_column(row):
    return jnp.concatenate([jnp.broadcast_to(row[:, c:c + LANES], (LANES, LANES)).T
                            for c in range(0, row.shape[1], LANES)], axis=0)


def _shift_window_buffers(k_ref, v_ref, cache_refs, new_refs):
    for g, (c_ref, n_ref, (win, _)) in enumerate(zip(cache_refs, new_refs, DSWA_GROUPS)):
        rolled = pltpu.roll(c_ref[...], win - 1, 1)
        new_col = jnp.concatenate([_as_column(k_ref[g]), _as_column(v_ref[g])], axis=0)
        last = lax.broadcasted_iota(jnp.int32, new_col.shape, 1) == LANES - 1
        if win > LANES:
            n_ref[:, 0:win - LANES] = rolled[:, 0:win - LANES]
        n_ref[:, win - LANES:win] = jnp.where(last, new_col, rolled[:, win - LANES:win])


def _mlp_body(x_ref, g2_ref, wup_ref, wdn_ref, gf_ref, k_ref, v_ref, c0_ref, c1_ref, c2_ref,
              out_ref, n0_ref, n1_ref, n2_ref, *, final):
    _shift_window_buffers(k_ref, v_ref, (c0_ref, c1_ref, c2_ref), (n0_ref, n1_ref, n2_ref))
    tm = x_ref.shape[0]
    sub = min(tm, ROW_TILE)
    for r0 in range(0, tm, sub):
        rows = slice(r0, r0 + sub)
        x = x_ref[rows, :]
        h = _rmsnorm(x, g2_ref[...]).astype(BF16)
        acc = x
        for c in range(D_FF // D_MODEL):
            cols = slice(c * D_MODEL, (c + 1) * D_MODEL)
            a = jnp.maximum(jnp.dot(h, wup_ref[:, cols], preferred_element_type=F32), 0.0)
            acc = acc + jnp.dot((a * a).astype(BF16), wdn_ref[cols, :], preferred_element_type=F32)
        out_ref[rows, :] = _rmsnorm(acc, gf_ref[...]) if final else acc


def _mlp(x2d, norm2, wup, wdn, norm_f, final, tm, k_new, v_new, caches, layer, new_caches):
    m = x2d.shape[0]
    depth, n = caches[0].shape[:2]
    steps = m // tm
    assert n <= steps
    row = pl.BlockSpec((tm, D_MODEL), lambda i: (i, 0))
    seq = lambda i: jnp.minimum(i, n - 1)
    grp = pl.BlockSpec((3, None, 1, GROUP_W), lambda i: (0, seq(i), 0, 0))
    cache_specs = [pl.BlockSpec((None, None, 2 * GROUP_W, win), lambda i: (layer, seq(i), 0, 0))
                   for win, _ in DSWA_GROUPS]
    in_specs = [row, _const_spec((1, D_MODEL)), _const_spec(wup.shape), _const_spec(wdn.shape),
                _const_spec((1, D_MODEL)), grp, grp] + cache_specs
    args = [x2d, norm2.reshape(1, D_MODEL), wup, wdn, norm_f.reshape(1, D_MODEL),
            k_new.reshape(3, n, 1, GROUP_W), v_new.reshape(3, n, 1, GROUP_W), *caches]
    aliases = {}
    if new_caches is not None:
        aliases = {len(args) + i: 1 + i for i in range(3)}
        in_specs = in_specs + [pl.BlockSpec(memory_space=pl.ANY)] * 3
        args = args + list(new_caches)
    n_in = len(cache_specs) + 7

    def body(*refs):
        _mlp_body(*refs[:n_in], *refs[len(args):], final=final)

    return pl.pallas_call(
        body,
        grid=(steps,),
        in_specs=in_specs,
        out_specs=[row] + cache_specs,
        out_shape=[jax.ShapeDtypeStruct((m, D_MODEL), F32)]
        + [jax.ShapeDtypeStruct((depth, n, 2 * GROUP_W, win), F32) for win, _ in DSWA_GROUPS],
        input_output_aliases=aliases,
        compiler_params=_params(1),
        name="mlp",
    )(*args)


def _mlp_sample_body(x_ref, g2_ref, wup_ref, wdn_ref, gf_ref, out_ref, wup_bf_ref, wdn_bf_ref, acc_scr, *, final):
    c = pl.program_id(0)
    wu = wup_ref[...].astype(BF16)
    wd = wdn_ref[...].astype(BF16)
    wup_bf_ref[...] = wu
    wdn_bf_ref[...] = wd
    x = x_ref[...]
    h = _rmsnorm(x, g2_ref[...]).astype(BF16)
    a = jnp.maximum(jnp.dot(h, wu, preferred_element_type=F32), 0.0)
    part = jnp.dot((a * a).astype(BF16), wd, preferred_element_type=F32)

    @pl.when(c == 0)
    def _():
        acc_scr[...] = x + part

    @pl.when(c > 0)
    def _():
        acc_scr[...] = acc_scr[...] + part

    @pl.when(c == pl.num_programs(0) - 1)
    def _():
        acc = acc_scr[...]
        out_ref[...] = _rmsnorm(acc, gf_ref[...]) if final else acc


def _mlp_sample(x2d, norm2, layer, w_up, w_down, norm_f, final):
    m = x2d.shape[0]
    n_chunks = D_FF // D_MODEL
    return pl.pallas_call(
        functools.partial(_mlp_sample_body, final=final),
        grid=(n_chunks,),
        in_specs=[_const_spec((m, D_MODEL)), _const_spec((1, D_MODEL)),
                  pl.BlockSpec((None, D_MODEL, D_MODEL), lambda c: (layer, 0, c)),
                  pl.BlockSpec((None, D_MODEL, D_MODEL), lambda c: (layer, c, 0)),
                  _const_spec((1, D_MODEL))],
        out_specs=[_const_spec((m, D_MODEL)), pl.BlockSpec((D_MODEL, D_MODEL), lambda c: (0, c)),
                   pl.BlockSpec((D_MODEL, D_MODEL), lambda c: (c, 0))],
        out_shape=[jax.ShapeDtypeStruct((m, D_MODEL), F32), jax.ShapeDtypeStruct((D_MODEL, D_FF), BF16),
                   jax.ShapeDtypeStruct((D_FF, D_MODEL), BF16)],
        scratch_shapes=[pltpu.VMEM((m, D_MODEL), F32)],
        compiler_params=_params(1),
        name="mlp_sample",
    )(x2d, norm2.reshape(1, D_MODEL), w_up, w_down, norm_f.reshape(1, D_MODEL))


def _sample_mix_body(q_ref, k_ref, v_ref, c0_ref, c1_ref, c2_ref, zrow_ref, s_ref, ya_ref, yc_ref, s_out_ref):
    for i in range(q_ref.shape[1]):
        _sample_mix_one(q_ref.at[:, i], k_ref.at[:, i], v_ref.at[:, i], c0_ref.at[i], c1_ref.at[i], c2_ref.at[i],
                        zrow_ref.at[i], s_ref.at[i], ya_ref.at[i], yc_ref.at[i], s_out_ref.at[i])


def _sample_mix_one(q_ref, k_ref, v_ref, c0_ref, c1_ref, c2_ref, zrow_ref, s_ref, ya_ref, yc_ref, s_out_ref):
    lane = lax.broadcasted_iota(jnp.int32, (8, GROUP_W), 1)
    rowi = lax.broadcasted_iota(jnp.int32, (8, GROUP_W), 0)
    head_of_row = (lane // HEAD_DIM) == rowi
    head0 = lax.broadcasted_iota(jnp.int32, (1, GROUP_W), 1) < HEAD_DIM

    outs, lses = [], []
    for g, (c_ref, (win, dil)) in enumerate(zip((c0_ref, c1_ref, c2_ref), DSWA_GROUPS)):
        q, k_new, v_new = q_ref[g], k_ref[g], v_ref[g]
        buf = c_ref[...]
        pos = lax.broadcasted_iota(jnp.int32, (8, win), 1)
        q2 = jnp.where(head_of_row, jnp.broadcast_to(q, (8, GROUP_W)), 0.0)
        scale = HEAD_DIM ** -0.5
        s = jnp.where(pos % dil == 0, _dot(q2, buf[0:GROUP_W, :]) * scale, NEG_BIG)
        s_self = jnp.sum(q2 * k_new, axis=-1, keepdims=True) * scale
        m = jnp.maximum(jnp.max(s, axis=-1, keepdims=True), s_self)
        p, p_self = jnp.exp(s - m), jnp.exp(s_self - m)
        l = jnp.sum(p, axis=-1, keepdims=True) + p_self
        o = (_dot_nt(p, buf[GROUP_W:2 * GROUP_W, :]) + p_self * v_new) / l
        lse = m + jnp.log(l)
        outs.append(jnp.where(head0, o[0:1], o[1:2]))
        lses.append(jnp.where(head0, jnp.broadcast_to(lse[0:1], (1, GROUP_W)),
                              jnp.broadcast_to(lse[1:2], (1, GROUP_W))))
    m = jnp.maximum(jnp.maximum(lses[0], lses[1]), lses[2])
    e = [jnp.exp(x - m) for x in lses]
    ya_ref[...] = (e[0] * outs[0] + e[1] * outs[1] + e[2] * outs[2]) / (e[0] + e[1] + e[2])

    zrow = zrow_ref[...]
    q_col, log_f, k_col = (_as_column(zrow[:, j * C_WIDTH:(j + 1) * C_WIDTH])[:, 0:C_DK] for j in range(3))
    v_row = zrow[:, 3 * C_WIDTH:4 * C_WIDTH]
    gate_row = zrow[:, 4 * C_WIDTH:5 * C_WIDTH]
    v_sel = jnp.concatenate([jnp.broadcast_to(v_row[:, h * C_DK:(h + 1) * C_DK], (C_DK, C_DK))
                             for h in range(C_HEADS)],
                            axis=0)
    s_new = jnp.exp(log_f) * s_ref[...] + k_col * v_sel
    s_out_ref[...] = s_new
    qs = q_col * s_new
    o4 = jnp.concatenate([jnp.sum(qs[h * C_DK:(h + 1) * C_DK, :], axis=0, keepdims=True) for h in range(C_HEADS)],
                         axis=0)
    gate4 = jnp.concatenate([gate_row[:, h * C_DK:(h + 1) * C_DK] for h in range(C_HEADS)], axis=0)
    o4 = o4 * lax.rsqrt(jnp.mean(o4 * o4, axis=-1, keepdims=True) + EPS) * gate4
    yc_ref[...] = o4


def _sample_mix(q, k, v, caches, zc, state, layer, new_state):
    n = q.shape[1]
    per = max(p for p in (4, 2, 1) if n % p == 0)
    row3 = lambda w: pl.BlockSpec((per, 1, w), lambda b: (b, 0, 0))
    grp = pl.BlockSpec((3, per, 1, GROUP_W), lambda b: (0, b, 0, 0))
    cache_specs = [pl.BlockSpec((None, per, 2 * GROUP_W, win), lambda b: (layer, b, 0, 0))
                   for win, _ in DSWA_GROUPS]
    state_spec = pl.BlockSpec((None, per, C_WIDTH, C_DK), lambda b: (layer, b, 0, 0))
    in_specs = [grp] * 3 + cache_specs + [row3(5 * C_WIDTH), state_spec]
    args = [q.reshape(3, n, 1, GROUP_W), k.reshape(3, n, 1, GROUP_W), v.reshape(3, n, 1, GROUP_W),
            *caches, zc.reshape(n, 1, 5 * C_WIDTH), state]
    n_in = len(args)
    aliases = {}
    if new_state is not None:
        aliases = {n_in: 2}
        in_specs = in_specs + [pl.BlockSpec(memory_space=pl.ANY)]
        args = args + [new_state]

    def body(*refs):
        _sample_mix_body(*refs[:n_in], *refs[len(args):])

    return pl.pallas_call(
        body,
        grid=(n // per,),
        in_specs=in_specs,
        out_specs=[pl.BlockSpec((per, 1, GROUP_W), lambda b: (b, 0, 0)),
                   pl.BlockSpec((per, C_HEADS, C_DK), lambda b: (b, 0, 0)), state_spec],
        out_shape=[jax.ShapeDtypeStruct((n, 1, GROUP_W), F32), jax.ShapeDtypeStruct((n, C_HEADS, C_DK), F32),
                   jax.ShapeDtypeStruct(state.shape, F32)],
        input_output_aliases=aliases,
        compiler_params=_params(1),
        name="sample_mix",
    )(*args)


def kernel(x_prompt, x_sample, cache_kv_w128, cache_kv_w512, cache_kv_w2048, state_conv, state_hgrn, w_in, b_gate,
           norm1, conv_w, hgrn_lb, hgrn_norm, w_pa, w_pb, w_pc, w_o, norm2, w_up, w_down, norm_f):
    bp, tp, d = x_prompt.shape
    bs, ts, _ = x_sample.shape
    depth = w_in.shape[0]
    assert d == D_MODEL and ts == 1 and depth == 2
    assert PAST_LEN >= max(win for win, _ in DSWA_GROUPS)

    sm = jax.nn.softmax(hgrn_lb.astype(F32), axis=0)
    lower = jnp.cumsum(sm, axis=0) - sm[0:1]

    cos_p, sin_p = _rope_tables(jnp.arange(tp, dtype=jnp.int32))
    cos_s, sin_s = _rope_tables(jnp.full((bs,), PAST_LEN, dtype=jnp.int32))

    caches = [jnp.transpose(c, (0, 1, 3, 4, 5, 2)).reshape(depth, bs, 2 * GROUP_W, win)
              for c, (win, _) in zip((cache_kv_w128, cache_kv_w512, cache_kv_w2048), DSWA_GROUPS)]
    state = state_hgrn.reshape(depth, bs, C_WIDTH, C_DK)

    xp = x_prompt.reshape(bp * tp, d)
    xs = x_sample.reshape(bs, d)
    kv_p, kv_s = None, None
    conv_p, hgrn_p, conv_s, hgrn_s = [], [], [], None
    for l in range(depth):
        final = l == depth - 1

        q, k_s, v_s, zc, zb, zg, w_in_bf = _sample_inproj(xs, norm1[l], w_in, l, cos_s, sin_s, lower[l],
                                                          hgrn_norm[l])
        ya, yc, hgrn_s = _sample_mix(q, k_s, v_s, caches, zc, state, l, hgrn_s)
        x1, u, wpa, wpb, wpc, wo = _merge_sample(xs, ya.reshape(bs, GROUP_W), yc.reshape(bs, C_WIDTH), zb, zg,
                                                 state_conv[l, :, 0], state_conv[l, :, 1], b_gate[l], conv_w[l],
                                                 l, w_pa, w_pb, w_pc, w_o)
        xs, wup, wdn = _mlp_sample(x1, norm2[l], l, w_up, w_down, norm_f, final)
        conv_s.append(jnp.stack([state_conv[l, :, 1], u], axis=1))

        q, k, v, zc, zl = _inproj(xp, norm1[l], w_in_bf, cos_p, sin_p, lower[l], hgrn_norm[l], tm=2 * ROW_TILE)
        ya, *kv_p = _attn_prompt(q.reshape(3, bp, tp, -1), k.reshape(3, bp, tp, -1), v.reshape(3, bp, tp, -1),
                                 l, depth, kv_p)
        yc, st = _hgrn_prompt(zc.reshape(bp, tp, -1), zl.reshape(bp, tp, -1))
        x1, nc = _merge_prompt(xp.reshape(bp, tp, d), ya, yc, norm1[l], w_in_bf,
                               b_gate[l], conv_w[l], wpa, wpb, wpc, wo)
        xp, *kv_s = _mlp(x1.reshape(bp * tp, d), norm2[l], wup, wdn, norm_f, final, ROW_TILE,
                         k_s, v_s, caches, l, kv_s)
        conv_p.append(nc)
        hgrn_p.append(st)

    def kv6(c, n):
        return jnp.transpose(c.reshape(depth, n, 2, 2, HEAD_DIM, c.shape[3]), (0, 1, 5, 2, 3, 4))

    return (xp.reshape(bp, tp, d), xs.reshape(bs, ts, d),
            kv6(kv_p[0], bp), kv6(kv_p[1], bp), kv6(kv_p[2], bp), jnp.stack(conv_p), jnp.stack(hgrn_p),
            kv6(kv_s[0], bs), kv6(kv_s[1], bs), kv6(kv_s[2], bs), jnp.stack(conv_s),
            hgrn_s.reshape(depth, bs, C_HEADS, C_DK, C_DK))
```

```python
import functools

import jax
import jax.numpy as jnp
from jax import lax
from jax.experimental import pallas as pl
from jax.experimental.pallas import tpu as pltpu

F32 = jnp.float32
BF16 = jnp.bfloat16

D_MODEL = 1024
HEAD_DIM = 64
DSWA_GROUPS = ((128, 1), (512, 4), (2048, 16))
BAND = 128
GROUP_W = 2 * HEAD_DIM
A_WIDTH = 3 * GROUP_W
CONV_WIDTH = 384
CONV_K = 3
C_HEADS = 4
C_DK = 64
C_WIDTH = C_HEADS * C_DK
N_BRANCH = 3
D_FF = 4 * D_MODEL
ROPE_THETA = 10000.0
EPS = 1e-6
NEG_BIG = -1e30
LB_FLOOR = 1e-30
PAST_LEN = 8192
IN_COLS = 3 * A_WIDTH + 3 * CONV_WIDTH + 4 * C_WIDTH + N_BRANCH * D_MODEL
COL_B = 3 * A_WIDTH
COL_C = COL_B + 3 * CONV_WIDTH
COL_G = COL_C + 4 * C_WIDTH
CHUNK = 64
SUBLANES = 8
SUB = SUBLANES
LEVELS = (32, 16, 8)
VMEM_LIMIT = 56 * 1024 * 1024
TILES_PER_TRIP = 8
LANES = 128
ROW_TILE = 512
SUB_ROWS = 256


def _params(n_grid):
    return pltpu.CompilerParams(dimension_semantics=("arbitrary",) * n_grid,
                                vmem_limit_bytes=VMEM_LIMIT)


def _const_spec(shape):
    nd = len(shape)
    return pl.BlockSpec(shape, lambda *_: (0,) * nd, pipeline_mode=pl.Buffered(1))


def _layer_spec(stacked, layer):
    zeros = (0,) * (stacked.ndim - 1)
    return pl.BlockSpec((None,) + stacked.shape[1:], lambda *_: (layer,) + zeros, pipeline_mode=pl.Buffered(1))


def _dot(a, b):
    return jnp.dot(a.astype(BF16), b.astype(BF16), preferred_element_type=F32)


def _dot_nt(a, b):
    return lax.dot_general(a.astype(BF16), b.astype(BF16), (((1,), (1,)), ((), ())),
                           preferred_element_type=F32)


def _dot_tn(a, b):
    return lax.dot_general(a.astype(BF16), b.astype(BF16), (((0,), (0,)), ((), ())),
                           preferred_element_type=F32)


def _rmsnorm(x, g):
    return x * lax.rsqrt(jnp.mean(x * x, axis=-1, keepdims=True) + EPS) * g


def _store_qkv_and_hgrn(za, zc, cos, sin, lb, gnorm, q_ref, k_ref, v_ref, zc_ref, rows, zl_ref=None):
    chunked = zl_ref is not None
    lane = lax.broadcasted_iota(jnp.int32, cos.shape, 1)
    first_half = (lane % HEAD_DIM) < HEAD_DIM // 2
    for out_ref, c0 in ((q_ref, 0), (k_ref, A_WIDTH)):
        for g in range(3):
            z = za[:, c0 + g * GROUP_W:c0 + (g + 1) * GROUP_W]
            partner = jnp.where(first_half, pltpu.roll(z, GROUP_W - HEAD_DIM // 2, 1),
                                pltpu.roll(z, HEAD_DIM // 2, 1))
            out_ref[g, rows, :] = z * cos + partner * sin
    for g in range(3):
        v_ref[g, rows, :] = za[:, 2 * A_WIDTH + g * GROUP_W:2 * A_WIDTH + (g + 1) * GROUP_W]
    parts = _hgrn_inputs(zc, lb, gnorm, chunked)
    for i, part in enumerate(parts):
        zc_ref[rows, i * C_WIDTH:(i + 1) * C_WIDTH] = part
    if chunked:
        q, b, k = parts[0:3]
        for i, part in enumerate(_hgrn_level_operands(q, b, k)):
            zl_ref[rows, i * C_WIDTH:(i + 1) * C_WIDTH] = part


def _inproj_body(x_ref, g_ref, w_ref, cos_ref, sin_ref, lb_ref, gn_ref, q_ref, k_ref, v_ref, zc_ref, zl_ref):
    tm = x_ref.shape[0]
    sub = min(tm, SUB_ROWS)
    for r0 in range(0, tm, sub):
        rows = slice(r0, r0 + sub)
        h = _rmsnorm(x_ref[rows, :], g_ref[...]).astype(BF16)
        za = jnp.dot(h, w_ref[:, 0:COL_B], preferred_element_type=F32)
        zc = jnp.dot(h, w_ref[:, COL_C:COL_G], preferred_element_type=F32)
        _store_qkv_and_hgrn(za, zc, cos_ref[rows, :], sin_ref[rows, :], lb_ref[...], gn_ref[...],
                            q_ref, k_ref, v_ref, zc_ref, rows, zl_ref)


def _inproj(x2d, norm, w_in, cos, sin, lb, gnorm, tm):
    m = x2d.shape[0]
    assert tm % CHUNK == 0
    n_pos = cos.shape[0] // tm
    row = lambda w: pl.BlockSpec((tm, w), lambda i: (i, 0))
    tab = pl.BlockSpec((tm, GROUP_W), lambda i: (i % n_pos, 0))
    grp = pl.BlockSpec((3, tm, GROUP_W), lambda i: (0, i, 0))
    return pl.pallas_call(
        _inproj_body,
        grid=(m // tm,),
        in_specs=[row(D_MODEL), _const_spec((1, D_MODEL)), _const_spec(w_in.shape),
                  tab, tab, _const_spec((1, C_WIDTH)), _const_spec((1, C_WIDTH))],
        out_specs=[grp] * 3 + [row(5 * C_WIDTH), row(2 * len(LEVELS) * C_WIDTH)],
        out_shape=[jax.ShapeDtypeStruct((3, m, GROUP_W), F32)] * 3
        + [jax.ShapeDtypeStruct((m, 5 * C_WIDTH), F32), jax.ShapeDtypeStruct((m, 2 * len(LEVELS) * C_WIDTH), BF16)],
        compiler_params=_params(1),
        name="inproj",
    )(x2d, norm.reshape(1, D_MODEL), w_in, cos, sin, lb.reshape(1, C_WIDTH), gnorm.reshape(1, C_WIDTH))


def _sample_inproj_body(x_ref, g_ref, w_ref, cos_ref, sin_ref, lb_ref, gn_ref,
                        q_ref, k_ref, v_ref, zc_ref, zb_ref, zg_ref, wbf_ref, z_scr, *, n_blocks):
    j = pl.program_id(0)
    w = w_ref[...].astype(BF16)
    wbf_ref[...] = w
    h = _rmsnorm(x_ref[...], g_ref[...]).astype(BF16)
    z_scr[j] = jnp.dot(h, w, preferred_element_type=F32)

    @pl.when(j == n_blocks - 1)
    def _():
        z = jnp.concatenate([z_scr[i] for i in range(n_blocks)], axis=1)
        _store_qkv_and_hgrn(z[:, 0:COL_B], z[:, COL_C:COL_G], cos_ref[...], sin_ref[...], lb_ref[...],
                            gn_ref[...], q_ref, k_ref, v_ref, zc_ref, slice(None))
        zb_ref[...] = z[:, COL_B:COL_C]
        zg_ref[...] = z[:, COL_G:IN_COLS]


def _sample_inproj(x2d, norm, w_in_f32, layer, cos, sin, lb, gnorm, n_blocks=5):
    m = x2d.shape[0]
    bw = IN_COLS // n_blocks
    assert bw * n_blocks == IN_COLS and bw % 128 == 0
    widths = (5 * C_WIDTH, 3 * CONV_WIDTH, N_BRANCH * D_MODEL)
    return pl.pallas_call(
        functools.partial(_sample_inproj_body, n_blocks=n_blocks),
        grid=(n_blocks,),
        in_specs=[_const_spec((m, D_MODEL)), _const_spec((1, D_MODEL)),
                  pl.BlockSpec((None, D_MODEL, bw), lambda j: (layer, 0, j)),
                  _const_spec((m, GROUP_W)), _const_spec((m, GROUP_W)),
                  _const_spec((1, C_WIDTH)), _const_spec((1, C_WIDTH))],
        out_specs=[_const_spec((3, m, GROUP_W))] * 3 + [_const_spec((m, w)) for w in widths]
        + [pl.BlockSpec((D_MODEL, bw), lambda j: (0, j))],
        out_shape=[jax.ShapeDtypeStruct((3, m, GROUP_W), F32)] * 3
        + [jax.ShapeDtypeStruct((m, w), F32) for w in widths]
        + [jax.ShapeDtypeStruct((D_MODEL, IN_COLS), BF16)],
        scratch_shapes=[pltpu.VMEM((n_blocks, m, bw), F32)],
        compiler_params=_params(1),
        name="sample_inproj",
    )(x2d, norm.reshape(1, D_MODEL), w_in_f32, cos, sin, lb.reshape(1, C_WIDTH), gnorm.reshape(1, C_WIDTH))


def _rope_tables(pos):
    half = HEAD_DIM // 2
    inv = ROPE_THETA ** (-jnp.arange(half, dtype=F32) / half)
    ang = pos.astype(F32)[:, None] * inv[None, :]
    cos, sin = jnp.cos(ang), jnp.sin(ang)
    return jnp.tile(cos, (1, 4)), jnp.tile(jnp.concatenate([-sin, sin], axis=1), (1, 2))


def _attend(tiles):
    head0 = lax.broadcasted_iota(jnp.int32, (BAND, GROUP_W), 1) < HEAD_DIM
    scale = HEAD_DIM ** -0.5
    q2 = [jnp.concatenate([jnp.where(head0, q * scale, 0.0), jnp.where(head0, 0.0, q * scale)],
                          axis=0).astype(BF16) for q, _ in tiles]
    keys = [jnp.concatenate([k for k, _, _ in blocks], axis=0) for _, blocks in tiles]
    vals = [jnp.concatenate([v1 for _, v1, _ in blocks], axis=0) for _, blocks in tiles]
    valid = [jnp.concatenate([ok for _, _, ok in blocks], axis=1) for _, blocks in tiles]
    s = [jnp.where(ok, _dot_nt(qq, kk), NEG_BIG) for qq, kk, ok in zip(q2, keys, valid)]
    m = [jnp.max(x, axis=-1, keepdims=True) for x in s]
    p = [jnp.exp(x - mm).astype(BF16) for x, mm in zip(s, m)]
    ext = [jnp.dot(pp, vv, preferred_element_type=F32) for pp, vv in zip(p, vals)]
    results = []
    for e, mm in zip(ext, m):
        den = e[:, GROUP_W:]
        o2 = e[:, 0:GROUP_W] / den
        lse2 = mm + jnp.log(den)
        results.append((jnp.where(head0, o2[0:BAND], o2[BAND:]), jnp.where(head0, lse2[0:BAND], lse2[BAND:])))
    return results


def _attn_prompt_body(q_ref, k_ref, v_ref, ya_ref, c0_ref, c1_ref, c2_ref, o_scr, lse_scr, *, seq):
    qi = lax.broadcasted_iota(jnp.int32, (2 * BAND, BAND), 0) % BAND
    kj = lax.broadcasted_iota(jnp.int32, (2 * BAND, BAND), 1)
    cur_valid = kj <= qi
    prev_valid = kj >= qi
    ones = jnp.ones((BAND, GROUP_W), BF16)

    def rows(start, dil):
        return pl.ds(start, BAND, stride=dil) if dil > 1 else pl.ds(start, BAND)

    def key_block(g, r):
        return k_ref[g, r, :].astype(BF16), jnp.concatenate([v_ref[g, r, :].astype(BF16), ones], axis=1)

    def run(g, dil, chains):
        tiles, slices = [], []
        for starts, first_prev in chains:
            chain = [rows(s, dil) for s in starts]
            keys = [key_block(g, r) for r in chain]
            for j, r in enumerate(chain):
                blocks = [(*keys[j], cur_valid)]
                if j > 0:
                    blocks.append((*keys[j - 1], prev_valid))
                elif first_prev is not None:
                    blocks.append((*key_block(g, rows(first_prev[0], dil)), prev_valid & first_prev[1]))
                tiles.append((q_ref[g, r, :], blocks))
            slices += chain
        for r, (o, lse) in zip(slices, _attend(tiles)):
            o_scr[g, r, :] = o
            lse_scr[g, r, :] = lse

    for g, (_, dil) in enumerate(DSWA_GROUPS):
        n_blocks = seq // dil // BAND
        if dil == 1:
            def dense(i, carry):
                s0 = pl.multiple_of(TILES_PER_TRIP * i * BAND, TILES_PER_TRIP * BAND)
                prev = pl.multiple_of(jnp.maximum(TILES_PER_TRIP * i - 1, 0) * BAND, BAND)
                run(g, dil, [([s0 + u * BAND for u in range(TILES_PER_TRIP)], (prev, i > 0))])
                for gw, (c_ref, (win, _)) in enumerate(zip((c0_ref, c1_ref, c2_ref), DSWA_GROUPS)):
                    if win == seq:
                        for j in range(0, TILES_PER_TRIP * BAND, ROW_TILE):
                            r = pl.ds(s0 + j, ROW_TILE)
                            c_ref[0:GROUP_W, r] = k_ref[gw, r, :].T
                            c_ref[GROUP_W:2 * GROUP_W, r] = v_ref[gw, r, :].T
                return carry

            lax.fori_loop(0, n_blocks // TILES_PER_TRIP, dense, 0)
        else:
            per = max(1, TILES_PER_TRIP // n_blocks)
            assert dil % per == 0

            def residues(j, carry, g=g, dil=dil, n_blocks=n_blocks, per=per):
                run(g, dil, [([per * j + u + i * BAND * dil for i in range(n_blocks)], None) for u in range(per)])
                return carry

            lax.fori_loop(0, dil // per, residues, 0)

    step = 256

    def merge(i, carry):
        sl = pl.ds(pl.multiple_of(i * step, step), step)
        lse = [lse_scr[g, sl, :] for g in range(3)]
        m = jnp.maximum(jnp.maximum(lse[0], lse[1]), lse[2])
        e = [jnp.exp(x - m) for x in lse]
        acc = e[0] * o_scr[0, sl, :] + e[1] * o_scr[1, sl, :] + e[2] * o_scr[2, sl, :]
        ya_ref[0, sl, :] = acc / (e[0] + e[1] + e[2])
        return carry

    lax.fori_loop(0, seq // step, merge, 0)

    for g, (c_ref, (win, _)) in enumerate(zip((c0_ref, c1_ref, c2_ref), DSWA_GROUPS)):
        if win == seq:
            continue
        for j in range(0, win, ROW_TILE):
            w = min(ROW_TILE, win - j)
            c_ref[0:GROUP_W, j:j + w] = k_ref[g, seq - win + j:seq - win + j + w, :].T
            c_ref[GROUP_W:2 * GROUP_W, j:j + w] = v_ref[g, seq - win + j:seq - win + j + w, :].T


def _attn_prompt(q, k, v, layer, depth, caches):
    _, bsz, seq, _ = q.shape
    assert seq % (BAND * TILES_PER_TRIP) == 0
    assert all(seq >= win and seq % (dil * BAND) == 0 for win, dil in DSWA_GROUPS)
    qkv_spec = pl.BlockSpec((3, None, seq, GROUP_W), lambda b: (0, b, 0, 0))
    cache_specs = [pl.BlockSpec((None, None, 2 * GROUP_W, win), lambda b: (layer, b, 0, 0))
                   for win, _ in DSWA_GROUPS]
    cache_shapes = [jax.ShapeDtypeStruct((depth, bsz, 2 * GROUP_W, win), F32) for win, _ in DSWA_GROUPS]
    in_specs = [qkv_spec] * 3
    args = [q, k, v]
    aliases = {}
    if caches is not None:
        in_specs = in_specs + [pl.BlockSpec(memory_space=pl.ANY)] * 3
        args = args + list(caches)
        aliases = {3: 1, 4: 2, 5: 3}

    def body(*refs):
        ins, rest = refs[:3], refs[3 + (3 if caches is not None else 0):]
        _attn_prompt_body(*ins, *rest, seq=seq)

    return pl.pallas_call(
        body,
        grid=(bsz,),
        in_specs=in_specs,
        out_specs=[pl.BlockSpec((1, seq, GROUP_W), lambda b: (b, 0, 0))] + cache_specs,
        out_shape=[jax.ShapeDtypeStruct((bsz, seq, GROUP_W), F32)] + cache_shapes,
        scratch_shapes=[pltpu.VMEM((3, seq, GROUP_W), F32), pltpu.VMEM((3, seq, GROUP_W), F32)],
        input_output_aliases=aliases,
        compiler_params=_params(1),
        name="attn_prompt",
    )(*args)


def _log_forget_and_kgate(zf, lb):
    e = jnp.exp(-jnp.abs(zf))
    log_sig = -(jnp.maximum(-zf, 0.0) + jnp.log(1.0 + e))
    a = jnp.log(jnp.maximum(lb, LB_FLOOR))
    c = jnp.log1p(-lb) + log_sig
    log_f = jnp.maximum(a, c) + jnp.log(1.0 + jnp.exp(-jnp.abs(a - c)))
    kgate = (1.0 - lb) * (jnp.where(zf >= 0.0, e, 1.0) / (1.0 + e))
    return log_f, kgate


def _silu(x):
    return x * jax.nn.sigmoid(x)


def _split3(x):
    hi = x.astype(BF16)
    r1 = x - hi.astype(F32)
    mid = r1.astype(BF16)
    lo = (r1 - mid.astype(F32)).astype(BF16)
    return hi, mid, lo


def _hgrn_inputs(zc, lb, gnorm, chunked):
    q, zf, v, og = (zc[:, i * C_WIDTH:(i + 1) * C_WIDTH] for i in range(4))
    log_f, k = _log_forget_and_kgate(zf, lb)
    if chunked:
        n = zc.shape[0]
        r = lax.broadcasted_iota(jnp.int32, (n, n), 0)
        c = lax.broadcasted_iota(jnp.int32, (n, n), 1)
        tri = jnp.where((c <= r) & ((r // CHUNK) == (c // CHUNK)), 1.0, 0.0).astype(BF16)
        b = functools.reduce(jnp.add, [jnp.dot(tri, p, preferred_element_type=F32) for p in _split3(log_f)])
    else:
        b = log_f
    return q, b, k, v, gnorm * _silu(og)


def _hgrn_level_operands(q, b, k):
    n = q.shape[0]
    row = lax.broadcasted_iota(jnp.int32, (n, C_WIDTH), 0)
    out = []
    for m in LEVELS:
        ref_q, ref_k = [], []
        for j in range(n // m):
            own = b[j * m:(j + 1) * m, :]
            if j % 2 == 1:
                ref_q.append(jnp.broadcast_to(b[j * m - 1:j * m, :], (m, C_WIDTH)))
                ref_k.append(own)
            else:
                ref_q.append(own)
                ref_k.append(jnp.broadcast_to(b[(j + 1) * m - 1:(j + 1) * m, :], (m, C_WIDTH)))
        upper = (row // m) % 2 == 1
        qt = jnp.where(upper, q * jnp.exp(jnp.minimum(b - jnp.concatenate(ref_q, axis=0), 0.0)), 0.0)
        kt = jnp.where(upper, 0.0, k * jnp.exp(jnp.minimum(jnp.concatenate(ref_k, axis=0) - b, 0.0)))
        out += [qt.astype(BF16), kt.astype(BF16)]
    return out


def _hgrn_chunks(seqs, cst):
    n = CHUNK
    ns = len(seqs)
    ones_bd = cst["ones_bd"]

    def head_sums(x):
        return jnp.dot(x.astype(BF16), ones_bd, preferred_element_type=F32)

    tiles = (n // SUBLANES, SUBLANES, C_WIDTH)
    ys = []
    for q, b, k, *_ in seqs:
        q3, k3, b3 = (x.reshape(tiles) for x in (q, k, b))
        f3 = jnp.exp(jnp.minimum(b3 - pltpu.roll(b3, 1, 1), 0.0))
        ys.append((q * k).astype(BF16))
        decay = None
        for d in range(1, SUB):
            decay = f3 if d == 1 else decay * pltpu.roll(f3, d - 1, 1)
            ys.append((q3 * pltpu.roll(k3, d, 1) * decay).reshape(n, C_WIDTH).astype(BF16))
    att_d = head_sums(jnp.concatenate(ys, axis=0)).astype(BF16)
    lvl_scores = []
    for lvl in range(len(LEVELS)):
        k_stack = [jnp.concatenate([s[5][2 * lvl + 1]] * C_HEADS, axis=0) * cst["stack"] for s in seqs]
        lvl_scores.append([_dot_nt(s[5][2 * lvl], ks).astype(BF16) for s, ks in zip(seqs, k_stack)])
    q_hat = [s[0] * jnp.exp(s[1]) for s in seqs]
    b_last = [s[1][n - 1:n, :] for s in seqs]
    k_hat = [s[2] * jnp.exp(bl - s[1]) for s, bl in zip(seqs, b_last)]
    o_st = [_dot_nt(qh, s[6]) for qh, s in zip(q_hat, seqs)]
    upd = [_dot_tn(s[3], kh) for s, kh in zip(seqs, k_hat)]

    att = [functools.reduce(jnp.add, [att_d[(i * SUB + d) * n:(i * SUB + d + 1) * n, :] * cst["place"][d]
                                      for d in range(SUB)]) for i in range(ns)]
    for lvl, scores in enumerate(lvl_scores):
        att = [x + y * cst["same_block"][lvl] for x, y in zip(att, scores)]

    v_stack = [jnp.concatenate([s[3].astype(BF16)] * C_HEADS, axis=0) * cst["stack"] for s in seqs]
    o = [jnp.dot(x, vs, preferred_element_type=F32) + y for x, vs, y in zip(att, v_stack, o_st)]
    st = [s[6] * jnp.exp(bl) + u * cst["same_head"] for s, bl, u in zip(seqs, b_last, upd)]

    sq = [x * x for x in o]
    sq_hi = [x.astype(BF16) for x in sq]
    sq_lo = [(x - h.astype(F32)).astype(BF16) for x, h in zip(sq, sq_hi)]
    ms = head_sums(jnp.concatenate(sq_hi + sq_lo, axis=0))
    out = []
    for i, (x, s) in enumerate(zip(o, seqs)):
        m2 = ms[i * n:(i + 1) * n, :] + ms[(ns + i) * n:(ns + i + 1) * n, :]
        out.append((x * lax.rsqrt(m2 * (1.0 / C_DK) + EPS) * s[4], st[i]))
    return out


def _hgrn_constants():
    n = CHUNK
    hr = lax.broadcasted_iota(jnp.int32, (C_WIDTH, C_WIDTH), 0) // C_DK
    hc = lax.broadcasted_iota(jnp.int32, (C_WIDTH, C_WIDTH), 1) // C_DK
    same_head = jnp.where(hr == hc, 1.0, 0.0)
    srow = lax.broadcasted_iota(jnp.int32, (n, C_HEADS * n), 0)
    scol = lax.broadcasted_iota(jnp.int32, (n, C_HEADS * n), 1) % n
    return {
        "ones_bd": same_head.astype(BF16),
        "same_head": same_head,
        "stack": same_head.astype(BF16),
        "place": [jnp.where((scol == srow - d) & (srow % SUB >= d), 1.0, 0.0).astype(BF16) for d in range(SUB)],
        "same_block": [jnp.where(srow // (2 * m) == scol // (2 * m), 1.0, 0.0).astype(BF16) for m in LEVELS],
    }


def _hgrn_prompt_body(zc_ref, zl_ref, yc_ref, st_ref, st_scr, *, rows, n_seq):
    t = pl.program_id(1)

    @pl.when(t == 0)
    def _():
        st_scr[...] = jnp.zeros_like(st_scr)

    cst = _hgrn_constants()

    def chunk(c, carry):
        sl = pl.ds(pl.multiple_of(c * CHUNK, CHUNK), CHUNK)
        seqs = [(*(zc_ref[i, sl, j * C_WIDTH:(j + 1) * C_WIDTH] for j in range(5)),
                 [zl_ref[i, sl, j * C_WIDTH:(j + 1) * C_WIDTH] for j in range(2 * len(LEVELS))], st_scr[i])
                for i in range(n_seq)]
        for i, (o, st) in enumerate(_hgrn_chunks(seqs, cst)):
            st_scr[i] = st
            yc_ref[i, sl, :] = o
        return carry

    lax.fori_loop(0, rows // CHUNK, chunk, 0)

    @pl.when(t == pl.num_programs(1) - 1)
    def _():
        for i in range(n_seq):
            for h in range(C_HEADS):
                st_ref[i, h] = st_scr[i, h * C_DK:(h + 1) * C_DK, h * C_DK:(h + 1) * C_DK]


def _hgrn_prompt(zc, zl, rows=ROW_TILE // 2):
    bsz, seq, _ = zc.shape
    n_seq = max(n for n in (8, 4, 2, 1) if bsz % n == 0)
    assert CHUNK == C_DK
    yc, st = pl.pallas_call(
        functools.partial(_hgrn_prompt_body, rows=rows, n_seq=n_seq),
        grid=(bsz // n_seq, seq // rows),
        in_specs=[pl.BlockSpec((n_seq, rows, 5 * C_WIDTH), lambda b, t: (b, t, 0)),
                  pl.BlockSpec((n_seq, rows, 2 * len(LEVELS) * C_WIDTH), lambda b, t: (b, t, 0))],
        out_specs=[pl.BlockSpec((n_seq, rows, C_WIDTH), lambda b, t: (b, t, 0)),
                   pl.BlockSpec((n_seq, C_HEADS, C_DK, C_DK), lambda b, t: (b, 0, 0, 0))],
        out_shape=[jax.ShapeDtypeStruct((bsz, seq, C_WIDTH), F32),
                   jax.ShapeDtypeStruct((bsz, C_HEADS, C_DK, C_DK), F32)],
        scratch_shapes=[pltpu.VMEM((n_seq, C_WIDTH, C_WIDTH), F32)],
        compiler_params=_params(2),
        name="hgrn_prompt",
    )(zc, zl)
    return yc, jnp.swapaxes(st, -1, -2)


def _gated_merge(x, ya, yb, yc, zg, bg, wpa, wpb, wpc, wo):
    g = jax.nn.sigmoid(zg + bg)
    mix = (g[:, 0:D_MODEL] * _dot(ya, wpa) + g[:, D_MODEL:2 * D_MODEL] * _dot(yb, wpb)
           + g[:, 2 * D_MODEL:3 * D_MODEL] * _dot(yc, wpc))
    return x + _dot(mix, wo)


def _project_conv_and_gates(x, g1, w_ref):
    h = _rmsnorm(x, g1).astype(BF16)
    return (jnp.dot(h, w_ref[:, COL_B:COL_C], preferred_element_type=F32),
            jnp.dot(h, w_ref[:, COL_G:IN_COLS], preferred_element_type=F32))


def _merge_prompt_body(x_ref, ya_ref, yc_ref, g1_ref, w_ref, bg_ref, cw_ref, wpa_ref, wpb_ref, wpc_ref,
                       wo_ref, x1_ref, nc_ref, u_scr, *, rows):
    @pl.when(pl.program_id(1) == 0)
    def _():
        u_scr[0:8, :] = jnp.zeros((8, CONV_WIDTH), F32)

    sub = min(rows, ROW_TILE)
    for r0 in range(0, rows, sub):
        sl = slice(r0, r0 + sub)
        x = x_ref[0, sl, :]
        zb, zg = _project_conv_and_gates(x, g1_ref[...], w_ref)
        u = zb[:, 2 * CONV_WIDTH:3 * CONV_WIDTH] * zb[:, 0:CONV_WIDTH]
        u_scr[8 + r0:8 + r0 + sub, :] = u
        conv = (cw_ref[0:1, :] * u_scr[pl.ds(6 + r0, sub), :] + cw_ref[1:2, :] * u_scr[pl.ds(7 + r0, sub), :]
                + cw_ref[2:3, :] * u)
        yb = zb[:, CONV_WIDTH:2 * CONV_WIDTH] * conv
        x1_ref[0, sl, :] = _gated_merge(x, ya_ref[0, sl, :], yb, yc_ref[0, sl, :], zg, bg_ref[...],
                                        wpa_ref[...], wpb_ref[...], wpc_ref[...], wo_ref[...])
    nc_ref[0] = u_scr[8 + rows - (CONV_K - 1):8 + rows, :]
    u_scr[0:8, :] = u_scr[rows:rows + 8, :]


def _merge_prompt(x, ya, yc, norm1, w_in, b_gate, conv_w, wpa, wpb, wpc, wo, rows=2 * ROW_TILE):
    bsz, seq, _ = x.shape
    blk = lambda w: pl.BlockSpec((1, rows, w), lambda b, t: (b, t, 0))
    return pl.pallas_call(
        functools.partial(_merge_prompt_body, rows=rows),
        grid=(bsz, seq // rows),
        in_specs=[blk(D_MODEL), blk(GROUP_W), blk(C_WIDTH),
                  _const_spec((1, D_MODEL)), _const_spec(w_in.shape),
                  _const_spec((1, N_BRANCH * D_MODEL)), _const_spec((CONV_K, CONV_WIDTH)),
                  _const_spec(wpa.shape), _const_spec(wpb.shape), _const_spec(wpc.shape), _const_spec(wo.shape)],
        out_specs=[blk(D_MODEL), pl.BlockSpec((1, CONV_K - 1, CONV_WIDTH), lambda b, t: (b, 0, 0))],
        out_shape=[jax.ShapeDtypeStruct((bsz, seq, D_MODEL), F32),
                   jax.ShapeDtypeStruct((bsz, CONV_K - 1, CONV_WIDTH), F32)],
        scratch_shapes=[pltpu.VMEM((rows + 8, CONV_WIDTH), F32)],
        compiler_params=_params(2),
        name="merge_prompt",
    )(x, ya, yc, norm1.reshape(1, D_MODEL), w_in, b_gate.reshape(1, -1), conv_w, wpa, wpb, wpc, wo)


def _merge_sample_body(x_ref, ya_ref, yc_ref, zb_ref, zg_ref, p0_ref, p1_ref, bg_ref, cw_ref, wpa_ref,
                       wpb_ref, wpc_ref, wo_ref, x1_ref, u_ref, wpa_bf_ref, wpb_bf_ref, wpc_bf_ref, wo_bf_ref):
    bf = []
    for w_ref, bf_ref in ((wpa_ref, wpa_bf_ref), (wpb_ref, wpb_bf_ref), (wpc_ref, wpc_bf_ref),
                          (wo_ref, wo_bf_ref)):
        bf.append(w_ref[...].astype(BF16))
        bf_ref[...] = bf[-1]
    zb = zb_ref[...]
    u = zb[:, 2 * CONV_WIDTH:3 * CONV_WIDTH] * zb[:, 0:CONV_WIDTH]
    conv = cw_ref[0:1, :] * p0_ref[...] + cw_ref[1:2, :] * p1_ref[...] + cw_ref[2:3, :] * u
    yb = zb[:, CONV_WIDTH:2 * CONV_WIDTH] * conv
    u_ref[...] = u
    x1_ref[...] = _gated_merge(x_ref[...], ya_ref[...], yb, yc_ref[...], zg_ref[...], bg_ref[...], *bf)


def _merge_sample(x, ya, yc, zb, zg, prev0, prev1, b_gate, conv_w, layer, w_pa, w_pb, w_pc, w_o):
    m = x.shape[0]
    small = (x, ya, yc, zb, zg, prev0, prev1, b_gate.reshape(1, -1), conv_w)
    stacks = (w_pa, w_pb, w_pc, w_o)
    return pl.pallas_call(
        _merge_sample_body,
        grid=(1,),
        in_specs=[_const_spec(a.shape) for a in small] + [_layer_spec(w, layer) for w in stacks],
        out_specs=[_const_spec((m, D_MODEL)), _const_spec((m, CONV_WIDTH))]
        + [_const_spec(w.shape[1:]) for w in stacks],
        out_shape=[jax.ShapeDtypeStruct((m, D_MODEL), F32), jax.ShapeDtypeStruct((m, CONV_WIDTH), F32)]
        + [jax.ShapeDtypeStruct(w.shape[1:], BF16) for w in stacks],
        compiler_params=_params(1),
        name="merge_sample",
    )(*small, *stacks)


def _as_column(row):
    return jnp.concatenate([jnp.broadcast_to(row[:, c:c + LANES], (LANES, LANES)).T
                            for c in range(0, row.shape[1], LANES)], axis=0)


def _shift_window_buffers(k_ref, v_ref, cache_refs, new_refs):
    for g, (c_ref, n_ref, (win, _)) in enumerate(zip(cache_refs, new_refs, DSWA_GROUPS)):
        rolled = pltpu.roll(c_ref[...], win - 1, 1)
        new_col = jnp.concatenate([_as_column(k_ref[g]), _as_column(v_ref[g])], axis=0)
        last = lax.broadcasted_iota(jnp.int32, new_col.shape, 1) == LANES - 1
        if win > LANES:
            n_ref[:, 0:win - LANES] = rolled[:, 0:win - LANES]
        n_ref[:, win - LANES:win] = jnp.where(last, new_col, rolled[:, win - LANES:win])


def _mlp_body(x_ref, g2_ref, wup_ref, wdn_ref, gf_ref, k_ref, v_ref, c0_ref, c1_ref, c2_ref,
              out_ref, n0_ref, n1_ref, n2_ref, *, final):
    _shift_window_buffers(k_ref, v_ref, (c0_ref, c1_ref, c2_ref), (n0_ref, n1_ref, n2_ref))
    tm = x_ref.shape[0]
    sub = min(tm, ROW_TILE)
    for r0 in range(0, tm, sub):
        rows = slice(r0, r0 + sub)
        x = x_ref[rows, :]
        h = _rmsnorm(x, g2_ref[...]).astype(BF16)
        acc = x
        for c in range(D_FF // D_MODEL):
            cols = slice(c * D_MODEL, (c + 1) * D_MODEL)
            a = jnp.maximum(jnp.dot(h, wup_ref[:, cols], preferred_element_type=F32), 0.0)
            acc = acc + jnp.dot((a * a).astype(BF16), wdn_ref[cols, :], preferred_element_type=F32)
        out_ref[rows, :] = _rmsnorm(acc, gf_ref[...]) if final else acc


def _mlp(x2d, norm2, wup, wdn, norm_f, final, tm, k_new, v_new, caches, layer, new_caches):
    m = x2d.shape[0]
    depth, n = caches[0].shape[:2]
    steps = m // tm
    assert n <= steps
    row = pl.BlockSpec((tm, D_MODEL), lambda i: (i, 0))
    seq = lambda i: jnp.minimum(i, n - 1)
    grp = pl.BlockSpec((3, None, 1, GROUP_W), lambda i: (0, seq(i), 0, 0))
    cache_specs = [pl.BlockSpec((None, None, 2 * GROUP_W, win), lambda i: (layer, seq(i), 0, 0))
                   for win, _ in DSWA_GROUPS]
    in_specs = [row, _const_spec((1, D_MODEL)), _const_spec(wup.shape), _const_spec(wdn.shape),
                _const_spec((1, D_MODEL)), grp, grp] + cache_specs
    args = [x2d, norm2.reshape(1, D_MODEL), wup, wdn, norm_f.reshape(1, D_MODEL),
            k_new.reshape(3, n, 1, GROUP_W), v_new.reshape(3, n, 1, GROUP_W), *caches]
    aliases = {}
    if new_caches is not None:
        aliases = {len(args) + i: 1 + i for i in range(3)}
        in_specs = in_specs + [pl.BlockSpec(memory_space=pl.ANY)] * 3
        args = args + list(new_caches)
    n_in = len(cache_specs) + 7

    def body(*refs):
        _mlp_body(*refs[:n_in], *refs[len(args):], final=final)

    return pl.pallas_call(
        body,
        grid=(steps,),
        in_specs=in_specs,
        out_specs=[row] + cache_specs,
        out_shape=[jax.ShapeDtypeStruct((m, D_MODEL), F32)]
        + [jax.ShapeDtypeStruct((depth, n, 2 * GROUP_W, win), F32) for win, _ in DSWA_GROUPS],
        input_output_aliases=aliases,
        compiler_params=_params(1),
        name="mlp",
    )(*args)


def _mlp_sample_body(x_ref, g2_ref, wup_ref, wdn_ref, gf_ref, out_ref, wup_bf_ref, wdn_bf_ref, acc_scr, *, final):
    c = pl.program_id(0)
    wu = wup_ref[...].astype(BF16)
    wd = wdn_ref[...].astype(BF16)
    wup_bf_ref[...] = wu
    wdn_bf_ref[...] = wd
    x = x_ref[...]
    h = _rmsnorm(x, g2_ref[...]).astype(BF16)
    a = jnp.maximum(jnp.dot(h, wu, preferred_element_type=F32), 0.0)
    part = jnp.dot((a * a).astype(BF16), wd, preferred_element_type=F32)

    @pl.when(c == 0)
    def _():
        acc_scr[...] = x + part

    @pl.when(c > 0)
    def _():
        acc_scr[...] = acc_scr[...] + part

    @pl.when(c == pl.num_programs(0) - 1)
    def _():
        acc = acc_scr[...]
        out_ref[...] = _rmsnorm(acc, gf_ref[...]) if final else acc


def _mlp_sample(x2d, norm2, layer, w_up, w_down, norm_f, final):
    m = x2d.shape[0]
    n_chunks = D_FF // D_MODEL
    return pl.pallas_call(
        functools.partial(_mlp_sample_body, final=final),
        grid=(n_chunks,),
        in_specs=[_const_spec((m, D_MODEL)), _const_spec((1, D_MODEL)),
                  pl.BlockSpec((None, D_MODEL, D_MODEL), lambda c: (layer, 0, c)),
                  pl.BlockSpec((None, D_MODEL, D_MODEL), lambda c: (layer, c, 0)),
                  _const_spec((1, D_MODEL))],
        out_specs=[_const_spec((m, D_MODEL)), pl.BlockSpec((D_MODEL, D_MODEL), lambda c: (0, c)),
                   pl.BlockSpec((D_MODEL, D_MODEL), lambda c: (c, 0))],
        out_shape=[jax.ShapeDtypeStruct((m, D_MODEL), F32), jax.ShapeDtypeStruct((D_MODEL, D_FF), BF16),
                   jax.ShapeDtypeStruct((D_FF, D_MODEL), BF16)],
        scratch_shapes=[pltpu.VMEM((m, D_MODEL), F32)],
        compiler_params=_params(1),
        name="mlp_sample",
    )(x2d, norm2.reshape(1, D_MODEL), w_up, w_down, norm_f.reshape(1, D_MODEL))


def _sample_mix_body(q_ref, k_ref, v_ref, c0_ref, c1_ref, c2_ref, zrow_ref, s_ref, ya_ref, yc_ref, s_out_ref):
    for i in range(q_ref.shape[1]):
        _sample_mix_one(q_ref.at[:, i], k_ref.at[:, i], v_ref.at[:, i], c0_ref.at[i], c1_ref.at[i], c2_ref.at[i],
                        zrow_ref.at[i], s_ref.at[i], ya_ref.at[i], yc_ref.at[i], s_out_ref.at[i])


def _sample_mix_one(q_ref, k_ref, v_ref, c0_ref, c1_ref, c2_ref, zrow_ref, s_ref, ya_ref, yc_ref, s_out_ref):
    lane = lax.broadcasted_iota(jnp.int32, (8, GROUP_W), 1)
    rowi = lax.broadcasted_iota(jnp.int32, (8, GROUP_W), 0)
    head_of_row = (lane // HEAD_DIM) == rowi
    head0 = lax.broadcasted_iota(jnp.int32, (1, GROUP_W), 1) < HEAD_DIM

    outs, lses = [], []
    for g, (c_ref, (win, dil)) in enumerate(zip((c0_ref, c1_ref, c2_ref), DSWA_GROUPS)):
        q, k_new, v_new = q_ref[g], k_ref[g], v_ref[g]
        buf = c_ref[...]
        pos = lax.broadcasted_iota(jnp.int32, (8, win), 1)
        q2 = jnp.where(head_of_row, jnp.broadcast_to(q, (8, GROUP_W)), 0.0)
        scale = HEAD_DIM ** -0.5
        s = jnp.where(pos % dil == 0, _dot(q2, buf[0:GROUP_W, :]) * scale, NEG_BIG)
        s_self = jnp.sum(q2 * k_new, axis=-1, keepdims=True) * scale
        m = jnp.maximum(jnp.max(s, axis=-1, keepdims=True), s_self)
        p, p_self = jnp.exp(s - m), jnp.exp(s_self - m)
        l = jnp.sum(p, axis=-1, keepdims=True) + p_self
        o = (_dot_nt(p, buf[GROUP_W:2 * GROUP_W, :]) + p_self * v_new) / l
        lse = m + jnp.log(l)
        outs.append(jnp.where(head0, o[0:1], o[1:2]))
        lses.append(jnp.where(head0, jnp.broadcast_to(lse[0:1], (1, GROUP_W)),
                              jnp.broadcast_to(lse[1:2], (1, GROUP_W))))
    m = jnp.maximum(jnp.maximum(lses[0], lses[1]), lses[2])
    e = [jnp.exp(x - m) for x in lses]
    ya_ref[...] = (e[0] * outs[0] + e[1] * outs[1] + e[2] * outs[2]) / (e[0] + e[1] + e[2])

    zrow = zrow_ref[...]
    q_col, log_f, k_col = (_as_column(zrow[:, j * C_WIDTH:(j + 1) * C_WIDTH])[:, 0:C_DK] for j in range(3))
    v_row = zrow[:, 3 * C_WIDTH:4 * C_WIDTH]
    gate_row = zrow[:, 4 * C_WIDTH:5 * C_WIDTH]
    v_sel = jnp.concatenate([jnp.broadcast_to(v_row[:, h * C_DK:(h + 1) * C_DK], (C_DK, C_DK))
                             for h in range(C_HEADS)],
                            axis=0)
    s_new = jnp.exp(log_f) * s_ref[...] + k_col * v_sel
    s_out_ref[...] = s_new
    qs = q_col * s_new
    o4 = jnp.concatenate([jnp.sum(qs[h * C_DK:(h + 1) * C_DK, :], axis=0, keepdims=True) for h in range(C_HEADS)],
                         axis=0)
    gate4 = jnp.concatenate([gate_row[:, h * C_DK:(h + 1) * C_DK] for h in range(C_HEADS)], axis=0)
    o4 = o4 * lax.rsqrt(jnp.mean(o4 * o4, axis=-1, keepdims=True) + EPS) * gate4
    yc_ref[...] = o4


def _sample_mix(q, k, v, caches, zc, state, layer, new_state):
    n = q.shape[1]
    per = max(p for p in (4, 2, 1) if n % p == 0)
    row3 = lambda w: pl.BlockSpec((per, 1, w), lambda b: (b, 0, 0))
    grp = pl.BlockSpec((3, per, 1, GROUP_W), lambda b: (0, b, 0, 0))
    cache_specs = [pl.BlockSpec((None, per, 2 * GROUP_W, win), lambda b: (layer, b, 0, 0))
                   for win, _ in DSWA_GROUPS]
    state_spec = pl.BlockSpec((None, per, C_WIDTH, C_DK), lambda b: (layer, b, 0, 0))
    in_specs = [grp] * 3 + cache_specs + [row3(5 * C_WIDTH), state_spec]
    args = [q.reshape(3, n, 1, GROUP_W), k.reshape(3, n, 1, GROUP_W), v.reshape(3, n, 1, GROUP_W),
            *caches, zc.reshape(n, 1, 5 * C_WIDTH), state]
    n_in = len(args)
    aliases = {}
    if new_state is not None:
        aliases = {n_in: 2}
        in_specs = in_specs + [pl.BlockSpec(memory_space=pl.ANY)]
        args = args + [new_state]

    def body(*refs):
        _sample_mix_body(*refs[:n_in], *refs[len(args):])

    return pl.pallas_call(
        body,
        grid=(n // per,),
        in_specs=in_specs,
        out_specs=[pl.BlockSpec((per, 1, GROUP_W), lambda b: (b, 0, 0)),
                   pl.BlockSpec((per, C_HEADS, C_DK), lambda b: (b, 0, 0)), state_spec],
        out_shape=[jax.ShapeDtypeStruct((n, 1, GROUP_W), F32), jax.ShapeDtypeStruct((n, C_HEADS, C_DK), F32),
                   jax.ShapeDtypeStruct(state.shape, F32)],
        input_output_aliases=aliases,
        compiler_params=_params(1),
        name="sample_mix",
    )(*args)


def kernel(x_prompt, x_sample, cache_kv_w128, cache_kv_w512, cache_kv_w2048, state_conv, state_hgrn, w_in, b_gate,
           norm1, conv_w, hgrn_lb, hgrn_norm, w_pa, w_pb, w_pc, w_o, norm2, w_up, w_down, norm_f):
    bp, tp, d = x_prompt.shape
    bs, ts, _ = x_sample.shape
    depth = w_in.shape[0]
    assert d == D_MODEL and ts == 1 and depth == 2
    assert PAST_LEN >= max(win for win, _ in DSWA_GROUPS)

    sm = jax.nn.softmax(hgrn_lb.astype(F32), axis=0)
    lower = jnp.cumsum(sm, axis=0) - sm[0:1]

    cos_p, sin_p = _rope_tables(jnp.arange(tp, dtype=jnp.int32))
    cos_s, sin_s = _rope_tables(jnp.full((bs,), PAST_LEN, dtype=jnp.int32))

    caches = [jnp.transpose(c, (0, 1, 3, 4, 5, 2)).reshape(depth, bs, 2 * GROUP_W, win)
              for c, (win, _) in zip((cache_kv_w128, cache_kv_w512, cache_kv_w2048), DSWA_GROUPS)]
    state = state_hgrn.reshape(depth, bs, C_WIDTH, C_DK)

    xp = x_prompt.reshape(bp * tp, d)
    xs = x_sample.reshape(bs, d)
    kv_p, kv_s = None, None
    conv_p, hgrn_p, conv_s, hgrn_s = [], [], [], None
    for l in range(depth):
        final = l == depth - 1

        q, k_s, v_s, zc, zb, zg, w_in_bf = _sample_inproj(xs, norm1[l], w_in, l, cos_s, sin_s, lower[l],
                                                          hgrn_norm[l])
        ya, yc, hgrn_s = _sample_mix(q, k_s, v_s, caches, zc, state, l, hgrn_s)
        x1, u, wpa, wpb, wpc, wo = _merge_sample(xs, ya.reshape(bs, GROUP_W), yc.reshape(bs, C_WIDTH), zb, zg,
                                                 state_conv[l, :, 0], state_conv[l, :, 1], b_gate[l], conv_w[l],
                                                 l, w_pa, w_pb, w_pc, w_o)
        xs, wup, wdn = _mlp_sample(x1, norm2[l], l, w_up, w_down, norm_f, final)
        conv_s.append(jnp.stack([state_conv[l, :, 1], u], axis=1))

        q, k, v, zc, zl = _inproj(xp, norm1[l], w_in_bf, cos_p, sin_p, lower[l], hgrn_norm[l], tm=2 * ROW_TILE)
        ya, *kv_p = _attn_prompt(q.reshape(3, bp, tp, -1), k.reshape(3, bp, tp, -1), v.reshape(3, bp, tp, -1),
                                 l, depth, kv_p)
        yc, st = _hgrn_prompt(zc.reshape(bp, tp, -1), zl.reshape(bp, tp, -1))
        x1, nc = _merge_prompt(xp.reshape(bp, tp, d), ya, yc, norm1[l], w_in_bf,
                               b_gate[l], conv_w[l], wpa, wpb, wpc, wo)
        xp, *kv_s = _mlp(x1.reshape(bp * tp, d), norm2[l], wup, wdn, norm_f, final, ROW_TILE,
                         k_s, v_s, caches, l, kv_s)
        conv_p.append(nc)
        hgrn_p.append(st)

    def kv6(c, n):
        return jnp.transpose(c.reshape(depth, n, 2, 2, HEAD_DIM, c.shape[3]), (0, 1, 5, 2, 3, 4))

    return (xp.reshape(bp, tp, d), xs.reshape(bs, ts, d),
            kv6(kv_p[0], bp), kv6(kv_p[1], bp), kv6(kv_p[2], bp), jnp.stack(conv_p), jnp.stack(hgrn_p),
            kv6(kv_s[0], bs), kv6(kv_s[1], bs), kv6(kv_s[2], bs), jnp.stack(conv_s),
            hgrn_s.reshape(depth, bs, C_HEADS, C_DK, C_DK))
```

```python
import functools

import jax
import jax.numpy as jnp
from jax import lax
from jax.experimental import pallas as pl
from jax.experimental.pallas import tpu as pltpu

F32 = jnp.float32
BF16 = jnp.bfloat16

D_MODEL = 1024
HEAD_DIM = 64
DSWA_GROUPS = ((128, 1), (512, 4), (2048, 16))
BAND = 128
GROUP_W = 2 * HEAD_DIM
A_WIDTH = 3 * GROUP_W
CONV_WIDTH = 384
CONV_K = 3
C_HEADS = 4
C_DK = 64
C_WIDTH = C_HEADS * C_DK
N_BRANCH = 3
D_FF = 4 * D_MODEL
ROPE_THETA = 10000.0
EPS = 1e-6
NEG_BIG = -1e30
LB_FLOOR = 1e-30
PAST_LEN = 8192
IN_COLS = 3 * A_WIDTH + 3 * CONV_WIDTH + 4 * C_WIDTH + N_BRANCH * D_MODEL
COL_B = 3 * A_WIDTH
COL_C = COL_B + 3 * CONV_WIDTH
COL_G = COL_C + 4 * C_WIDTH
CHUNK = 64
SUBLANES = 8
SUB = SUBLANES
LEVELS = (32, 16, 8)
VMEM_LIMIT = 56 * 1024 * 1024
TILES_PER_TRIP = 8
LANES = 128
STREAM_BUFFERS = 3
ROW_TILE = 512
SUB_ROWS = 256


def _params(n_grid):
    return pltpu.CompilerParams(dimension_semantics=("arbitrary",) * n_grid,
                                vmem_limit_bytes=VMEM_LIMIT)


def _const_spec(shape):
    nd = len(shape)
    return pl.BlockSpec(shape, lambda *_: (0,) * nd, pipeline_mode=pl.Buffered(1))


def _layer_spec(stacked, layer):
    zeros = (0,) * (stacked.ndim - 1)
    return pl.BlockSpec((None,) + stacked.shape[1:], lambda *_: (layer,) + zeros, pipeline_mode=pl.Buffered(1))


def _dot(a, b):
    return jnp.dot(a.astype(BF16), b.astype(BF16), preferred_element_type=F32)


def _dot_nt(a, b):
    return lax.dot_general(a.astype(BF16), b.astype(BF16), (((1,), (1,)), ((), ())),
                           preferred_element_type=F32)


def _dot_tn(a, b):
    return lax.dot_general(a.astype(BF16), b.astype(BF16), (((0,), (0,)), ((), ())),
                           preferred_element_type=F32)


def _rmsnorm(x, g):
    return x * lax.rsqrt(jnp.mean(x * x, axis=-1, keepdims=True) + EPS) * g


def _store_qkv_and_hgrn(za, zc, cos, sin, lb, gnorm, q_ref, k_ref, v_ref, zc_ref, rows, zl_ref=None):
    chunked = zl_ref is not None
    lane = lax.broadcasted_iota(jnp.int32, cos.shape, 1)
    first_half = (lane % HEAD_DIM) < HEAD_DIM // 2
    for out_ref, c0 in ((q_ref, 0), (k_ref, A_WIDTH)):
        for g in range(3):
            z = za[:, c0 + g * GROUP_W:c0 + (g + 1) * GROUP_W]
            partner = jnp.where(first_half, pltpu.roll(z, GROUP_W - HEAD_DIM // 2, 1),
                                pltpu.roll(z, HEAD_DIM // 2, 1))
            out_ref[g, rows, :] = z * cos + partner * sin
    for g in range(3):
        v_ref[g, rows, :] = za[:, 2 * A_WIDTH + g * GROUP_W:2 * A_WIDTH + (g + 1) * GROUP_W]
    parts = _hgrn_inputs(zc, lb, gnorm, chunked)
    for i, part in enumerate(parts):
        zc_ref[rows, i * C_WIDTH:(i + 1) * C_WIDTH] = part
    if chunked:
        q, b, k = parts[0:3]
        for i, part in enumerate(_hgrn_level_operands(q, b, k)):
            zl_ref[rows, i * C_WIDTH:(i + 1) * C_WIDTH] = part


def _inproj_body(x_ref, g_ref, w_ref, cos_ref, sin_ref, lb_ref, gn_ref, q_ref, k_ref, v_ref, zc_ref, zl_ref):
    tm = x_ref.shape[0]
    sub = min(tm, SUB_ROWS)
    for r0 in range(0, tm, sub):
        rows = slice(r0, r0 + sub)
        h = _rmsnorm(x_ref[rows, :], g_ref[...]).astype(BF16)
        za = jnp.dot(h, w_ref[:, 0:COL_B], preferred_element_type=F32)
        zc = jnp.dot(h, w_ref[:, COL_C:COL_G], preferred_element_type=F32)
        _store_qkv_and_hgrn(za, zc, cos_ref[rows, :], sin_ref[rows, :], lb_ref[...], gn_ref[...],
                            q_ref, k_ref, v_ref, zc_ref, rows, zl_ref)


def _inproj(x2d, norm, w_in, cos, sin, lb, gnorm, tm):
    m = x2d.shape[0]
    assert tm % CHUNK == 0
    n_pos = cos.shape[0] // tm
    row = lambda w: pl.BlockSpec((tm, w), lambda i: (i, 0))
    tab = pl.BlockSpec((tm, GROUP_W), lambda i: (i % n_pos, 0))
    grp = pl.BlockSpec((3, tm, GROUP_W), lambda i: (0, i, 0))
    return pl.pallas_call(
        _inproj_body,
        grid=(m // tm,),
        in_specs=[row(D_MODEL), _const_spec((1, D_MODEL)), _const_spec(w_in.shape),
                  tab, tab, _const_spec((1, C_WIDTH)), _const_spec((1, C_WIDTH))],
        out_specs=[grp] * 3 + [row(5 * C_WIDTH), row(2 * len(LEVELS) * C_WIDTH)],
        out_shape=[jax.ShapeDtypeStruct((3, m, GROUP_W), F32)] * 3
        + [jax.ShapeDtypeStruct((m, 5 * C_WIDTH), F32), jax.ShapeDtypeStruct((m, 2 * len(LEVELS) * C_WIDTH), BF16)],
        compiler_params=_params(1),
        name="inproj",
    )(x2d, norm.reshape(1, D_MODEL), w_in, cos, sin, lb.reshape(1, C_WIDTH), gnorm.reshape(1, C_WIDTH))


def _sample_inproj_body(x_ref, g_ref, w_ref, cos_ref, sin_ref, lb_ref, gn_ref,
                        q_ref, k_ref, v_ref, zc_ref, zb_ref, zg_ref, wbf_ref, z_scr, *, n_blocks):
    j = pl.program_id(0)
    w = w_ref[...].astype(BF16)
    wbf_ref[...] = w
    h = _rmsnorm(x_ref[...], g_ref[...]).astype(BF16)
    z_scr[j] = jnp.dot(h, w, preferred_element_type=F32)

    @pl.when(j == n_blocks - 1)
    def _():
        z = jnp.concatenate([z_scr[i] for i in range(n_blocks)], axis=1)
        _store_qkv_and_hgrn(z[:, 0:COL_B], z[:, COL_C:COL_G], cos_ref[...], sin_ref[...], lb_ref[...],
                            gn_ref[...], q_ref, k_ref, v_ref, zc_ref, slice(None))
        zb_ref[...] = z[:, COL_B:COL_C]
        zg_ref[...] = z[:, COL_G:IN_COLS]


def _sample_inproj(x2d, norm, w_in_f32, layer, cos, sin, lb, gnorm, n_blocks=5):
    m = x2d.shape[0]
    bw = IN_COLS // n_blocks
    assert bw * n_blocks == IN_COLS and bw % 128 == 0
    widths = (5 * C_WIDTH, 3 * CONV_WIDTH, N_BRANCH * D_MODEL)
    return pl.pallas_call(
        functools.partial(_sample_inproj_body, n_blocks=n_blocks),
        grid=(n_blocks,),
        in_specs=[_const_spec((m, D_MODEL)), _const_spec((1, D_MODEL)),
                  pl.BlockSpec((None, D_MODEL, bw), lambda j: (layer, 0, j)),
                  _const_spec((m, GROUP_W)), _const_spec((m, GROUP_W)),
                  _const_spec((1, C_WIDTH)), _const_spec((1, C_WIDTH))],
        out_specs=[_const_spec((3, m, GROUP_W))] * 3 + [_const_spec((m, w)) for w in widths]
        + [pl.BlockSpec((D_MODEL, bw), lambda j: (0, j))],
        out_shape=[jax.ShapeDtypeStruct((3, m, GROUP_W), F32)] * 3
        + [jax.ShapeDtypeStruct((m, w), F32) for w in widths]
        + [jax.ShapeDtypeStruct((D_MODEL, IN_COLS), BF16)],
        scratch_shapes=[pltpu.VMEM((n_blocks, m, bw), F32)],
        compiler_params=_params(1),
        name="sample_inproj",
    )(x2d, norm.reshape(1, D_MODEL), w_in_f32, cos, sin, lb.reshape(1, C_WIDTH), gnorm.reshape(1, C_WIDTH))


def _rope_tables(pos):
    half = HEAD_DIM // 2
    inv = ROPE_THETA ** (-jnp.arange(half, dtype=F32) / half)
    ang = pos.astype(F32)[:, None] * inv[None, :]
    cos, sin = jnp.cos(ang), jnp.sin(ang)
    return jnp.tile(cos, (1, 4)), jnp.tile(jnp.concatenate([-sin, sin], axis=1), (1, 2))


def _attend(tiles):
    head0 = lax.broadcasted_iota(jnp.int32, (BAND, GROUP_W), 1) < HEAD_DIM
    scale = HEAD_DIM ** -0.5
    q2 = [jnp.concatenate([jnp.where(head0, q * scale, 0.0), jnp.where(head0, 0.0, q * scale)],
                          axis=0).astype(BF16) for q, _ in tiles]
    keys = [jnp.concatenate([k for k, _, _ in blocks], axis=0) for _, blocks in tiles]
    vals = [jnp.concatenate([v1 for _, v1, _ in blocks], axis=0) for _, blocks in tiles]
    valid = [jnp.concatenate([ok for _, _, ok in blocks], axis=1) for _, blocks in tiles]
    s = [jnp.where(ok, _dot_nt(qq, kk), NEG_BIG) for qq, kk, ok in zip(q2, keys, valid)]
    m = [jnp.max(x, axis=-1, keepdims=True) for x in s]
    p = [jnp.exp(x - mm).astype(BF16) for x, mm in zip(s, m)]
    ext = [jnp.dot(pp, vv, preferred_element_type=F32) for pp, vv in zip(p, vals)]
    results = []
    for e, mm in zip(ext, m):
        den = e[:, GROUP_W:]
        o2 = e[:, 0:GROUP_W] / den
        lse2 = mm + jnp.log(den)
        results.append((jnp.where(head0, o2[0:BAND], o2[BAND:]), jnp.where(head0, lse2[0:BAND], lse2[BAND:])))
    return results


def _attn_prompt_body(q_ref, k_ref, v_ref, ya_ref, c0_ref, c1_ref, c2_ref, o_scr, lse_scr, *, seq):
    qi = lax.broadcasted_iota(jnp.int32, (2 * BAND, BAND), 0) % BAND
    kj = lax.broadcasted_iota(jnp.int32, (2 * BAND, BAND), 1)
    cur_valid = kj <= qi
    prev_valid = kj >= qi
    ones = jnp.ones((BAND, GROUP_W), BF16)

    def rows(start, dil):
        return pl.ds(start, BAND, stride=dil) if dil > 1 else pl.ds(start, BAND)

    def key_block(g, r):
        return k_ref[g, r, :].astype(BF16), jnp.concatenate([v_ref[g, r, :].astype(BF16), ones], axis=1)

    def run(g, dil, chains):
        tiles, slices = [], []
        for starts, first_prev in chains:
            chain = [rows(s, dil) for s in starts]
            keys = [key_block(g, r) for r in chain]
            for j, r in enumerate(chain):
                blocks = [(*keys[j], cur_valid)]
                if j > 0:
                    blocks.append((*keys[j - 1], prev_valid))
                elif first_prev is not None:
                    blocks.append((*key_block(g, rows(first_prev[0], dil)), prev_valid & first_prev[1]))
                tiles.append((q_ref[g, r, :], blocks))
            slices += chain
        for r, (o, lse) in zip(slices, _attend(tiles)):
            o_scr[g, r, :] = o
            lse_scr[g, r, :] = lse

    for g, (_, dil) in enumerate(DSWA_GROUPS):
        n_blocks = seq // dil // BAND
        if dil == 1:
            def dense(i, carry):
                s0 = pl.multiple_of(TILES_PER_TRIP * i * BAND, TILES_PER_TRIP * BAND)
                prev = pl.multiple_of(jnp.maximum(TILES_PER_TRIP * i - 1, 0) * BAND, BAND)
                run(g, dil, [([s0 + u * BAND for u in range(TILES_PER_TRIP)], (prev, i > 0))])
                for gw, (c_ref, (win, _)) in enumerate(zip((c0_ref, c1_ref, c2_ref), DSWA_GROUPS)):
                    if win == seq:
                        for j in range(0, TILES_PER_TRIP * BAND, ROW_TILE):
                            r = pl.ds(s0 + j, ROW_TILE)
                            c_ref[0:GROUP_W, r] = k_ref[gw, r, :].T
                            c_ref[GROUP_W:2 * GROUP_W, r] = v_ref[gw, r, :].T
                return carry

            lax.fori_loop(0, n_blocks // TILES_PER_TRIP, dense, 0)
        else:
            per = max(1, TILES_PER_TRIP // n_blocks)
            assert dil % per == 0

            def residues(j, carry, g=g, dil=dil, n_blocks=n_blocks, per=per):
                run(g, dil, [([per * j + u + i * BAND * dil for i in range(n_blocks)], None) for u in range(per)])
                return carry

            lax.fori_loop(0, dil // per, residues, 0)

    step = 256

    def merge(i, carry):
        sl = pl.ds(pl.multiple_of(i * step, step), step)
        lse = [lse_scr[g, sl, :] for g in range(3)]
        m = jnp.maximum(jnp.maximum(lse[0], lse[1]), lse[2])
        e = [jnp.exp(x - m) for x in lse]
        acc = e[0] * o_scr[0, sl, :] + e[1] * o_scr[1, sl, :] + e[2] * o_scr[2, sl, :]
        ya_ref[0, sl, :] = acc / (e[0] + e[1] + e[2])
        return carry

    lax.fori_loop(0, seq // step, merge, 0)

    for g, (c_ref, (win, _)) in enumerate(zip((c0_ref, c1_ref, c2_ref), DSWA_GROUPS)):
        if win == seq:
            continue
        for j in range(0, win, ROW_TILE):
            w = min(ROW_TILE, win - j)
            c_ref[0:GROUP_W, j:j + w] = k_ref[g, seq - win + j:seq - win + j + w, :].T
            c_ref[GROUP_W:2 * GROUP_W, j:j + w] = v_ref[g, seq - win + j:seq - win + j + w, :].T


def _attn_prompt(q, k, v, layer, depth, caches):
    _, bsz, seq, _ = q.shape
    assert seq % (BAND * TILES_PER_TRIP) == 0
    assert all(seq >= win and seq % (dil * BAND) == 0 for win, dil in DSWA_GROUPS)
    qkv_spec = pl.BlockSpec((3, None, seq, GROUP_W), lambda b: (0, b, 0, 0))
    cache_specs = [pl.BlockSpec((None, None, 2 * GROUP_W, win), lambda b: (layer, b, 0, 0))
                   for win, _ in DSWA_GROUPS]
    cache_shapes = [jax.ShapeDtypeStruct((depth, bsz, 2 * GROUP_W, win), F32) for win, _ in DSWA_GROUPS]
    in_specs = [qkv_spec] * 3
    args = [q, k, v]
    aliases = {}
    if caches is not None:
        in_specs = in_specs + [pl.BlockSpec(memory_space=pl.ANY)] * 3
        args = args + list(caches)
        aliases = {3: 1, 4: 2, 5: 3}

    def body(*refs):
        ins, rest = refs[:3], refs[3 + (3 if caches is not None else 0):]
        _attn_prompt_body(*ins, *rest, seq=seq)

    return pl.pallas_call(
        body,
        grid=(bsz,),
        in_specs=in_specs,
        out_specs=[pl.BlockSpec((1, seq, GROUP_W), lambda b: (b, 0, 0))] + cache_specs,
        out_shape=[jax.ShapeDtypeStruct((bsz, seq, GROUP_W), F32)] + cache_shapes,
        scratch_shapes=[pltpu.VMEM((3, seq, GROUP_W), F32), pltpu.VMEM((3, seq, GROUP_W), F32)],
        input_output_aliases=aliases,
        compiler_params=_params(1),
        name="attn_prompt",
    )(*args)


def _log_forget_and_kgate(zf, lb):
    e = jnp.exp(-jnp.abs(zf))
    log_sig = -(jnp.maximum(-zf, 0.0) + jnp.log(1.0 + e))
    a = jnp.log(jnp.maximum(lb, LB_FLOOR))
    c = jnp.log1p(-lb) + log_sig
    log_f = jnp.maximum(a, c) + jnp.log(1.0 + jnp.exp(-jnp.abs(a - c)))
    kgate = (1.0 - lb) * (jnp.where(zf >= 0.0, e, 1.0) / (1.0 + e))
    return log_f, kgate


def _silu(x):
    return x * jax.nn.sigmoid(x)


def _split3(x):
    hi = x.astype(BF16)
    r1 = x - hi.astype(F32)
    mid = r1.astype(BF16)
    lo = (r1 - mid.astype(F32)).astype(BF16)
    return hi, mid, lo


def _hgrn_inputs(zc, lb, gnorm, chunked):
    q, zf, v, og = (zc[:, i * C_WIDTH:(i + 1) * C_WIDTH] for i in range(4))
    log_f, k = _log_forget_and_kgate(zf, lb)
    if chunked:
        n = zc.shape[0]
        r = lax.broadcasted_iota(jnp.int32, (n, n), 0)
        c = lax.broadcasted_iota(jnp.int32, (n, n), 1)
        tri = jnp.where((c <= r) & ((r // CHUNK) == (c // CHUNK)), 1.0, 0.0).astype(BF16)
        b = functools.reduce(jnp.add, [jnp.dot(tri, p, preferred_element_type=F32) for p in _split3(log_f)])
    else:
        b = log_f
    return q, b, k, v, gnorm * _silu(og)


def _hgrn_level_operands(q, b, k):
    n = q.shape[0]
    row = lax.broadcasted_iota(jnp.int32, (n, C_WIDTH), 0)
    out = []
    for m in LEVELS:
        ref_q, ref_k = [], []
        for j in range(n // m):
            own = b[j * m:(j + 1) * m, :]
            if j % 2 == 1:
                ref_q.append(jnp.broadcast_to(b[j * m - 1:j * m, :], (m, C_WIDTH)))
                ref_k.append(own)
            else:
                ref_q.append(own)
                ref_k.append(jnp.broadcast_to(b[(j + 1) * m - 1:(j + 1) * m, :], (m, C_WIDTH)))
        upper = (row // m) % 2 == 1
        qt = jnp.where(upper, q * jnp.exp(jnp.minimum(b - jnp.concatenate(ref_q, axis=0), 0.0)), 0.0)
        kt = jnp.where(upper, 0.0, k * jnp.exp(jnp.minimum(jnp.concatenate(ref_k, axis=0) - b, 0.0)))
        out += [qt.astype(BF16), kt.astype(BF16)]
    return out


def _hgrn_chunks(seqs, cst):
    n = CHUNK
    ns = len(seqs)
    ones_bd = cst["ones_bd"]

    def head_sums(x):
        return jnp.dot(x.astype(BF16), ones_bd, preferred_element_type=F32)

    tiles = (n // SUBLANES, SUBLANES, C_WIDTH)
    ys = []
    for q, b, k, *_ in seqs:
        q3, k3, b3 = (x.reshape(tiles) for x in (q, k, b))
        f3 = jnp.exp(jnp.minimum(b3 - pltpu.roll(b3, 1, 1), 0.0))
        ys.append((q * k).astype(BF16))
        decay = None
        for d in range(1, SUB):
            decay = f3 if d == 1 else decay * pltpu.roll(f3, d - 1, 1)
            ys.append((q3 * pltpu.roll(k3, d, 1) * decay).reshape(n, C_WIDTH).astype(BF16))
    att_d = head_sums(jnp.concatenate(ys, axis=0)).astype(BF16)
    lvl_scores = []
    for lvl in range(len(LEVELS)):
        k_stack = [jnp.concatenate([s[5][2 * lvl + 1]] * C_HEADS, axis=0) * cst["stack"] for s in seqs]
        lvl_scores.append([_dot_nt(s[5][2 * lvl], ks).astype(BF16) for s, ks in zip(seqs, k_stack)])
    q_hat = [s[0] * jnp.exp(s[1]) for s in seqs]
    b_last = [s[1][n - 1:n, :] for s in seqs]
    k_hat = [s[2] * jnp.exp(bl - s[1]) for s, bl in zip(seqs, b_last)]
    o_st = [_dot_nt(qh, s[6]) for qh, s in zip(q_hat, seqs)]
    upd = [_dot_tn(s[3], kh) for s, kh in zip(seqs, k_hat)]

    att = [functools.reduce(jnp.add, [att_d[(i * SUB + d) * n:(i * SUB + d + 1) * n, :] * cst["place"][d]
                                      for d in range(SUB)]) for i in range(ns)]
    for lvl, scores in enumerate(lvl_scores):
        att = [x + y * cst["same_block"][lvl] for x, y in zip(att, scores)]

    v_stack = [jnp.concatenate([s[3].astype(BF16)] * C_HEADS, axis=0) * cst["stack"] for s in seqs]
    o = [jnp.dot(x, vs, preferred_element_type=F32) + y for x, vs, y in zip(att, v_stack, o_st)]
    st = [s[6] * jnp.exp(bl) + u * cst["same_head"] for s, bl, u in zip(seqs, b_last, upd)]

    sq = [x * x for x in o]
    sq_hi = [x.astype(BF16) for x in sq]
    sq_lo = [(x - h.astype(F32)).astype(BF16) for x, h in zip(sq, sq_hi)]
    ms = head_sums(jnp.concatenate(sq_hi + sq_lo, axis=0))
    out = []
    for i, (x, s) in enumerate(zip(o, seqs)):
        m2 = ms[i * n:(i + 1) * n, :] + ms[(ns + i) * n:(ns + i + 1) * n, :]
        out.append((x * lax.rsqrt(m2 * (1.0 / C_DK) + EPS) * s[4], st[i]))
    return out


def _hgrn_constants():
    n = CHUNK
    hr = lax.broadcasted_iota(jnp.int32, (C_WIDTH, C_WIDTH), 0) // C_DK
    hc = lax.broadcasted_iota(jnp.int32, (C_WIDTH, C_WIDTH), 1) // C_DK
    same_head = jnp.where(hr == hc, 1.0, 0.0)
    srow = lax.broadcasted_iota(jnp.int32, (n, C_HEADS * n), 0)
    scol = lax.broadcasted_iota(jnp.int32, (n, C_HEADS * n), 1) % n
    return {
        "ones_bd": same_head.astype(BF16),
        "same_head": same_head,
        "stack": same_head.astype(BF16),
        "place": [jnp.where((scol == srow - d) & (srow % SUB >= d), 1.0, 0.0).astype(BF16) for d in range(SUB)],
        "same_block": [jnp.where(srow // (2 * m) == scol // (2 * m), 1.0, 0.0).astype(BF16) for m in LEVELS],
    }


def _hgrn_prompt_body(zc_ref, zl_ref, yc_ref, st_ref, st_scr, *, rows, n_seq):
    t = pl.program_id(1)

    @pl.when(t == 0)
    def _():
        st_scr[...] = jnp.zeros_like(st_scr)

    cst = _hgrn_constants()

    def chunk(c, carry):
        sl = pl.ds(pl.multiple_of(c * CHUNK, CHUNK), CHUNK)
        seqs = [(*(zc_ref[i, sl, j * C_WIDTH:(j + 1) * C_WIDTH] for j in range(5)),
                 [zl_ref[i, sl, j * C_WIDTH:(j + 1) * C_WIDTH] for j in range(2 * len(LEVELS))], st_scr[i])
                for i in range(n_seq)]
        for i, (o, st) in enumerate(_hgrn_chunks(seqs, cst)):
            st_scr[i] = st
            yc_ref[i, sl, :] = o
        return carry

    lax.fori_loop(0, rows // CHUNK, chunk, 0)

    @pl.when(t == pl.num_programs(1) - 1)
    def _():
        for i in range(n_seq):
            for h in range(C_HEADS):
                st_ref[i, h] = st_scr[i, h * C_DK:(h + 1) * C_DK, h * C_DK:(h + 1) * C_DK]


def _hgrn_prompt(zc, zl, rows=ROW_TILE // 2):
    bsz, seq, _ = zc.shape
    n_seq = max(n for n in (8, 4, 2, 1) if bsz % n == 0)
    assert CHUNK == C_DK
    yc, st = pl.pallas_call(
        functools.partial(_hgrn_prompt_body, rows=rows, n_seq=n_seq),
        grid=(bsz // n_seq, seq // rows),
        in_specs=[pl.BlockSpec((n_seq, rows, 5 * C_WIDTH), lambda b, t: (b, t, 0)),
                  pl.BlockSpec((n_seq, rows, 2 * len(LEVELS) * C_WIDTH), lambda b, t: (b, t, 0))],
        out_specs=[pl.BlockSpec((n_seq, rows, C_WIDTH), lambda b, t: (b, t, 0)),
                   pl.BlockSpec((n_seq, C_HEADS, C_DK, C_DK), lambda b, t: (b, 0, 0, 0))],
        out_shape=[jax.ShapeDtypeStruct((bsz, seq, C_WIDTH), F32),
                   jax.ShapeDtypeStruct((bsz, C_HEADS, C_DK, C_DK), F32)],
        scratch_shapes=[pltpu.VMEM((n_seq, C_WIDTH, C_WIDTH), F32)],
        compiler_params=_params(2),
        name="hgrn_prompt",
    )(zc, zl)
    return yc, jnp.swapaxes(st, -1, -2)


def _gated_merge(x, ya, yb, yc, zg, bg, wpa, wpb, wpc, wo):
    g = jax.nn.sigmoid(zg + bg)
    mix = (g[:, 0:D_MODEL] * _dot(ya, wpa) + g[:, D_MODEL:2 * D_MODEL] * _dot(yb, wpb)
           + g[:, 2 * D_MODEL:3 * D_MODEL] * _dot(yc, wpc))
    return x + _dot(mix, wo)


def _project_conv_and_gates(x, g1, w_ref):
    h = _rmsnorm(x, g1).astype(BF16)
    return (jnp.dot(h, w_ref[:, COL_B:COL_C], preferred_element_type=F32),
            jnp.dot(h, w_ref[:, COL_G:IN_COLS], preferred_element_type=F32))


def _merge_prompt_body(x_ref, ya_ref, yc_ref, g1_ref, w_ref, bg_ref, cw_ref, wpa_ref, wpb_ref, wpc_ref,
                       wo_ref, x1_ref, nc_ref, u_scr, *, rows):
    @pl.when(pl.program_id(1) == 0)
    def _():
        u_scr[0:8, :] = jnp.zeros((8, CONV_WIDTH), F32)

    sub = min(rows, ROW_TILE)
    for r0 in range(0, rows, sub):
        sl = slice(r0, r0 + sub)
        x = x_ref[0, sl, :]
        zb, zg = _project_conv_and_gates(x, g1_ref[...], w_ref)
        u = zb[:, 2 * CONV_WIDTH:3 * CONV_WIDTH] * zb[:, 0:CONV_WIDTH]
        u_scr[8 + r0:8 + r0 + sub, :] = u
        conv = (cw_ref[0:1, :] * u_scr[pl.ds(6 + r0, sub), :] + cw_ref[1:2, :] * u_scr[pl.ds(7 + r0, sub), :]
                + cw_ref[2:3, :] * u)
        yb = zb[:, CONV_WIDTH:2 * CONV_WIDTH] * conv
        x1_ref[0, sl, :] = _gated_merge(x, ya_ref[0, sl, :], yb, yc_ref[0, sl, :], zg, bg_ref[...],
                                        wpa_ref[...], wpb_ref[...], wpc_ref[...], wo_ref[...])
    nc_ref[0] = u_scr[8 + rows - (CONV_K - 1):8 + rows, :]
    u_scr[0:8, :] = u_scr[rows:rows + 8, :]


def _merge_prompt(x, ya, yc, norm1, w_in, b_gate, conv_w, wpa, wpb, wpc, wo, rows=2 * ROW_TILE):
    bsz, seq, _ = x.shape
    blk = lambda w: pl.BlockSpec((1, rows, w), lambda b, t: (b, t, 0))
    return pl.pallas_call(
        functools.partial(_merge_prompt_body, rows=rows),
        grid=(bsz, seq // rows),
        in_specs=[blk(D_MODEL), blk(GROUP_W), blk(C_WIDTH),
                  _const_spec((1, D_MODEL)), _const_spec(w_in.shape),
                  _const_spec((1, N_BRANCH * D_MODEL)), _const_spec((CONV_K, CONV_WIDTH)),
                  _const_spec(wpa.shape), _const_spec(wpb.shape), _const_spec(wpc.shape), _const_spec(wo.shape)],
        out_specs=[blk(D_MODEL), pl.BlockSpec((1, CONV_K - 1, CONV_WIDTH), lambda b, t: (b, 0, 0))],
        out_shape=[jax.ShapeDtypeStruct((bsz, seq, D_MODEL), F32),
                   jax.ShapeDtypeStruct((bsz, CONV_K - 1, CONV_WIDTH), F32)],
        scratch_shapes=[pltpu.VMEM((rows + 8, CONV_WIDTH), F32)],
        compiler_params=_params(2),
        name="merge_prompt",
    )(x, ya, yc, norm1.reshape(1, D_MODEL), w_in, b_gate.reshape(1, -1), conv_w, wpa, wpb, wpc, wo)


def _merge_sample_body(x_ref, ya_ref, yc_ref, zb_ref, zg_ref, p0_ref, p1_ref, bg_ref, cw_ref, wpa_ref,
                       wpb_ref, wpc_ref, wo_ref, x1_ref, u_ref, wpa_bf_ref, wpb_bf_ref, wpc_bf_ref, wo_bf_ref):
    bf = []
    for w_ref, bf_ref in ((wpa_ref, wpa_bf_ref), (wpb_ref, wpb_bf_ref), (wpc_ref, wpc_bf_ref),
                          (wo_ref, wo_bf_ref)):
        bf.append(w_ref[...].astype(BF16))
        bf_ref[...] = bf[-1]
    zb = zb_ref[...]
    u = zb[:, 2 * CONV_WIDTH:3 * CONV_WIDTH] * zb[:, 0:CONV_WIDTH]
    conv = cw_ref[0:1, :] * p0_ref[...] + cw_ref[1:2, :] * p1_ref[...] + cw_ref[2:3, :] * u
    yb = zb[:, CONV_WIDTH:2 * CONV_WIDTH] * conv
    u_ref[...] = u
    x1_ref[...] = _gated_merge(x_ref[...], ya_ref[...], yb, yc_ref[...], zg_ref[...], bg_ref[...], *bf)


def _merge_sample(x, ya, yc, zb, zg, prev0, prev1, b_gate, conv_w, layer, w_pa, w_pb, w_pc, w_o):
    m = x.shape[0]
    small = (x, ya, yc, zb, zg, prev0, prev1, b_gate.reshape(1, -1), conv_w)
    stacks = (w_pa, w_pb, w_pc, w_o)
    return pl.pallas_call(
        _merge_sample_body,
        grid=(1,),
        in_specs=[_const_spec(a.shape) for a in small] + [_layer_spec(w, layer) for w in stacks],
        out_specs=[_const_spec((m, D_MODEL)), _const_spec((m, CONV_WIDTH))]
        + [_const_spec(w.shape[1:]) for w in stacks],
        out_shape=[jax.ShapeDtypeStruct((m, D_MODEL), F32), jax.ShapeDtypeStruct((m, CONV_WIDTH), F32)]
        + [jax.ShapeDtypeStruct(w.shape[1:], BF16) for w in stacks],
        compiler_params=_params(1),
        name="merge_sample",
    )(*small, *stacks)


def _as_column(row):
    return jnp.concatenate([jnp.broadcast_to(row[:, c:c + LANES], (LANES, LANES)).T
                            for c in range(0, row.shape[1], LANES)], axis=0)


def _shift_window_buffers(k_ref, v_ref, cache_refs, new_refs):
    for g, (c_ref, n_ref, (win, _)) in enumerate(zip(cache_refs, new_refs, DSWA_GROUPS)):
        rolled = pltpu.roll(c_ref[...], win - 1, 1)
        new_col = jnp.concatenate([_as_column(k_ref[g]), _as_column(v_ref[g])], axis=0)
        last = lax.broadcasted_iota(jnp.int32, new_col.shape, 1) == LANES - 1
        if win > LANES:
            n_ref[:, 0:win - LANES] = rolled[:, 0:win - LANES]
        n_ref[:, win - LANES:win] = jnp.where(last, new_col, rolled[:, win - LANES:win])


def _mlp_body(x_ref, g2_ref, wup_ref, wdn_ref, gf_ref, k_ref, v_ref, c0_ref, c1_ref, c2_ref,
              out_ref, n0_ref, n1_ref, n2_ref, *, final):
    _shift_window_buffers(k_ref, v_ref, (c0_ref, c1_ref, c2_ref), (n0_ref, n1_ref, n2_ref))
    tm = x_ref.shape[0]
    sub = min(tm, ROW_TILE)
    for r0 in range(0, tm, sub):
        rows = slice(r0, r0 + sub)
        x = x_ref[rows, :]
        h = _rmsnorm(x, g2_ref[...]).astype(BF16)
        acc = x
        for c in range(D_FF // D_MODEL):
            cols = slice(c * D_MODEL, (c + 1) * D_MODEL)
            a = jnp.maximum(jnp.dot(h, wup_ref[:, cols], preferred_element_type=F32), 0.0)
            acc = acc + jnp.dot((a * a).astype(BF16), wdn_ref[cols, :], preferred_element_type=F32)
        out_ref[rows, :] = _rmsnorm(acc, gf_ref[...]) if final else acc


def _mlp(x2d, norm2, wup, wdn, norm_f, final, tm, k_new, v_new, caches, layer, new_caches):
    m = x2d.shape[0]
    depth, n = caches[0].shape[:2]
    steps = m // tm
    assert n <= steps
    row = pl.BlockSpec((tm, D_MODEL), lambda i: (i, 0))
    seq = lambda i: jnp.minimum(i, n - 1)
    grp = pl.BlockSpec((3, None, 1, GROUP_W), lambda i: (0, seq(i), 0, 0))
    cache_specs = [pl.BlockSpec((None, None, 2 * GROUP_W, win), lambda i: (layer, seq(i), 0, 0))
                   for win, _ in DSWA_GROUPS]
    in_specs = [row, _const_spec((1, D_MODEL)), _const_spec(wup.shape), _const_spec(wdn.shape),
                _const_spec((1, D_MODEL)), grp, grp] + cache_specs
    args = [x2d, norm2.reshape(1, D_MODEL), wup, wdn, norm_f.reshape(1, D_MODEL),
            k_new.reshape(3, n, 1, GROUP_W), v_new.reshape(3, n, 1, GROUP_W), *caches]
    aliases = {}
    if new_caches is not None:
        aliases = {len(args) + i: 1 + i for i in range(3)}
        in_specs = in_specs + [pl.BlockSpec(memory_space=pl.ANY)] * 3
        args = args + list(new_caches)
    n_in = len(cache_specs) + 7

    def body(*refs):
        _mlp_body(*refs[:n_in], *refs[len(args):], final=final)

    return pl.pallas_call(
        body,
        grid=(steps,),
        in_specs=in_specs,
        out_specs=[row] + cache_specs,
        out_shape=[jax.ShapeDtypeStruct((m, D_MODEL), F32)]
        + [jax.ShapeDtypeStruct((depth, n, 2 * GROUP_W, win), F32) for win, _ in DSWA_GROUPS],
        input_output_aliases=aliases,
        compiler_params=_params(1),
        name="mlp",
    )(*args)


def _mlp_sample_body(x_ref, g2_ref, wup_ref, wdn_ref, gf_ref, out_ref, wup_bf_ref, wdn_bf_ref, acc_scr, *, final):
    c = pl.program_id(0)
    wu = wup_ref[...].astype(BF16)
    wd = wdn_ref[...].astype(BF16)
    wup_bf_ref[...] = wu
    wdn_bf_ref[...] = wd
    x = x_ref[...]
    h = _rmsnorm(x, g2_ref[...]).astype(BF16)
    a = jnp.maximum(jnp.dot(h, wu, preferred_element_type=F32), 0.0)
    part = jnp.dot((a * a).astype(BF16), wd, preferred_element_type=F32)

    @pl.when(c == 0)
    def _():
        acc_scr[...] = x + part

    @pl.when(c > 0)
    def _():
        acc_scr[...] = acc_scr[...] + part

    @pl.when(c == pl.num_programs(0) - 1)
    def _():
        acc = acc_scr[...]
        out_ref[...] = _rmsnorm(acc, gf_ref[...]) if final else acc


def _mlp_sample(x2d, norm2, layer, w_up, w_down, norm_f, final):
    m = x2d.shape[0]
    n_chunks = D_FF // D_MODEL
    return pl.pallas_call(
        functools.partial(_mlp_sample_body, final=final),
        grid=(n_chunks,),
        in_specs=[_const_spec((m, D_MODEL)), _const_spec((1, D_MODEL)),
                  pl.BlockSpec((None, D_MODEL, D_MODEL), lambda c: (layer, 0, c)),
                  pl.BlockSpec((None, D_MODEL, D_MODEL), lambda c: (layer, c, 0)),
                  _const_spec((1, D_MODEL))],
        out_specs=[_const_spec((m, D_MODEL)), pl.BlockSpec((D_MODEL, D_MODEL), lambda c: (0, c)),
                   pl.BlockSpec((D_MODEL, D_MODEL), lambda c: (c, 0))],
        out_shape=[jax.ShapeDtypeStruct((m, D_MODEL), F32), jax.ShapeDtypeStruct((D_MODEL, D_FF), BF16),
                   jax.ShapeDtypeStruct((D_FF, D_MODEL), BF16)],
        scratch_shapes=[pltpu.VMEM((m, D_MODEL), F32)],
        compiler_params=_params(1),
        name="mlp_sample",
    )(x2d, norm2.reshape(1, D_MODEL), w_up, w_down, norm_f.reshape(1, D_MODEL))


def _sample_mix_body(q_ref, k_ref, v_ref, c0_hbm, c1_hbm, c2_hbm, zrow_ref, s_ref, ya_ref, yc_ref, s_out_ref,
                     buf0, buf1, buf2, sem, *, layer, steps):
    i = pl.program_id(0)
    per = q_ref.shape[1]
    hbm, bufs = (c0_hbm, c1_hbm, c2_hbm), (buf0, buf1, buf2)

    def copy(g, block, slot):
        return pltpu.make_async_copy(hbm[g].at[layer, pl.ds(block * per, per)], bufs[g].at[slot], sem.at[g, slot])

    @pl.when(i == 0)
    def _():
        for j in range(min(STREAM_BUFFERS, steps)):
            for g in range(3):
                copy(g, j, j).start()

    slot = i % STREAM_BUFFERS
    for g in range(3):
        copy(g, i, slot).wait()
    for s in range(per):
        _sample_mix_one(q_ref.at[:, s], k_ref.at[:, s], v_ref.at[:, s], buf0.at[slot, s], buf1.at[slot, s],
                        buf2.at[slot, s], zrow_ref.at[s], s_ref.at[s], ya_ref.at[s], yc_ref.at[s], s_out_ref.at[s])

    @pl.when(i + STREAM_BUFFERS < steps)
    def _():
        for g in range(3):
            copy(g, i + STREAM_BUFFERS, slot).start()


def _sample_mix_one(q_ref, k_ref, v_ref, c0_ref, c1_ref, c2_ref, zrow_ref, s_ref, ya_ref, yc_ref, s_out_ref):
    lane = lax.broadcasted_iota(jnp.int32, (8, GROUP_W), 1)
    rowi = lax.broadcasted_iota(jnp.int32, (8, GROUP_W), 0)
    head_of_row = (lane // HEAD_DIM) == rowi
    head0 = lax.broadcasted_iota(jnp.int32, (1, GROUP_W), 1) < HEAD_DIM

    outs, lses = [], []
    for g, (c_ref, (win, dil)) in enumerate(zip((c0_ref, c1_ref, c2_ref), DSWA_GROUPS)):
        q, k_new, v_new = q_ref[g], k_ref[g], v_ref[g]
        buf = c_ref[...]
        pos = lax.broadcasted_iota(jnp.int32, (8, win), 1)
        q2 = jnp.where(head_of_row, jnp.broadcast_to(q, (8, GROUP_W)), 0.0)
        scale = HEAD_DIM ** -0.5
        s = jnp.where(pos % dil == 0, _dot(q2, buf[0:GROUP_W, :]) * scale, NEG_BIG)
        s_self = jnp.sum(q2 * k_new, axis=-1, keepdims=True) * scale
        m = jnp.maximum(jnp.max(s, axis=-1, keepdims=True), s_self)
        p, p_self = jnp.exp(s - m), jnp.exp(s_self - m)
        l = jnp.sum(p, axis=-1, keepdims=True) + p_self
        o = (_dot_nt(p, buf[GROUP_W:2 * GROUP_W, :]) + p_self * v_new) / l
        lse = m + jnp.log(l)
        outs.append(jnp.where(head0, o[0:1], o[1:2]))
        lses.append(jnp.where(head0, jnp.broadcast_to(lse[0:1], (1, GROUP_W)),
                              jnp.broadcast_to(lse[1:2], (1, GROUP_W))))
    m = jnp.maximum(jnp.maximum(lses[0], lses[1]), lses[2])
    e = [jnp.exp(x - m) for x in lses]
    ya_ref[...] = (e[0] * outs[0] + e[1] * outs[1] + e[2] * outs[2]) / (e[0] + e[1] + e[2])

    zrow = zrow_ref[...]
    q_col, log_f, k_col = (_as_column(zrow[:, j * C_WIDTH:(j + 1) * C_WIDTH])[:, 0:C_DK] for j in range(3))
    v_row = zrow[:, 3 * C_WIDTH:4 * C_WIDTH]
    gate_row = zrow[:, 4 * C_WIDTH:5 * C_WIDTH]
    v_sel = jnp.concatenate([jnp.broadcast_to(v_row[:, h * C_DK:(h + 1) * C_DK], (C_DK, C_DK))
                             for h in range(C_HEADS)],
                            axis=0)
    s_new = jnp.exp(log_f) * s_ref[...] + k_col * v_sel
    s_out_ref[...] = s_new
    qs = q_col * s_new
    o4 = jnp.concatenate([jnp.sum(qs[h * C_DK:(h + 1) * C_DK, :], axis=0, keepdims=True) for h in range(C_HEADS)],
                         axis=0)
    gate4 = jnp.concatenate([gate_row[:, h * C_DK:(h + 1) * C_DK] for h in range(C_HEADS)], axis=0)
    o4 = o4 * lax.rsqrt(jnp.mean(o4 * o4, axis=-1, keepdims=True) + EPS) * gate4
    yc_ref[...] = o4


def _sample_mix(q, k, v, caches, zc, state, layer, new_state):
    n = q.shape[1]
    per = max(p for p in (4, 2, 1) if n % p == 0)
    row3 = lambda w: pl.BlockSpec((per, 1, w), lambda b: (b, 0, 0))
    grp = pl.BlockSpec((3, per, 1, GROUP_W), lambda b: (0, b, 0, 0))
    steps = n // per
    state_spec = pl.BlockSpec((None, per, C_WIDTH, C_DK), lambda b: (layer, b, 0, 0))
    in_specs = [grp] * 3 + [pl.BlockSpec(memory_space=pl.ANY)] * 3 + [row3(5 * C_WIDTH), state_spec]
    args = [q.reshape(3, n, 1, GROUP_W), k.reshape(3, n, 1, GROUP_W), v.reshape(3, n, 1, GROUP_W),
            *caches, zc.reshape(n, 1, 5 * C_WIDTH), state]
    n_in = len(args)
    aliases = {}
    if new_state is not None:
        aliases = {n_in: 2}
        in_specs = in_specs + [pl.BlockSpec(memory_space=pl.ANY)]
        args = args + [new_state]

    def body(*refs):
        _sample_mix_body(*refs[:n_in], *refs[len(args):], layer=layer, steps=steps)

    return pl.pallas_call(
        body,
        grid=(steps,),
        in_specs=in_specs,
        out_specs=[pl.BlockSpec((per, 1, GROUP_W), lambda b: (b, 0, 0)),
                   pl.BlockSpec((per, C_HEADS, C_DK), lambda b: (b, 0, 0)), state_spec],
        out_shape=[jax.ShapeDtypeStruct((n, 1, GROUP_W), F32), jax.ShapeDtypeStruct((n, C_HEADS, C_DK), F32),
                   jax.ShapeDtypeStruct(state.shape, F32)],
        scratch_shapes=[pltpu.VMEM((STREAM_BUFFERS, per, 2 * GROUP_W, win), F32) for win, _ in DSWA_GROUPS]
        + [pltpu.SemaphoreType.DMA((3, STREAM_BUFFERS))],
        input_output_aliases=aliases,
        compiler_params=_params(1),
        name="sample_mix",
    )(*args)


def kernel(x_prompt, x_sample, cache_kv_w128, cache_kv_w512, cache_kv_w2048, state_conv, state_hgrn, w_in, b_gate,
           norm1, conv_w, hgrn_lb, hgrn_norm, w_pa, w_pb, w_pc, w_o, norm2, w_up, w_down, norm_f):
    bp, tp, d = x_prompt.shape
    bs, ts, _ = x_sample.shape
    depth = w_in.shape[0]
    assert d == D_MODEL and ts == 1 and depth == 2
    assert PAST_LEN >= max(win for win, _ in DSWA_GROUPS)

    sm = jax.nn.softmax(hgrn_lb.astype(F32), axis=0)
    lower = jnp.cumsum(sm, axis=0) - sm[0:1]

    cos_p, sin_p = _rope_tables(jnp.arange(tp, dtype=jnp.int32))
    cos_s, sin_s = _rope_tables(jnp.full((bs,), PAST_LEN, dtype=jnp.int32))

    caches = [jnp.transpose(c, (0, 1, 3, 4, 5, 2)).reshape(depth, bs, 2 * GROUP_W, win)
              for c, (win, _) in zip((cache_kv_w128, cache_kv_w512, cache_kv_w2048), DSWA_GROUPS)]
    state = state_hgrn.reshape(depth, bs, C_WIDTH, C_DK)

    xp = x_prompt.reshape(bp * tp, d)
    xs = x_sample.reshape(bs, d)
    kv_p, kv_s = None, None
    conv_p, hgrn_p, conv_s, hgrn_s = [], [], [], None
    for l in range(depth):
        final = l == depth - 1

        q, k_s, v_s, zc, zb, zg, w_in_bf = _sample_inproj(xs, norm1[l], w_in, l, cos_s, sin_s, lower[l],
                                                          hgrn_norm[l])
        ya, yc, hgrn_s = _sample_mix(q, k_s, v_s, caches, zc, state, l, hgrn_s)
        x1, u, wpa, wpb, wpc, wo = _merge_sample(xs, ya.reshape(bs, GROUP_W), yc.reshape(bs, C_WIDTH), zb, zg,
                                                 state_conv[l, :, 0], state_conv[l, :, 1], b_gate[l], conv_w[l],
                                                 l, w_pa, w_pb, w_pc, w_o)
        xs, wup, wdn = _mlp_sample(x1, norm2[l], l, w_up, w_down, norm_f, final)
        conv_s.append(jnp.stack([state_conv[l, :, 1], u], axis=1))

        q, k, v, zc, zl = _inproj(xp, norm1[l], w_in_bf, cos_p, sin_p, lower[l], hgrn_norm[l], tm=2 * ROW_TILE)
        ya, *kv_p = _attn_prompt(q.reshape(3, bp, tp, -1), k.reshape(3, bp, tp, -1), v.reshape(3, bp, tp, -1),
                                 l, depth, kv_p)
        yc, st = _hgrn_prompt(zc.reshape(bp, tp, -1), zl.reshape(bp, tp, -1))
        x1, nc = _merge_prompt(xp.reshape(bp, tp, d), ya, yc, norm1[l], w_in_bf,
                               b_gate[l], conv_w[l], wpa, wpb, wpc, wo)
        xp, *kv_s = _mlp(x1.reshape(bp * tp, d), norm2[l], wup, wdn, norm_f, final, ROW_TILE,
                         k_s, v_s, caches, l, kv_s)
        conv_p.append(nc)
        hgrn_p.append(st)

    def kv6(c, n):
        return jnp.transpose(c.reshape(depth, n, 2, 2, HEAD_DIM, c.shape[3]), (0, 1, 5, 2, 3, 4))

    return (xp.reshape(bp, tp, d), xs.reshape(bs, ts, d),
            kv6(kv_p[0], bp), kv6(kv_p[1], bp), kv6(kv_p[2], bp), jnp.stack(conv_p), jnp.stack(hgrn_p),
            kv6(kv_s[0], bs), kv6(kv_s[1], bs), kv6(kv_s[2], bs), jnp.stack(conv_s),
            hgrn_s.reshape(depth, bs, C_HEADS, C_DK, C_DK))
```
